```python
import math
import jax
import jax.numpy as jnp
from jax import lax
import numpy as np

D_MODEL = 1024
BATCH = 32
SEQ = 2048
DEPTH = 4

HEAD_DIM = 64
N_Q_HEADS = 8
N_KV_HEADS = 2
GROUP = N_Q_HEADS // N_KV_HEADS
WINDOW = 128
ATTN_BLOCK = WINDOW
Q_WIDTH = N_Q_HEADS * HEAD_DIM
KV_WIDTH = N_KV_HEADS * HEAD_DIM
NUM_BUCKETS = 32
MAX_DISTANCE = 128
CONV_CHANNELS = D_MODEL // 2
CONV_WIDTH = 31
EVEN_IN = Q_WIDTH + 2 * KV_WIDTH + 2 * CONV_CHANNELS
EVEN_CAT = Q_WIDTH + CONV_CHANNELS
LRU_WIDTH = D_MODEL
LRU_HEADS = 8
LRU_BLOCK = LRU_WIDTH // LRU_HEADS
LRU_CONV_WIDTH = 4
RG_LRU_C = 8.0
D_FF = 2816
RMS_EPS = 1e-6
LN_EPS = 1e-5
NEG_INF = -1e30
N_EVEN = (DEPTH + 1) // 2
N_ODD = DEPTH // 2

kernel_name = "hybrid_swa_conformer_rglru_macaron"


def _rmsnorm(x, g):
    xf = x.astype(jnp.float32)
    y = xf * lax.rsqrt(jnp.mean(xf * xf, axis=-1, keepdims=True) + RMS_EPS)
    return (y * g.astype(jnp.float32)).astype(x.dtype)


def _layernorm(x, g, b):
    xf = x.astype(jnp.float32)
    mu = jnp.mean(xf, axis=-1, keepdims=True)
    var = jnp.mean(jnp.square(xf - mu), axis=-1, keepdims=True)
    y = (xf - mu) * lax.rsqrt(var + LN_EPS)
    return (y * g.astype(jnp.float32) + b.astype(jnp.float32)).astype(x.dtype)


def _swiglu(x, wg, wu, wd):
    return (jax.nn.silu(x @ wg) * (x @ wu)) @ wd


def _causal_depthwise_conv(x, w, b):
    k_width, chans = w.shape
    y = lax.conv_general_dilated(
        x, w[:, None, :].astype(x.dtype), window_strides=(1,),
        padding=[(k_width - 1, 0)], dimension_numbers=("NWC", "WIO", "NWC"),
        feature_group_count=chans)
    return y + b.astype(x.dtype)


def _t5_bucket(dist):
    n = jnp.maximum(dist, 0)
    max_exact = NUM_BUCKETS // 2
    nf = jnp.maximum(n, max_exact).astype(jnp.float32)
    large = max_exact + (jnp.log(nf / max_exact) / math.log(MAX_DISTANCE / max_exact)
                         * (NUM_BUCKETS - max_exact)).astype(jnp.int32)
    large = jnp.minimum(large, NUM_BUCKETS - 1)
    return jnp.where(n < max_exact, n, large)


def _swa_sink_attention(q, k, v, sinks, rel_bias):
    bsz, seq = q.shape[:2]
    nb = seq // ATTN_BLOCK
    qb = q.reshape(bsz, nb, ATTN_BLOCK, N_KV_HEADS, GROUP, HEAD_DIM)
    kb = k.reshape(bsz, nb, ATTN_BLOCK, N_KV_HEADS, HEAD_DIM)
    vb = v.reshape(bsz, nb, ATTN_BLOCK, N_KV_HEADS, HEAD_DIM)
    kk = jnp.concatenate([jnp.concatenate([jnp.zeros_like(kb[:, :1]), kb[:, :-1]], axis=1), kb], axis=2)
    vv = jnp.concatenate([jnp.concatenate([jnp.zeros_like(vb[:, :1]), vb[:, :-1]], axis=1), vb], axis=2)
    scores = jnp.einsum("bnqhgd,bnshd->bnhgqs", qb, kk,
                        preferred_element_type=jnp.float32) * (1.0 / math.sqrt(HEAD_DIM))
    qi = jnp.arange(ATTN_BLOCK)[:, None]
    sj = jnp.arange(2 * ATTN_BLOCK)[None, :]
    dist = qi + ATTN_BLOCK - sj
    bias = rel_bias.astype(jnp.float32)[_t5_bucket(dist)]
    bias = jnp.transpose(bias, (2, 0, 1)).reshape(N_KV_HEADS, GROUP, ATTN_BLOCK, 2 * ATTN_BLOCK)
    in_window = (dist >= 0) & (dist < WINDOW)
    key_pos = jnp.arange(nb)[:, None, None] * ATTN_BLOCK + sj[None] - ATTN_BLOCK
    mask = in_window[None] & (key_pos >= 0)
    scores = jnp.where(mask[None, :, None, None], scores + bias[None, None], NEG_INF)
    sink = sinks.astype(jnp.float32).reshape(N_KV_HEADS, GROUP)[None, None, :, :, None, None]
    sink = jnp.broadcast_to(sink, scores.shape[:-1] + (1,))
    probs = jax.nn.softmax(jnp.concatenate([scores, sink], axis=-1), axis=-1)[..., :-1]
    out = jnp.einsum("bnhgqs,bnshd->bnqhgd", probs.astype(v.dtype), vv)
    return out.reshape(bsz, seq, Q_WIDTH)


def _attn_conv_mixer(h, w_in, sinks, conv_w, conv_b, ln_g, ln_b, w_out, rel_bias):
    bsz, seq, _ = h.shape
    u = h @ w_in
    o1 = Q_WIDTH
    o2 = o1 + KV_WIDTH
    o3 = o2 + KV_WIDTH
    o4 = o3 + CONV_CHANNELS
    q = u[..., :o1].reshape(bsz, seq, N_Q_HEADS, HEAD_DIM)
    k = u[..., o1:o2].reshape(bsz, seq, N_KV_HEADS, HEAD_DIM)
    v = u[..., o2:o3].reshape(bsz, seq, N_KV_HEADS, HEAD_DIM)
    attn = _swa_sink_attention(q, k, v, sinks, rel_bias)
    glu = u[..., o3:o4] * jax.nn.sigmoid(u[..., o4:])
    c = jax.nn.silu(_layernorm(_causal_depthwise_conv(glu, conv_w, conv_b), ln_g, ln_b))
    return jnp.concatenate([attn, c], axis=-1) @ w_out


def _rg_lru(x, ga_w, ga_b, gx_w, gx_b, lam):
    bsz, seq, width = x.shape
    xh = x.reshape(bsz, seq, LRU_HEADS, LRU_BLOCK)
    r = jax.nn.sigmoid(jnp.einsum("bshi,hij->bshj", xh, ga_w).reshape(bsz, seq, width) + ga_b)
    i = jax.nn.sigmoid(jnp.einsum("bshi,hij->bshj", xh, gx_w).reshape(bsz, seq, width) + gx_b)
    log_a = RG_LRU_C * r.astype(jnp.float32) * jax.nn.log_sigmoid(lam.astype(jnp.float32))
    a = jnp.exp(log_a)
    bx = jnp.sqrt(-jnp.expm1(2.0 * log_a)) * (i * x).astype(jnp.float32)

    def combine(left, right):
        a1, b1 = left
        a2, b2 = right
        return a1 * a2, a2 * b1 + b2

    _, hs = lax.associative_scan(combine, (a, bx), axis=1)
    return hs.astype(x.dtype)


def _recurrent_mixer(h, w_in, conv_w, conv_b, ga_w, ga_b, gx_w, gx_b, lam, w_out):
    u = h @ w_in
    gate = jax.nn.gelu(u[..., :LRU_WIDTH])
    rec = _causal_depthwise_conv(u[..., LRU_WIDTH:], conv_w, conv_b)
    rec = _rg_lru(rec, ga_w, ga_b, gx_w, gx_b, lam)
    return (gate * rec) @ w_out


def setup_inputs(seed: int = 0) -> dict:
    key = jax.random.key(seed)
    ks = iter(jax.random.split(key, 40))

    def nrm(shape, scale):
        return jax.random.normal(next(ks), shape, jnp.float32) * scale

    def gain(shape):
        return 1.0 + nrm(shape, 0.02)

    x = nrm((BATCH, SEQ, D_MODEL), 1.0)
    u = jax.random.uniform(next(ks), (N_ODD, LRU_WIDTH), jnp.float32, 0.9, 0.999)
    a0 = u ** (1.0 / RG_LRU_C)
    lru_lambda = jnp.log(a0) - jnp.log1p(-a0)
    return {
        "x": x,
        "norm_ffn1": gain((DEPTH, D_MODEL)),
        "ffn1_wg": nrm((DEPTH, D_MODEL, D_FF), D_MODEL ** -0.5),
        "ffn1_wu": nrm((DEPTH, D_MODEL, D_FF), D_MODEL ** -0.5),
        "ffn1_wd": nrm((DEPTH, D_FF, D_MODEL), D_FF ** -0.5),
        "norm_mix": gain((DEPTH, D_MODEL)),
        "norm_ffn2": gain((DEPTH, D_MODEL)),
        "ffn2_wg": nrm((DEPTH, D_MODEL, D_FF), D_MODEL ** -0.5),
        "ffn2_wu": nrm((DEPTH, D_MODEL, D_FF), D_MODEL ** -0.5),
        "ffn2_wd": nrm((DEPTH, D_FF, D_MODEL), D_FF ** -0.5),
        "rel_bias": nrm((NUM_BUCKETS, N_Q_HEADS), 0.3),
        "even_w_in": nrm((N_EVEN, D_MODEL, EVEN_IN), D_MODEL ** -0.5),
        "attn_sinks": nrm((N_EVEN, N_Q_HEADS), 1.0),
        "conv_b_w": nrm((N_EVEN, CONV_WIDTH, CONV_CHANNELS), CONV_WIDTH ** -0.5),
        "conv_b_b": nrm((N_EVEN, CONV_CHANNELS), 0.02),
        "conv_ln_g": gain((N_EVEN, CONV_CHANNELS)),
        "conv_ln_b": nrm((N_EVEN, CONV_CHANNELS), 0.02),
        "even_w_out": nrm((N_EVEN, EVEN_CAT, D_MODEL), EVEN_CAT ** -0.5),
        "odd_w_in": nrm((N_ODD, D_MODEL, 2 * LRU_WIDTH), D_MODEL ** -0.5),
        "lru_conv_w": nrm((N_ODD, LRU_CONV_WIDTH, LRU_WIDTH), LRU_CONV_WIDTH ** -0.5),
        "lru_conv_b": nrm((N_ODD, LRU_WIDTH), 0.02),
        "gate_a_w": nrm((N_ODD, LRU_HEADS, LRU_BLOCK, LRU_BLOCK), LRU_BLOCK ** -0.5),
        "gate_a_b": nrm((N_ODD, LRU_WIDTH), 0.02),
        "gate_x_w": nrm((N_ODD, LRU_HEADS, LRU_BLOCK, LRU_BLOCK), LRU_BLOCK ** -0.5),
        "gate_x_b": nrm((N_ODD, LRU_WIDTH), 0.02),
        "lru_lambda": lru_lambda,
        "odd_w_out": nrm((N_ODD, LRU_WIDTH, D_MODEL), LRU_WIDTH ** -0.5),
        "norm_final": gain((D_MODEL,)),
    }


def reference(x, norm_ffn1, ffn1_wg, ffn1_wu, ffn1_wd, norm_mix, norm_ffn2, ffn2_wg, ffn2_wu, ffn2_wd,
              rel_bias, even_w_in, attn_sinks, conv_b_w, conv_b_b, conv_ln_g, conv_ln_b, even_w_out,
              odd_w_in, lru_conv_w, lru_conv_b, gate_a_w, gate_a_b, gate_x_w, gate_x_b, lru_lambda,
              odd_w_out, norm_final):
    h = x
    for layer in range(DEPTH):
        h = h + 0.5 * _swiglu(_rmsnorm(h, norm_ffn1[layer]), ffn1_wg[layer], ffn1_wu[layer], ffn1_wd[layer])
        hn = _rmsnorm(h, norm_mix[layer])
        if layer % 2 == 0:
            e = layer // 2
            h = h + _attn_conv_mixer(hn, even_w_in[e], attn_sinks[e], conv_b_w[e], conv_b_b[e],
                                     conv_ln_g[e], conv_ln_b[e], even_w_out[e], rel_bias)
        else:
            o = layer // 2
            h = h + _recurrent_mixer(hn, odd_w_in[o], lru_conv_w[o], lru_conv_b[o], gate_a_w[o],
                                     gate_a_b[o], gate_x_w[o], gate_x_b[o], lru_lambda[o], odd_w_out[o])
        h = h + 0.5 * _swiglu(_rmsnorm(h, norm_ffn2[layer]), ffn2_wg[layer], ffn2_wu[layer], ffn2_wd[layer])
    return _rmsnorm(h, norm_final)
```

```python
import functools
import math

import jax
import jax.numpy as jnp
import numpy as np
from jax import lax
from jax.experimental import pallas as pl
from jax.experimental.pallas import tpu as pltpu

D_MODEL = 1024
D_FF = 2816
HEAD_DIM = 64
N_Q_HEADS = 8
N_KV_HEADS = 2
WINDOW = 128
Q_WIDTH = N_Q_HEADS * HEAD_DIM
KV_WIDTH = N_KV_HEADS * HEAD_DIM
NUM_BUCKETS = 32
MAX_DISTANCE = 128
CONV_CHANNELS = 512
CONV_WIDTH = 31
EVEN_IN = Q_WIDTH + 2 * KV_WIDTH + 2 * CONV_CHANNELS
LRU_WIDTH = 1024
LRU_HEADS = 8
LRU_BLOCK = 128
LRU_CONV_WIDTH = 4
RG_LRU_C = 8.0
RMS_EPS = 1e-6
LN_EPS = 1e-5
NEG_INF = -1e30

LANES = 128
SUBLANES = 8
MXU_DIM = 256
VMEM_LIMIT_BYTES = 56 * 1024 * 1024

FF_CHUNK = MXU_DIM
N_FF_CHUNKS = D_FF // FF_CHUNK
FFN_ROWS = 512
MIX_ROWS = 512
ROW_CHUNK = 64
CONV_HALO = 32
LRU_HALO = 8

BF16 = jnp.bfloat16
F32 = jnp.float32


def _resident(shape):
    nd = len(shape)
    return pl.BlockSpec(shape, lambda *_: (0,) * nd, pipeline_mode=pl.Buffered(1))


def _rmsnorm_rows(x, g):
    ms = jnp.mean(x * x, axis=-1, keepdims=True)
    return x * lax.rsqrt(ms + RMS_EPS) * g


def _ffn_kernel(x_ref, g_ref, wgu_ref, wd_ref, o_ref, xn_scr, a_scr):
    rows = x_ref.shape[0]

    def norm_body(i, carry):
        r = pl.ds(pl.multiple_of(i * ROW_CHUNK, ROW_CHUNK), ROW_CHUNK)
        xn_scr[r, :] = _rmsnorm_rows(x_ref[r, :], g_ref[...]).astype(BF16)
        return carry

    lax.fori_loop(0, rows // ROW_CHUNK, norm_body, 0)
    xn = xn_scr[...]
    for c in range(N_FF_CHUNKS):
        gu = jnp.dot(xn, wgu_ref[c], preferred_element_type=F32)
        g = gu[:, :FF_CHUNK]
        u = gu[:, FF_CHUNK:]
        a_scr[:, c * FF_CHUNK:(c + 1) * FF_CHUNK] = (g * jax.nn.sigmoid(g) * u).astype(BF16)
    y = jnp.dot(a_scr[...], wd_ref[...], preferred_element_type=F32)
    o_ref[...] = x_ref[...] + 0.5 * y


def _ffn_call(h2d, gain, wgu, wd):
    n = h2d.shape[0]
    return pl.pallas_call(
        _ffn_kernel,
        out_shape=jax.ShapeDtypeStruct(h2d.shape, F32),
        grid=(n // FFN_ROWS,),
        in_specs=[
            pl.BlockSpec((FFN_ROWS, D_MODEL), lambda i: (i, 0)),
            _resident((1, D_MODEL)),
            _resident(wgu.shape),
            _resident(wd.shape),
        ],
        out_specs=pl.BlockSpec((FFN_ROWS, D_MODEL), lambda i: (i, 0)),
        scratch_shapes=[
            pltpu.VMEM((FFN_ROWS, D_MODEL), BF16),
            pltpu.VMEM((FFN_ROWS, D_FF), BF16),
        ],
        compiler_params=pltpu.CompilerParams(
            dimension_semantics=("parallel",), vmem_limit_bytes=VMEM_LIMIT_BYTES),
        name="ffn",
    )(h2d, gain, wgu, wd)


def _even_kernel(sinks_ref, h_ref, g_ref, win_ref, bias_ref, cw_ref, cb_ref, lng_ref, lnb_ref,
                 wout_ref, o_ref, hn_scr, u_scr, kv_scr, xpad_scr, cat_scr):
    rows = h_ref.shape[0]
    s = pl.program_id(1)

    @pl.when(s == 0)
    def _():
        kv_scr[0:WINDOW, :] = jnp.zeros((WINDOW, 2 * KV_WIDTH), BF16)
        xpad_scr[0:CONV_HALO, :] = jnp.zeros((CONV_HALO, CONV_CHANNELS), F32)

    def norm_body(i, carry):
        r = pl.ds(pl.multiple_of(i * ROW_CHUNK, ROW_CHUNK), ROW_CHUNK)
        hn_scr[r, :] = _rmsnorm_rows(h_ref[r, :], g_ref[...]).astype(BF16)
        return carry

    lax.fori_loop(0, rows // ROW_CHUNK, norm_body, 0)
    u_scr[...] = jnp.dot(hn_scr[...], win_ref[...], preferred_element_type=F32)

    o_k = Q_WIDTH
    o_g = Q_WIDTH + 2 * KV_WIDTH
    kv_scr[WINDOW:, :] = u_scr[:, o_k:o_g].astype(BF16)

    def glu_body(i, carry):
        r = pl.multiple_of(i * ROW_CHUNK, ROW_CHUNK)
        a = u_scr[pl.ds(r, ROW_CHUNK), o_g:o_g + CONV_CHANNELS]
        b = u_scr[pl.ds(r, ROW_CHUNK), o_g + CONV_CHANNELS:]
        xpad_scr[pl.ds(CONV_HALO + r, ROW_CHUNK), :] = a * jax.nn.sigmoid(b)
        return carry

    lax.fori_loop(0, rows // ROW_CHUNK, glu_body, 0)

    lane = lax.broadcasted_iota(jnp.int32, (2 * WINDOW, LANES), 1)
    qlane = lax.broadcasted_iota(jnp.int32, (WINDOW, LANES), 1)
    scale = 1.0 / math.sqrt(HEAD_DIM)
    for j in range(rows // WINDOW):
        first = (s == 0).astype(jnp.int32) if j == 0 else 0
        kvb = kv_scr[j * WINDOW:(j + 2) * WINDOW, :]
        k_cat = kvb[:, :KV_WIDTH].astype(F32)
        v_cat = kvb[:, KV_WIDTH:].astype(F32)
        k_sw = pltpu.roll(k_cat, HEAD_DIM, axis=1)
        v_sw = pltpu.roll(v_cat, HEAD_DIM, axis=1)
        k_dup = (jnp.where(lane < HEAD_DIM, k_cat, k_sw).astype(BF16),
                 jnp.where(lane < HEAD_DIM, k_sw, k_cat).astype(BF16))
        v_dup = (jnp.where(lane < HEAD_DIM, v_cat, v_sw).astype(BF16),
                 jnp.where(lane < HEAD_DIM, v_sw, v_cat).astype(BF16))
        for c in range(Q_WIDTH // LANES):
            kvh = (2 * c) // (N_Q_HEADS // N_KV_HEADS)
            qcol = u_scr[j * WINDOW:(j + 1) * WINDOW, c * LANES:(c + 1) * LANES] * scale
            outs = []
            for half in range(2):
                head = 2 * c + half
                keep = (qlane < HEAD_DIM) if half == 0 else (qlane >= HEAD_DIM)
                q = jnp.where(keep, qcol, 0.0).astype(BF16)
                sc = lax.dot_general(q, k_dup[kvh], (((1,), (1,)), ((), ())),
                                     preferred_element_type=F32)
                sc = sc + bias_ref[first, head]
                sink = sinks_ref[head]
                m = jnp.maximum(jnp.max(sc, axis=-1, keepdims=True), sink)
                e = jnp.exp(sc - m)
                den = jnp.sum(e, axis=-1, keepdims=True) + jnp.exp(sink - m)
                p = (e / den).astype(BF16)
                outs.append(jnp.dot(p, v_dup[kvh], preferred_element_type=F32))
            attn = jnp.where(qlane < HEAD_DIM, outs[0], outs[1])
            cat_scr[j * WINDOW:(j + 1) * WINDOW, c * LANES:(c + 1) * LANES] = attn.astype(BF16)

    for i in range(rows // ROW_CHUNK):
        base = CONV_HALO + i * ROW_CHUNK - (CONV_WIDTH - 1)
        acc = jnp.zeros((ROW_CHUNK, CONV_CHANNELS), F32)
        for k in range(CONV_WIDTH):
            acc = acc + xpad_scr[base + k:base + k + ROW_CHUNK, :] * cw_ref[k:k + 1, :]
        y = acc + cb_ref[...]
        mu = jnp.mean(y, axis=-1, keepdims=True)
        yc = y - mu
        var = jnp.mean(yc * yc, axis=-1, keepdims=True)
        z = yc * lax.rsqrt(var + LN_EPS) * lng_ref[...] + lnb_ref[...]
        cat_scr[i * ROW_CHUNK:(i + 1) * ROW_CHUNK, Q_WIDTH:] = (z * jax.nn.sigmoid(z)).astype(BF16)

    o_ref[...] = h_ref[...] + jnp.dot(cat_scr[...], wout_ref[...], preferred_element_type=F32)

    kv_scr[0:WINDOW, :] = kv_scr[rows:rows + WINDOW, :]
    xpad_scr[0:CONV_HALO, :] = xpad_scr[rows:rows + CONV_HALO, :]


def _even_call(h, gain, w_in, sinks, bias_tbl, conv_w, conv_b, ln_g, ln_b, w_out):
    bsz, seq, _ = h.shape
    tile = pl.BlockSpec((None, MIX_ROWS, D_MODEL), lambda b, s: (b, s, 0))
    return pl.pallas_call(
        _even_kernel,
        out_shape=jax.ShapeDtypeStruct(h.shape, F32),
        grid=(bsz, seq // MIX_ROWS),
        in_specs=[
            pl.BlockSpec(memory_space=pltpu.SMEM),
            tile,
            _resident((1, D_MODEL)),
            _resident(w_in.shape),
            _resident(bias_tbl.shape),
            _resident(conv_w.shape),
            _resident(conv_b.shape),
            _resident(ln_g.shape),
            _resident(ln_b.shape),
            _resident(w_out.shape),
        ],
        out_specs=tile,
        scratch_shapes=[
            pltpu.VMEM((MIX_ROWS, D_MODEL), BF16),
            pltpu.VMEM((MIX_ROWS, EVEN_IN), F32),
            pltpu.VMEM((WINDOW + MIX_ROWS, 2 * KV_WIDTH), BF16),
            pltpu.VMEM((CONV_HALO + MIX_ROWS, CONV_CHANNELS), F32),
            pltpu.VMEM((MIX_ROWS, D_MODEL), BF16),
        ],
        compiler_params=pltpu.CompilerParams(
            dimension_semantics=("parallel", "arbitrary"), vmem_limit_bytes=VMEM_LIMIT_BYTES),
        name="even_mixer",
    )(sinks, h, gain, w_in, bias_tbl, conv_w, conv_b, ln_g, ln_b, w_out)


def _odd_kernel(h_ref, g_ref, win_ref, cw_ref, cb_ref, wgate_ref, gab_ref, gxb_ref, lam_ref,
                wout_ref, o_ref, hn_scr, u_scr, xpad_scr, rec_scr, gpre_scr, a_scr, b_scr,
                y_scr, state_scr):
    rows = h_ref.shape[0]
    s = pl.program_id(1)

    @pl.when(s == 0)
    def _():
        xpad_scr[0:LRU_HALO, :] = jnp.zeros((LRU_HALO, LRU_WIDTH), F32)
        state_scr[...] = jnp.zeros((SUBLANES, LRU_WIDTH), F32)

    def norm_body(i, carry):
        r = pl.ds(pl.multiple_of(i * ROW_CHUNK, ROW_CHUNK), ROW_CHUNK)
        hn_scr[r, :] = _rmsnorm_rows(h_ref[r, :], g_ref[...]).astype(BF16)
        return carry

    lax.fori_loop(0, rows // ROW_CHUNK, norm_body, 0)
    u_scr[...] = jnp.dot(hn_scr[...], win_ref[...], preferred_element_type=F32)
    xpad_scr[LRU_HALO:, :] = u_scr[:, LRU_WIDTH:]

    for i in range(rows // ROW_CHUNK):
        base = LRU_HALO + i * ROW_CHUNK - (LRU_CONV_WIDTH - 1)
        acc = jnp.zeros((ROW_CHUNK, LRU_WIDTH), F32)
        for k in range(LRU_CONV_WIDTH):
            acc = acc + xpad_scr[base + k:base + k + ROW_CHUNK, :] * cw_ref[k:k + 1, :]
        rec_scr[i * ROW_CHUNK:(i + 1) * ROW_CHUNK, :] = acc + cb_ref[...]

    for hd in range(LRU_HEADS):
        xh = rec_scr[:, hd * LRU_BLOCK:(hd + 1) * LRU_BLOCK].astype(BF16)
        gpre_scr[:, hd * 2 * LRU_BLOCK:(hd + 1) * 2 * LRU_BLOCK] = jnp.dot(
            xh, wgate_ref[hd], preferred_element_type=F32)

    lam = lam_ref[...]
    log_sig_lam = -(jnp.maximum(-lam, 0.0) + jnp.log1p(jnp.exp(-jnp.abs(lam))))
    row8 = lax.broadcasted_iota(jnp.int32, (ROW_CHUNK // SUBLANES, SUBLANES, LRU_WIDTH), 1)

    for hd in range(LRU_HEADS):
        cs = slice(hd * LRU_BLOCK, (hd + 1) * LRU_BLOCK)

        def head_body(i, carry, hd=hd, cs=cs):
            r = pl.ds(pl.multiple_of(i * ROW_CHUNK, ROW_CHUNK), ROW_CHUNK)
            x = rec_scr[r, cs]
            ra = gpre_scr[r, hd * 2 * LRU_BLOCK:hd * 2 * LRU_BLOCK + LRU_BLOCK]
            ri = gpre_scr[r, hd * 2 * LRU_BLOCK + LRU_BLOCK:(hd + 1) * 2 * LRU_BLOCK]
            rg = jax.nn.sigmoid(ra + gab_ref[:, cs])
            ig = jax.nn.sigmoid(ri + gxb_ref[:, cs])
            log_a = RG_LRU_C * rg * log_sig_lam[:, cs]
            a = jnp.exp(log_a)
            bx = jnp.sqrt(-jnp.tanh(log_a) * (a * a + 1.0)) * (ig * x)
            a_scr[r, cs] = a
            b_scr[r, cs] = bx
            return carry

        lax.fori_loop(0, rows // ROW_CHUNK, head_body, 0)

    def scan_body(i, carry):
        r = pl.ds(pl.multiple_of(i * ROW_CHUNK, ROW_CHUNK), ROW_CHUNK)
        a3 = a_scr[r, :].reshape(ROW_CHUNK // SUBLANES, SUBLANES, LRU_WIDTH)
        b3 = b_scr[r, :].reshape(ROW_CHUNK // SUBLANES, SUBLANES, LRU_WIDTH)
        for d in (1, 2, 4):
            a_sh = jnp.where(row8 >= d, pltpu.roll(a3, d, axis=1), 1.0)
            b_sh = jnp.where(row8 >= d, pltpu.roll(b3, d, axis=1), 0.0)
            b3 = a3 * b_sh + b3
            a3 = a3 * a_sh
        a_scr[r, :] = a3.reshape(ROW_CHUNK, LRU_WIDTH)
        b_scr[r, :] = b3.reshape(ROW_CHUNK, LRU_WIDTH)
        return carry

    lax.fori_loop(0, rows // ROW_CHUNK, scan_body, 0)

    def carry_body(gidx, hprev):
        r = pl.ds(pl.multiple_of(gidx * SUBLANES, SUBLANES), SUBLANES)
        hs = a_scr[r, :] * hprev + b_scr[r, :]
        b_scr[r, :] = hs
        return jnp.broadcast_to(hs[SUBLANES - 1:SUBLANES, :], (SUBLANES, LRU_WIDTH))

    state_scr[...] = lax.fori_loop(0, rows // SUBLANES, carry_body, state_scr[...])

    def out_gate_body(i, carry):
        r = pl.ds(pl.multiple_of(i * ROW_CHUNK, ROW_CHUNK), ROW_CHUNK)
        y_scr[r, :] = (jax.nn.gelu(u_scr[r, 0:LRU_WIDTH]) * b_scr[r, :]).astype(BF16)
        return carry

    lax.fori_loop(0, rows // ROW_CHUNK, out_gate_body, 0)

    o_ref[...] = h_ref[...] + jnp.dot(y_scr[...], wout_ref[...], preferred_element_type=F32)
    xpad_scr[0:LRU_HALO, :] = xpad_scr[rows:rows + LRU_HALO, :]


def _odd_call(h, gain, w_in, conv_w, conv_b, w_gate, ga_b, gx_b, lam, w_out):
    bsz, seq, _ = h.shape
    tile = pl.BlockSpec((None, MIX_ROWS, D_MODEL), lambda b, s: (b, s, 0))
    return pl.pallas_call(
        _odd_kernel,
        out_shape=jax.ShapeDtypeStruct(h.shape, F32),
        grid=(bsz, seq // MIX_ROWS),
        in_specs=[
            tile,
            _resident((1, D_MODEL)),
            _resident(w_in.shape),
            _resident(conv_w.shape),
            _resident(conv_b.shape),
            _resident(w_gate.shape),
            _resident(ga_b.shape),
            _resident(gx_b.shape),
            _resident(lam.shape),
            _resident(w_out.shape),
        ],
        out_specs=tile,
        scratch_shapes=[
            pltpu.VMEM((MIX_ROWS, D_MODEL), BF16),
            pltpu.VMEM((MIX_ROWS, 2 * LRU_WIDTH), F32),
            pltpu.VMEM((LRU_HALO + MIX_ROWS, LRU_WIDTH), F32),
            pltpu.VMEM((MIX_ROWS, LRU_WIDTH), F32),
            pltpu.VMEM((MIX_ROWS, 2 * LRU_WIDTH), F32),
            pltpu.VMEM((MIX_ROWS, LRU_WIDTH), F32),
            pltpu.VMEM((MIX_ROWS, LRU_WIDTH), F32),
            pltpu.VMEM((MIX_ROWS, LRU_WIDTH), BF16),
            pltpu.VMEM((SUBLANES, LRU_WIDTH), F32),
        ],
        compiler_params=pltpu.CompilerParams(
            dimension_semantics=("parallel", "arbitrary"), vmem_limit_bytes=VMEM_LIMIT_BYTES),
        name="odd_mixer",
    )(h, gain, w_in, conv_w, conv_b, w_gate, ga_b, gx_b, lam, w_out)


def _final_norm_kernel(x_ref, g_ref, o_ref):
    o_ref[...] = _rmsnorm_rows(x_ref[...], g_ref[...])


def _final_norm_call(h2d, gain):
    n = h2d.shape[0]
    return pl.pallas_call(
        _final_norm_kernel,
        out_shape=jax.ShapeDtypeStruct(h2d.shape, F32),
        grid=(n // FFN_ROWS,),
        in_specs=[pl.BlockSpec((FFN_ROWS, D_MODEL), lambda i: (i, 0)), _resident((1, D_MODEL))],
        out_specs=pl.BlockSpec((FFN_ROWS, D_MODEL), lambda i: (i, 0)),
        compiler_params=pltpu.CompilerParams(dimension_semantics=("parallel",)),
        name="final_norm",
    )(h2d, gain)


def _t5_bucket_table():
    qi = np.arange(WINDOW)[:, None]
    sj = np.arange(2 * WINDOW)[None, :]
    dist = qi + WINDOW - sj
    n = np.maximum(dist, 0)
    max_exact = NUM_BUCKETS // 2
    nf = np.maximum(n, max_exact).astype(np.float32)
    large = max_exact + (np.log(nf / max_exact) / math.log(MAX_DISTANCE / max_exact)
                         * (NUM_BUCKETS - max_exact)).astype(np.int32)
    large = np.minimum(large, NUM_BUCKETS - 1)
    bucket = np.where(n < max_exact, n, large)
    in_window = (dist >= 0) & (dist < WINDOW)
    return bucket, in_window, sj


def _attention_bias_tables(rel_bias):
    bucket, in_window, sj = _t5_bucket_table()
    bias = jnp.transpose(rel_bias.astype(F32)[bucket], (2, 0, 1))
    normal = jnp.where(in_window[None], bias, NEG_INF)
    first = jnp.where((in_window & (sj >= WINDOW))[None], bias, NEG_INF)
    return jnp.stack([normal, first])


def _pack_ffn(wg, wu, wd):
    wg = wg.astype(BF16).reshape(D_MODEL, N_FF_CHUNKS, FF_CHUNK)
    wu = wu.astype(BF16).reshape(D_MODEL, N_FF_CHUNKS, FF_CHUNK)
    wgu = jnp.transpose(jnp.concatenate([wg, wu], axis=-1), (1, 0, 2))
    return wgu, wd.astype(BF16)


def kernel(x, norm_ffn1, ffn1_wg, ffn1_wu, ffn1_wd, norm_mix, norm_ffn2, ffn2_wg, ffn2_wu, ffn2_wd,
           rel_bias, even_w_in, attn_sinks, conv_b_w, conv_b_b, conv_ln_g, conv_ln_b, even_w_out,
           odd_w_in, lru_conv_w, lru_conv_b, gate_a_w, gate_a_b, gate_x_w, gate_x_b, lru_lambda,
           odd_w_out, norm_final):
    bsz, seq, d = x.shape
    depth = norm_ffn1.shape[0]
    bias_tbl = _attention_bias_tables(rel_bias)
    row = lambda v: v.reshape(1, -1)

    def ffn(h, gain, wg, wu, wd):
        wgu, wdb = _pack_ffn(wg, wu, wd)
        return _ffn_call(h.reshape(bsz * seq, d), row(gain), wgu, wdb).reshape(bsz, seq, d)

    h = x
    for layer in range(depth):
        h = ffn(h, norm_ffn1[layer], ffn1_wg[layer], ffn1_wu[layer], ffn1_wd[layer])
        if layer % 2 == 0:
            e = layer // 2
            h = _even_call(h, row(norm_mix[layer]), even_w_in[e].astype(BF16), attn_sinks[e],
                           bias_tbl, conv_b_w[e], row(conv_b_b[e]), row(conv_ln_g[e]),
                           row(conv_ln_b[e]), even_w_out[e].astype(BF16))
        else:
            o = layer // 2
            w_gate = jnp.concatenate([gate_a_w[o], gate_x_w[o]], axis=-1).astype(BF16)
            h = _odd_call(h, row(norm_mix[layer]), odd_w_in[o].astype(BF16), lru_conv_w[o],
                          row(lru_conv_b[o]), w_gate, row(gate_a_b[o]), row(gate_x_b[o]),
                          row(lru_lambda[o]), odd_w_out[o].astype(BF16))
        h = ffn(h, norm_ffn2[layer], ffn2_wg[layer], ffn2_wu[layer], ffn2_wd[layer])
    return _final_norm_call(h.reshape(bsz * seq, d), row(norm_final)).reshape(bsz, seq, d)
```

```python
import functools
import math

import jax
import jax.numpy as jnp
import numpy as np
from jax import lax
from jax.experimental import pallas as pl
from jax.experimental.pallas import tpu as pltpu

D_MODEL = 1024
D_FF = 2816
HEAD_DIM = 64
N_Q_HEADS = 8
N_KV_HEADS = 2
WINDOW = 128
Q_WIDTH = N_Q_HEADS * HEAD_DIM
KV_WIDTH = N_KV_HEADS * HEAD_DIM
NUM_BUCKETS = 32
MAX_DISTANCE = 128
CONV_CHANNELS = 512
CONV_WIDTH = 31
EVEN_IN = Q_WIDTH + 2 * KV_WIDTH + 2 * CONV_CHANNELS
LRU_WIDTH = 1024
LRU_HEADS = 8
LRU_BLOCK = 128
LRU_CONV_WIDTH = 4
RG_LRU_C = 8.0
RMS_EPS = 1e-6
LN_EPS = 1e-5
NEG_INF = -1e30

LANES = 128
SUBLANES = 8
MXU_DIM = 256
VMEM_LIMIT_BYTES = 56 * 1024 * 1024

FF_CHUNK = MXU_DIM
N_FF_CHUNKS = D_FF // FF_CHUNK
FFN_ROWS = 512
MIX_ROWS = 512
ROW_CHUNK = 64
CONV_HALO = 32
LRU_HALO = 8
SCAN_ROWS = 16

BF16 = jnp.bfloat16
F32 = jnp.float32


def _resident(shape):
    nd = len(shape)
    return pl.BlockSpec(shape, lambda *_: (0,) * nd, pipeline_mode=pl.Buffered(1))


def _sigmoid(x):
    return 0.5 * jnp.tanh(0.5 * x) + 0.5


def _rmsnorm_rows(x, g):
    ms = jnp.mean(x * x, axis=-1, keepdims=True)
    return x * lax.rsqrt(ms + RMS_EPS) * g


def _ffn_kernel(x_ref, xnext_ref, g_ref, wgu_ref, wd_ref, o_ref, xn_scr, a_scr):
    rows = x_ref.shape[0]
    i = pl.program_id(0)
    slot = lax.rem(i, 2)

    def norm_chunk(src_ref, dst_slot, k):
        r = slice(k * ROW_CHUNK, (k + 1) * ROW_CHUNK)
        xn_scr[dst_slot, r, :] = _rmsnorm_rows(src_ref[r, :], g_ref[...]).astype(BF16)

    @pl.when(i == 0)
    def _():
        for k in range(rows // ROW_CHUNK):
            norm_chunk(x_ref, 0, k)

    for c in range(N_FF_CHUNKS):
        gu = jnp.dot(xn_scr[slot], wgu_ref[c], preferred_element_type=F32)
        g = gu[:, :FF_CHUNK]
        u = gu[:, FF_CHUNK:]
        a_scr[:, c * FF_CHUNK:(c + 1) * FF_CHUNK] = (g * _sigmoid(g) * u).astype(BF16)
        if c < rows // ROW_CHUNK:
            norm_chunk(xnext_ref, 1 - slot, c)
    y = jnp.dot(a_scr[...], wd_ref[...], preferred_element_type=F32)
    o_ref[...] = x_ref[...] + 0.5 * y


def _ffn_call(h2d, gain, wgu, wd):
    n = h2d.shape[0]
    n_tiles = n // FFN_ROWS
    assert FFN_ROWS // ROW_CHUNK <= N_FF_CHUNKS
    return pl.pallas_call(
        _ffn_kernel,
        out_shape=jax.ShapeDtypeStruct(h2d.shape, F32),
        grid=(n_tiles,),
        in_specs=[
            pl.BlockSpec((FFN_ROWS, D_MODEL), lambda i: (i, 0)),
            pl.BlockSpec((FFN_ROWS, D_MODEL), lambda i: (jnp.minimum(i + 1, n_tiles - 1), 0)),
            _resident((1, D_MODEL)),
            _resident(wgu.shape),
            _resident(wd.shape),
        ],
        out_specs=pl.BlockSpec((FFN_ROWS, D_MODEL), lambda i: (i, 0)),
        scratch_shapes=[
            pltpu.VMEM((2, FFN_ROWS, D_MODEL), BF16),
            pltpu.VMEM((FFN_ROWS, D_FF), BF16),
        ],
        compiler_params=pltpu.CompilerParams(
            dimension_semantics=("arbitrary",), vmem_limit_bytes=VMEM_LIMIT_BYTES),
        name="ffn",
    )(h2d, h2d, gain, wgu, wd)


def _even_kernel(sinks_ref, h_ref, g_ref, win_ref, bias_ref, cw_ref, cb_ref, lng_ref, lnb_ref,
                 wout_ref, o_ref, hn_scr, u_scr, kv_scr, xpad_scr, cat_scr):
    rows = h_ref.shape[0]
    s = pl.program_id(1)

    @pl.when(s == 0)
    def _():
        kv_scr[0:WINDOW, :] = jnp.zeros((WINDOW, 2 * KV_WIDTH), BF16)
        xpad_scr[0:CONV_HALO, :] = jnp.zeros((CONV_HALO, CONV_CHANNELS), F32)

    def norm_body(i, carry):
        r = pl.ds(pl.multiple_of(i * ROW_CHUNK, ROW_CHUNK), ROW_CHUNK)
        hn_scr[r, :] = _rmsnorm_rows(h_ref[r, :], g_ref[...]).astype(BF16)
        return carry

    lax.fori_loop(0, rows // ROW_CHUNK, norm_body, 0)
    u_scr[...] = jnp.dot(hn_scr[...], win_ref[...], preferred_element_type=F32)

    o_k = Q_WIDTH
    o_g = Q_WIDTH + 2 * KV_WIDTH
    kv_scr[WINDOW:, :] = u_scr[:, o_k:o_g].astype(BF16)

    def glu_body(i, carry):
        r = pl.multiple_of(i * ROW_CHUNK, ROW_CHUNK)
        a = u_scr[pl.ds(r, ROW_CHUNK), o_g:o_g + CONV_CHANNELS]
        b = u_scr[pl.ds(r, ROW_CHUNK), o_g + CONV_CHANNELS:]
        xpad_scr[pl.ds(CONV_HALO + r, ROW_CHUNK), :] = a * _sigmoid(b)
        return carry

    lax.fori_loop(0, rows // ROW_CHUNK, glu_body, 0)

    lane = lax.broadcasted_iota(jnp.int32, (2 * WINDOW, LANES), 1)
    qlane = lax.broadcasted_iota(jnp.int32, (WINDOW, LANES), 1)
    scale = 1.0 / math.sqrt(HEAD_DIM)
    for j in range(rows // WINDOW):
        first = (s == 0).astype(jnp.int32) if j == 0 else 0
        kvb = kv_scr[j * WINDOW:(j + 2) * WINDOW, :]
        k_cat = kvb[:, :KV_WIDTH].astype(F32)
        v_cat = kvb[:, KV_WIDTH:].astype(F32)
        k_sw = pltpu.roll(k_cat, HEAD_DIM, axis=1)
        v_sw = pltpu.roll(v_cat, HEAD_DIM, axis=1)
        k_dup = (jnp.where(lane < HEAD_DIM, k_cat, k_sw).astype(BF16),
                 jnp.where(lane < HEAD_DIM, k_sw, k_cat).astype(BF16))
        v_dup = (jnp.where(lane < HEAD_DIM, v_cat, v_sw).astype(BF16),
                 jnp.where(lane < HEAD_DIM, v_sw, v_cat).astype(BF16))
        for c in range(Q_WIDTH // LANES):
            kvh = (2 * c) // (N_Q_HEADS // N_KV_HEADS)
            qcol = u_scr[j * WINDOW:(j + 1) * WINDOW, c * LANES:(c + 1) * LANES] * scale
            outs = []
            for half in range(2):
                head = 2 * c + half
                keep = (qlane < HEAD_DIM) if half == 0 else (qlane >= HEAD_DIM)
                q = jnp.where(keep, qcol, 0.0).astype(BF16)
                sc = lax.dot_general(q, k_dup[kvh], (((1,), (1,)), ((), ())),
                                     preferred_element_type=F32)
                sc = sc + bias_ref[first, head]
                sink = sinks_ref[head]
                m = jnp.maximum(jnp.max(sc, axis=-1, keepdims=True), sink)
                e = jnp.exp(sc - m)
                den = jnp.sum(e, axis=-1, keepdims=True) + jnp.exp(sink - m)
                p = (e / den).astype(BF16)
                outs.append(jnp.dot(p, v_dup[kvh], preferred_element_type=F32))
            attn = jnp.where(qlane < HEAD_DIM, outs[0], outs[1])
            cat_scr[j * WINDOW:(j + 1) * WINDOW, c * LANES:(c + 1) * LANES] = attn.astype(BF16)

    first = CONV_HALO - (CONV_WIDTH - 1)
    span = CONV_HALO + ROW_CHUNK
    for i in range(rows // ROW_CHUNK):
        window = xpad_scr[i * ROW_CHUNK:i * ROW_CHUNK + span, :]
        acc = jnp.zeros((ROW_CHUNK, CONV_CHANNELS), F32)
        for r in range(SUBLANES):
            taps = [k for k in range(CONV_WIDTH) if (first + k) % SUBLANES == r]
            if not taps:
                continue
            shifted = pltpu.roll(window, span - r, axis=0) if r else window
            for k in taps:
                a = (first + k) // SUBLANES * SUBLANES
                acc = acc + shifted[a:a + ROW_CHUNK, :] * cw_ref[k:k + 1, :]
        y = acc + cb_ref[...]
        mu = jnp.mean(y, axis=-1, keepdims=True)
        yc = y - mu
        var = jnp.mean(yc * yc, axis=-1, keepdims=True)
        z = yc * lax.rsqrt(var + LN_EPS) * lng_ref[...] + lnb_ref[...]
        cat_scr[i * ROW_CHUNK:(i + 1) * ROW_CHUNK, Q_WIDTH:] = (z * _sigmoid(z)).astype(BF16)

    o_ref[...] = h_ref[...] + jnp.dot(cat_scr[...], wout_ref[...], preferred_element_type=F32)

    kv_scr[0:WINDOW, :] = kv_scr[rows:rows + WINDOW, :]
    xpad_scr[0:CONV_HALO, :] = xpad_scr[rows:rows + CONV_HALO, :]


def _even_call(h, gain, w_in, sinks, bias_tbl, conv_w, conv_b, ln_g, ln_b, w_out):
    bsz, seq, _ = h.shape
    tile = pl.BlockSpec((None, MIX_ROWS, D_MODEL), lambda b, s: (b, s, 0))
    return pl.pallas_call(
        _even_kernel,
        out_shape=jax.ShapeDtypeStruct(h.shape, F32),
        grid=(bsz, seq // MIX_ROWS),
        in_specs=[
            pl.BlockSpec(memory_space=pltpu.SMEM),
            tile,
            _resident((1, D_MODEL)),
            _resident(w_in.shape),
            _resident(bias_tbl.shape),
            _resident(conv_w.shape),
            _resident(conv_b.shape),
            _resident(ln_g.shape),
            _resident(ln_b.shape),
            _resident(w_out.shape),
        ],
        out_specs=tile,
        scratch_shapes=[
            pltpu.VMEM((MIX_ROWS, D_MODEL), BF16),
            pltpu.VMEM((MIX_ROWS, EVEN_IN), F32),
            pltpu.VMEM((WINDOW + MIX_ROWS, 2 * KV_WIDTH), BF16),
            pltpu.VMEM((CONV_HALO + MIX_ROWS, CONV_CHANNELS), F32),
            pltpu.VMEM((MIX_ROWS, D_MODEL), BF16),
        ],
        compiler_params=pltpu.CompilerParams(
            dimension_semantics=("parallel", "arbitrary"), vmem_limit_bytes=VMEM_LIMIT_BYTES),
        name="even_mixer",
    )(sinks, h, gain, w_in, bias_tbl, conv_w, conv_b, ln_g, ln_b, w_out)


def _odd_kernel(h_ref, g_ref, win_ref, cw_ref, cb_ref, wgate_ref, gab_ref, gxb_ref, lam_ref,
                wout_ref, o_ref, hn_scr, u_scr, xpad_scr, rec_scr, gpre_scr, y_scr, state_scr):
    rows = h_ref.shape[0]
    s = pl.program_id(1)

    @pl.when(s == 0)
    def _():
        xpad_scr[0:LRU_HALO, :] = jnp.zeros((LRU_HALO, LRU_WIDTH), F32)
        state_scr[...] = jnp.zeros((SUBLANES, LRU_WIDTH), F32)

    def norm_body(i, carry):
        r = pl.ds(pl.multiple_of(i * ROW_CHUNK, ROW_CHUNK), ROW_CHUNK)
        hn_scr[r, :] = _rmsnorm_rows(h_ref[r, :], g_ref[...]).astype(BF16)
        return carry

    lax.fori_loop(0, rows // ROW_CHUNK, norm_body, 0)
    u_scr[...] = jnp.dot(hn_scr[...], win_ref[...], preferred_element_type=F32)
    xpad_scr[LRU_HALO:, :] = u_scr[:, LRU_WIDTH:]

    span = LRU_HALO + ROW_CHUNK
    for i in range(rows // ROW_CHUNK):
        window = xpad_scr[i * ROW_CHUNK:i * ROW_CHUNK + span, :]
        acc = window[LRU_HALO:, :] * cw_ref[LRU_CONV_WIDTH - 1:LRU_CONV_WIDTH, :]
        for k in range(LRU_CONV_WIDTH - 1):
            shifted = pltpu.roll(window, LRU_CONV_WIDTH - 1 - k, axis=0)
            acc = acc + shifted[LRU_HALO:, :] * cw_ref[k:k + 1, :]
        rec_scr[i * ROW_CHUNK:(i + 1) * ROW_CHUNK, :] = acc + cb_ref[...]

    for hd in range(LRU_HEADS):
        cs = slice(hd * LRU_BLOCK, (hd + 1) * LRU_BLOCK)
        pre = jnp.dot(rec_scr[:, cs].astype(BF16), wgate_ref[hd], preferred_element_type=F32)
        gpre_scr[:, cs] = pre[:, :LRU_BLOCK]
        gpre_scr[:, LRU_WIDTH + hd * LRU_BLOCK:LRU_WIDTH + (hd + 1) * LRU_BLOCK] = pre[:, LRU_BLOCK:]

    lam = lam_ref[...]
    log_sig_lam = -(jnp.maximum(-lam, 0.0) + jnp.log1p(jnp.exp(-jnp.abs(lam))))
    groups = SCAN_ROWS // SUBLANES
    row8 = lax.broadcasted_iota(jnp.int32, (groups, SUBLANES, LRU_WIDTH), 1)

    def scan_body(i, hprev):
        r = pl.ds(pl.multiple_of(i * SCAN_ROWS, SCAN_ROWS), SCAN_ROWS)
        x = rec_scr[r, :]
        rg = _sigmoid(gpre_scr[r, 0:LRU_WIDTH] + gab_ref[...])
        ig = _sigmoid(gpre_scr[r, LRU_WIDTH:] + gxb_ref[...])
        log_a = RG_LRU_C * rg * log_sig_lam
        a = jnp.exp(log_a)
        bx = jnp.sqrt(-jnp.tanh(log_a) * (a * a + 1.0)) * (ig * x)
        a3 = a.reshape(groups, SUBLANES, LRU_WIDTH)
        b3 = bx.reshape(groups, SUBLANES, LRU_WIDTH)
        for d in (1, 2, 4):
            a_sh = jnp.where(row8 >= d, pltpu.roll(a3, d, axis=1), 1.0)
            b_sh = jnp.where(row8 >= d, pltpu.roll(b3, d, axis=1), 0.0)
            b3 = a3 * b_sh + b3
            a3 = a3 * a_sh
        hs = []
        for gi in range(groups):
            hg = a3[gi] * hprev + b3[gi]
            hprev = jnp.broadcast_to(hg[SUBLANES - 1:SUBLANES, :], (SUBLANES, LRU_WIDTH))
            hs.append(hg)
        hs = jnp.concatenate(hs, axis=0)
        y_scr[r, :] = (jax.nn.gelu(u_scr[r, 0:LRU_WIDTH]) * hs).astype(BF16)
        return hprev

    state_scr[...] = lax.fori_loop(0, rows // SCAN_ROWS, scan_body, state_scr[...])

    o_ref[...] = h_ref[...] + jnp.dot(y_scr[...], wout_ref[...], preferred_element_type=F32)
    xpad_scr[0:LRU_HALO, :] = xpad_scr[rows:rows + LRU_HALO, :]


def _odd_call(h, gain, w_in, conv_w, conv_b, w_gate, ga_b, gx_b, lam, w_out):
    bsz, seq, _ = h.shape
    tile = pl.BlockSpec((None, MIX_ROWS, D_MODEL), lambda b, s: (b, s, 0))
    return pl.pallas_call(
        _odd_kernel,
        out_shape=jax.ShapeDtypeStruct(h.shape, F32),
        grid=(bsz, seq // MIX_ROWS),
        in_specs=[
            tile,
            _resident((1, D_MODEL)),
            _resident(w_in.shape),
            _resident(conv_w.shape),
            _resident(conv_b.shape),
            _resident(w_gate.shape),
            _resident(ga_b.shape),
            _resident(gx_b.shape),
            _resident(lam.shape),
            _resident(w_out.shape),
        ],
        out_specs=tile,
        scratch_shapes=[
            pltpu.VMEM((MIX_ROWS, D_MODEL), BF16),
            pltpu.VMEM((MIX_ROWS, 2 * LRU_WIDTH), F32),
            pltpu.VMEM((LRU_HALO + MIX_ROWS, LRU_WIDTH), F32),
            pltpu.VMEM((MIX_ROWS, LRU_WIDTH), F32),
            pltpu.VMEM((MIX_ROWS, 2 * LRU_WIDTH), F32),
            pltpu.VMEM((MIX_ROWS, LRU_WIDTH), BF16),
            pltpu.VMEM((SUBLANES, LRU_WIDTH), F32),
        ],
        compiler_params=pltpu.CompilerParams(
            dimension_semantics=("parallel", "arbitrary"), vmem_limit_bytes=VMEM_LIMIT_BYTES),
        name="odd_mixer",
    )(h, gain, w_in, conv_w, conv_b, w_gate, ga_b, gx_b, lam, w_out)


def _final_norm_kernel(x_ref, g_ref, o_ref):
    o_ref[...] = _rmsnorm_rows(x_ref[...], g_ref[...])


def _final_norm_call(h2d, gain):
    n = h2d.shape[0]
    return pl.pallas_call(
        _final_norm_kernel,
        out_shape=jax.ShapeDtypeStruct(h2d.shape, F32),
        grid=(n // FFN_ROWS,),
        in_specs=[pl.BlockSpec((FFN_ROWS, D_MODEL), lambda i: (i, 0)), _resident((1, D_MODEL))],
        out_specs=pl.BlockSpec((FFN_ROWS, D_MODEL), lambda i: (i, 0)),
        compiler_params=pltpu.CompilerParams(dimension_semantics=("parallel",)),
        name="final_norm",
    )(h2d, gain)


def _t5_bucket_table():
    qi = np.arange(WINDOW)[:, None]
    sj = np.arange(2 * WINDOW)[None, :]
    dist = qi + WINDOW - sj
    n = np.maximum(dist, 0)
    max_exact = NUM_BUCKETS // 2
    nf = np.maximum(n, max_exact).astype(np.float32)
    large = max_exact + (np.log(nf / max_exact) / math.log(MAX_DISTANCE / max_exact)
                         * (NUM_BUCKETS - max_exact)).astype(np.int32)
    large = np.minimum(large, NUM_BUCKETS - 1)
    bucket = np.where(n < max_exact, n, large)
    in_window = (dist >= 0) & (dist < WINDOW)
    return bucket, in_window, sj


def _attention_bias_tables(rel_bias):
    bucket, in_window, sj = _t5_bucket_table()
    bias = jnp.transpose(rel_bias.astype(F32)[bucket], (2, 0, 1))
    normal = jnp.where(in_window[None], bias, NEG_INF)
    first = jnp.where((in_window & (sj >= WINDOW))[None], bias, NEG_INF)
    return jnp.stack([normal, first])


def _pack_ffn(wg, wu, wd):
    wg = wg.astype(BF16).reshape(D_MODEL, N_FF_CHUNKS, FF_CHUNK)
    wu = wu.astype(BF16).reshape(D_MODEL, N_FF_CHUNKS, FF_CHUNK)
    wgu = jnp.transpose(jnp.concatenate([wg, wu], axis=-1), (1, 0, 2))
    return wgu, wd.astype(BF16)


def kernel(x, norm_ffn1, ffn1_wg, ffn1_wu, ffn1_wd, norm_mix, norm_ffn2, ffn2_wg, ffn2_wu, ffn2_wd,
           rel_bias, even_w_in, attn_sinks, conv_b_w, conv_b_b, conv_ln_g, conv_ln_b, even_w_out,
           odd_w_in, lru_conv_w, lru_conv_b, gate_a_w, gate_a_b, gate_x_w, gate_x_b, lru_lambda,
           odd_w_out, norm_final):
    bsz, seq, d = x.shape
    depth = norm_ffn1.shape[0]
    bias_tbl = _attention_bias_tables(rel_bias)
    row = lambda v: v.reshape(1, -1)

    def ffn(h, gain, wg, wu, wd):
        wgu, wdb = _pack_ffn(wg, wu, wd)
        return _ffn_call(h.reshape(bsz * seq, d), row(gain), wgu, wdb).reshape(bsz, seq, d)

    h = x
    for layer in range(depth):
        h = ffn(h, norm_ffn1[layer], ffn1_wg[layer], ffn1_wu[layer], ffn1_wd[layer])
        if layer % 2 == 0:
            e = layer // 2
            h = _even_call(h, row(norm_mix[layer]), even_w_in[e].astype(BF16), attn_sinks[e],
                           bias_tbl, conv_b_w[e], row(conv_b_b[e]), row(conv_ln_g[e]),
                           row(conv_ln_b[e]), even_w_out[e].astype(BF16))
        else:
            o = layer // 2
            w_gate = jnp.concatenate([gate_a_w[o], gate_x_w[o]], axis=-1).astype(BF16)
            h = _odd_call(h, row(norm_mix[layer]), odd_w_in[o].astype(BF16), lru_conv_w[o],
                          row(lru_conv_b[o]), w_gate, row(gate_a_b[o]), row(gate_x_b[o]),
                          row(lru_lambda[o]), odd_w_out[o].astype(BF16))
        h = ffn(h, norm_ffn2[layer], ffn2_wg[layer], ffn2_wu[layer], ffn2_wd[layer])
    return _final_norm_call(h.reshape(bsz * seq, d), row(norm_final)).reshape(bsz, seq, d)
```

```python
import functools
import math

import jax
import jax.numpy as jnp
import numpy as np
from jax import lax
from jax.experimental import pallas as pl
from jax.experimental.pallas import tpu as pltpu

D_MODEL = 1024
D_FF = 2816
HEAD_DIM = 64
N_Q_HEADS = 8
N_KV_HEADS = 2
WINDOW = 128
Q_WIDTH = N_Q_HEADS * HEAD_DIM
KV_WIDTH = N_KV_HEADS * HEAD_DIM
NUM_BUCKETS = 32
MAX_DISTANCE = 128
CONV_CHANNELS = 512
CONV_WIDTH = 31
EVEN_IN = Q_WIDTH + 2 * KV_WIDTH + 2 * CONV_CHANNELS
LRU_WIDTH = 1024
LRU_HEADS = 8
LRU_BLOCK = 128
LRU_CONV_WIDTH = 4
RG_LRU_C = 8.0
RMS_EPS = 1e-6
LN_EPS = 1e-5
NEG_INF = -1e30

LANES = 128
SUBLANES = 8
MXU_DIM = 256
VMEM_LIMIT_BYTES = 56 * 1024 * 1024

FF_CHUNK = MXU_DIM
N_FF_CHUNKS = D_FF // FF_CHUNK
FFN_ROWS = 512
MIX_ROWS = 512
ROW_CHUNK = 64
CONV_HALO = 32
LRU_HALO = 8
SCAN_ROWS = 16

BF16 = jnp.bfloat16
F32 = jnp.float32


def _resident(shape):
    nd = len(shape)
    return pl.BlockSpec(shape, lambda *_: (0,) * nd, pipeline_mode=pl.Buffered(1))


def _sigmoid(x):
    return 0.5 * jnp.tanh(0.5 * x) + 0.5


def _rmsnorm_rows(x, g):
    ms = jnp.mean(x * x, axis=-1, keepdims=True)
    return x * lax.rsqrt(ms + RMS_EPS) * g


def _ffn_kernel(x_ref, xnext_ref, g_ref, wg_ref, wu_ref, wd_ref, o_ref, xn_scr, a_scr):
    rows = x_ref.shape[0]
    i = pl.program_id(0)
    slot = lax.rem(i, 2)

    def norm_chunk(src_ref, dst_slot, k):
        r = slice(k * ROW_CHUNK, (k + 1) * ROW_CHUNK)
        xn_scr[dst_slot, r, :] = _rmsnorm_rows(src_ref[r, :], g_ref[...]).astype(BF16)

    @pl.when(i == 0)
    def _():
        for k in range(rows // ROW_CHUNK):
            norm_chunk(x_ref, 0, k)

    for c in range(N_FF_CHUNKS):
        cols = slice(c * FF_CHUNK, (c + 1) * FF_CHUNK)
        g = jnp.dot(xn_scr[slot], wg_ref[:, cols], preferred_element_type=F32)
        u = jnp.dot(xn_scr[slot], wu_ref[:, cols], preferred_element_type=F32)
        a_scr[:, cols] = (g * _sigmoid(g) * u).astype(BF16)
        if c < rows // ROW_CHUNK:
            norm_chunk(xnext_ref, 1 - slot, c)
    y = jnp.dot(a_scr[...], wd_ref[...], preferred_element_type=F32)
    o_ref[...] = x_ref[...] + 0.5 * y


def _ffn_call(h2d, gain, wg, wu, wd):
    n = h2d.shape[0]
    n_tiles = n // FFN_ROWS
    assert FFN_ROWS // ROW_CHUNK <= N_FF_CHUNKS
    return pl.pallas_call(
        _ffn_kernel,
        out_shape=jax.ShapeDtypeStruct(h2d.shape, F32),
        grid=(n_tiles,),
        in_specs=[
            pl.BlockSpec((FFN_ROWS, D_MODEL), lambda i: (i, 0)),
            pl.BlockSpec((FFN_ROWS, D_MODEL), lambda i: (jnp.minimum(i + 1, n_tiles - 1), 0)),
            _resident((1, D_MODEL)),
            _resident(wg.shape),
            _resident(wu.shape),
            _resident(wd.shape),
        ],
        out_specs=pl.BlockSpec((FFN_ROWS, D_MODEL), lambda i: (i, 0)),
        scratch_shapes=[
            pltpu.VMEM((2, FFN_ROWS, D_MODEL), BF16),
            pltpu.VMEM((FFN_ROWS, D_FF), BF16),
        ],
        compiler_params=pltpu.CompilerParams(
            dimension_semantics=("arbitrary",), vmem_limit_bytes=VMEM_LIMIT_BYTES),
        name="ffn",
    )(h2d, h2d, gain, wg, wu, wd)


def _even_kernel(sinks_ref, h_ref, g_ref, win_ref, bias_ref, cw_ref, cb_ref, lng_ref, lnb_ref,
                 wout_ref, o_ref, hn_scr, u_scr, kv_scr, xpad_scr, cat_scr):
    rows = h_ref.shape[0]
    s = pl.program_id(1)

    @pl.when(s == 0)
    def _():
        kv_scr[0:WINDOW, :] = jnp.zeros((WINDOW, 2 * KV_WIDTH), BF16)
        xpad_scr[0:CONV_HALO, :] = jnp.zeros((CONV_HALO, CONV_CHANNELS), F32)

    for i in range(rows // ROW_CHUNK):
        r = slice(i * ROW_CHUNK, (i + 1) * ROW_CHUNK)
        hn_scr[r, :] = _rmsnorm_rows(h_ref[r, :], g_ref[...]).astype(BF16)
    u_scr[...] = jnp.dot(hn_scr[...], win_ref[...], preferred_element_type=F32)

    o_k = Q_WIDTH
    o_g = Q_WIDTH + 2 * KV_WIDTH
    kv_scr[WINDOW:, :] = u_scr[:, o_k:o_g].astype(BF16)

    for i in range(rows // ROW_CHUNK):
        r = slice(i * ROW_CHUNK, (i + 1) * ROW_CHUNK)
        a = u_scr[r, o_g:o_g + CONV_CHANNELS]
        b = u_scr[r, o_g + CONV_CHANNELS:]
        xpad_scr[CONV_HALO + i * ROW_CHUNK:CONV_HALO + (i + 1) * ROW_CHUNK, :] = a * _sigmoid(b)

    lane = lax.broadcasted_iota(jnp.int32, (2 * WINDOW, LANES), 1)
    qlane = lax.broadcasted_iota(jnp.int32, (WINDOW, LANES), 1)
    scale = 1.0 / math.sqrt(HEAD_DIM)
    for j in range(rows // WINDOW):
        first = (s == 0).astype(jnp.int32) if j == 0 else 0
        kvb = kv_scr[j * WINDOW:(j + 2) * WINDOW, :]
        k_cat = kvb[:, :KV_WIDTH].astype(F32)
        v_cat = kvb[:, KV_WIDTH:].astype(F32)
        k_sw = pltpu.roll(k_cat, HEAD_DIM, axis=1)
        v_sw = pltpu.roll(v_cat, HEAD_DIM, axis=1)
        k_dup = (jnp.where(lane < HEAD_DIM, k_cat, k_sw).astype(BF16),
                 jnp.where(lane < HEAD_DIM, k_sw, k_cat).astype(BF16))
        v_dup = (jnp.where(lane < HEAD_DIM, v_cat, v_sw).astype(BF16),
                 jnp.where(lane < HEAD_DIM, v_sw, v_cat).astype(BF16))
        for c in range(Q_WIDTH // LANES):
            kvh = (2 * c) // (N_Q_HEADS // N_KV_HEADS)
            qcol = u_scr[j * WINDOW:(j + 1) * WINDOW, c * LANES:(c + 1) * LANES] * scale
            outs = []
            for half in range(2):
                head = 2 * c + half
                keep = (qlane < HEAD_DIM) if half == 0 else (qlane >= HEAD_DIM)
                q = jnp.where(keep, qcol, 0.0).astype(BF16)
                sc = lax.dot_general(q, k_dup[kvh], (((1,), (1,)), ((), ())),
                                     preferred_element_type=F32)
                sc = sc + bias_ref[first, head]
                sink = sinks_ref[head]
                m = jnp.maximum(jnp.max(sc, axis=-1, keepdims=True), sink)
                e = jnp.exp(sc - m)
                den = jnp.sum(e, axis=-1, keepdims=True) + jnp.exp(sink - m)
                p = (e / den).astype(BF16)
                outs.append(jnp.dot(p, v_dup[kvh], preferred_element_type=F32))
            attn = jnp.where(qlane < HEAD_DIM, outs[0], outs[1])
            cat_scr[j * WINDOW:(j + 1) * WINDOW, c * LANES:(c + 1) * LANES] = attn.astype(BF16)

    first = CONV_HALO - (CONV_WIDTH - 1)
    span = CONV_HALO + ROW_CHUNK
    for i in range(rows // ROW_CHUNK):
        window = xpad_scr[i * ROW_CHUNK:i * ROW_CHUNK + span, :]
        acc = jnp.zeros((ROW_CHUNK, CONV_CHANNELS), F32)
        for r in range(SUBLANES):
            taps = [k for k in range(CONV_WIDTH) if (first + k) % SUBLANES == r]
            if not taps:
                continue
            shifted = pltpu.roll(window, span - r, axis=0) if r else window
            for k in taps:
                a = (first + k) // SUBLANES * SUBLANES
                acc = acc + shifted[a:a + ROW_CHUNK, :] * cw_ref[k:k + 1, :]
        y = acc + cb_ref[...]
        mu = jnp.mean(y, axis=-1, keepdims=True)
        yc = y - mu
        var = jnp.mean(yc * yc, axis=-1, keepdims=True)
        z = yc * lax.rsqrt(var + LN_EPS) * lng_ref[...] + lnb_ref[...]
        cat_scr[i * ROW_CHUNK:(i + 1) * ROW_CHUNK, Q_WIDTH:] = (z * _sigmoid(z)).astype(BF16)

    o_ref[...] = h_ref[...] + jnp.dot(cat_scr[...], wout_ref[...], preferred_element_type=F32)

    kv_scr[0:WINDOW, :] = kv_scr[rows:rows + WINDOW, :]
    xpad_scr[0:CONV_HALO, :] = xpad_scr[rows:rows + CONV_HALO, :]


def _even_call(h, gain, w_in, sinks, bias_tbl, conv_w, conv_b, ln_g, ln_b, w_out):
    bsz, seq, _ = h.shape
    tile = pl.BlockSpec((None, MIX_ROWS, D_MODEL), lambda b, s: (b, s, 0))
    return pl.pallas_call(
        _even_kernel,
        out_shape=jax.ShapeDtypeStruct(h.shape, F32),
        grid=(bsz, seq // MIX_ROWS),
        in_specs=[
            pl.BlockSpec(memory_space=pltpu.SMEM),
            tile,
            _resident((1, D_MODEL)),
            _resident(w_in.shape),
            _resident(bias_tbl.shape),
            _resident(conv_w.shape),
            _resident(conv_b.shape),
            _resident(ln_g.shape),
            _resident(ln_b.shape),
            _resident(w_out.shape),
        ],
        out_specs=tile,
        scratch_shapes=[
            pltpu.VMEM((MIX_ROWS, D_MODEL), BF16),
            pltpu.VMEM((MIX_ROWS, EVEN_IN), F32),
            pltpu.VMEM((WINDOW + MIX_ROWS, 2 * KV_WIDTH), BF16),
            pltpu.VMEM((CONV_HALO + MIX_ROWS, CONV_CHANNELS), F32),
            pltpu.VMEM((MIX_ROWS, D_MODEL), BF16),
        ],
        compiler_params=pltpu.CompilerParams(
            dimension_semantics=("parallel", "arbitrary"), vmem_limit_bytes=VMEM_LIMIT_BYTES),
        name="even_mixer",
    )(sinks, h, gain, w_in, bias_tbl, conv_w, conv_b, ln_g, ln_b, w_out)


def _odd_kernel(h_ref, g_ref, win_ref, cw_ref, cb_ref, wgate_ref, gab_ref, gxb_ref, lam_ref,
                wout_ref, o_ref, hn_scr, u_scr, xpad_scr, rec_scr, gpre_scr, y_scr, state_scr):
    rows = h_ref.shape[0]
    s = pl.program_id(1)

    @pl.when(s == 0)
    def _():
        xpad_scr[0:LRU_HALO, :] = jnp.zeros((LRU_HALO, LRU_WIDTH), F32)
        state_scr[...] = jnp.zeros((SUBLANES, LRU_WIDTH), F32)

    for i in range(rows // ROW_CHUNK):
        r = slice(i * ROW_CHUNK, (i + 1) * ROW_CHUNK)
        hn_scr[r, :] = _rmsnorm_rows(h_ref[r, :], g_ref[...]).astype(BF16)
    u_scr[...] = jnp.dot(hn_scr[...], win_ref[...], preferred_element_type=F32)
    xpad_scr[LRU_HALO:, :] = u_scr[:, LRU_WIDTH:]

    span = LRU_HALO + ROW_CHUNK
    for i in range(rows // ROW_CHUNK):
        window = xpad_scr[i * ROW_CHUNK:i * ROW_CHUNK + span, :]
        acc = window[LRU_HALO:, :] * cw_ref[LRU_CONV_WIDTH - 1:LRU_CONV_WIDTH, :]
        for k in range(LRU_CONV_WIDTH - 1):
            shifted = pltpu.roll(window, LRU_CONV_WIDTH - 1 - k, axis=0)
            acc = acc + shifted[LRU_HALO:, :] * cw_ref[k:k + 1, :]
        rec_scr[i * ROW_CHUNK:(i + 1) * ROW_CHUNK, :] = acc + cb_ref[...]

    for hd in range(LRU_HEADS):
        cs = slice(hd * LRU_BLOCK, (hd + 1) * LRU_BLOCK)
        pre = jnp.dot(rec_scr[:, cs].astype(BF16), wgate_ref[hd], preferred_element_type=F32)
        gpre_scr[:, cs] = pre[:, :LRU_BLOCK]
        gpre_scr[:, LRU_WIDTH + hd * LRU_BLOCK:LRU_WIDTH + (hd + 1) * LRU_BLOCK] = pre[:, LRU_BLOCK:]

    lam = lam_ref[...]
    log_sig_lam = -(jnp.maximum(-lam, 0.0) + jnp.log1p(jnp.exp(-jnp.abs(lam))))
    groups = SCAN_ROWS // SUBLANES
    row8 = lax.broadcasted_iota(jnp.int32, (groups, SUBLANES, LRU_WIDTH), 1)

    def scan_body(i, hprev):
        r = pl.ds(pl.multiple_of(i * SCAN_ROWS, SCAN_ROWS), SCAN_ROWS)
        x = rec_scr[r, :]
        rg = _sigmoid(gpre_scr[r, 0:LRU_WIDTH] + gab_ref[...])
        ig = _sigmoid(gpre_scr[r, LRU_WIDTH:] + gxb_ref[...])
        log_a = RG_LRU_C * rg * log_sig_lam
        a = jnp.exp(log_a)
        bx = jnp.sqrt(-jnp.tanh(log_a) * (a * a + 1.0)) * (ig * x)
        a3 = a.reshape(groups, SUBLANES, LRU_WIDTH)
        b3 = bx.reshape(groups, SUBLANES, LRU_WIDTH)
        for d in (1, 2, 4):
            a_sh = jnp.where(row8 >= d, pltpu.roll(a3, d, axis=1), 1.0)
            b_sh = jnp.where(row8 >= d, pltpu.roll(b3, d, axis=1), 0.0)
            b3 = a3 * b_sh + b3
            a3 = a3 * a_sh
        hs = []
        for gi in range(groups):
            hg = a3[gi] * hprev + b3[gi]
            hprev = jnp.broadcast_to(hg[SUBLANES - 1:SUBLANES, :], (SUBLANES, LRU_WIDTH))
            hs.append(hg)
        hs = jnp.concatenate(hs, axis=0)
        y_scr[r, :] = (jax.nn.gelu(u_scr[r, 0:LRU_WIDTH]) * hs).astype(BF16)
        return hprev

    state_scr[...] = lax.fori_loop(0, rows // SCAN_ROWS, scan_body, state_scr[...])

    o_ref[...] = h_ref[...] + jnp.dot(y_scr[...], wout_ref[...], preferred_element_type=F32)
    xpad_scr[0:LRU_HALO, :] = xpad_scr[rows:rows + LRU_HALO, :]


def _odd_call(h, gain, w_in, conv_w, conv_b, w_gate, ga_b, gx_b, lam, w_out):
    bsz, seq, _ = h.shape
    tile = pl.BlockSpec((None, MIX_ROWS, D_MODEL), lambda b, s: (b, s, 0))
    return pl.pallas_call(
        _odd_kernel,
        out_shape=jax.ShapeDtypeStruct(h.shape, F32),
        grid=(bsz, seq // MIX_ROWS),
        in_specs=[
            tile,
            _resident((1, D_MODEL)),
            _resident(w_in.shape),
            _resident(conv_w.shape),
            _resident(conv_b.shape),
            _resident(w_gate.shape),
            _resident(ga_b.shape),
            _resident(gx_b.shape),
            _resident(lam.shape),
            _resident(w_out.shape),
        ],
        out_specs=tile,
        scratch_shapes=[
            pltpu.VMEM((MIX_ROWS, D_MODEL), BF16),
            pltpu.VMEM((MIX_ROWS, 2 * LRU_WIDTH), F32),
            pltpu.VMEM((LRU_HALO + MIX_ROWS, LRU_WIDTH), F32),
            pltpu.VMEM((MIX_ROWS, LRU_WIDTH), F32),
            pltpu.VMEM((MIX_ROWS, 2 * LRU_WIDTH), F32),
            pltpu.VMEM((MIX_ROWS, LRU_WIDTH), BF16),
            pltpu.VMEM((SUBLANES, LRU_WIDTH), F32),
        ],
        compiler_params=pltpu.CompilerParams(
            dimension_semantics=("parallel", "arbitrary"), vmem_limit_bytes=VMEM_LIMIT_BYTES),
        name="odd_mixer",
    )(h, gain, w_in, conv_w, conv_b, w_gate, ga_b, gx_b, lam, w_out)


def _final_norm_kernel(x_ref, g_ref, o_ref):
    o_ref[...] = _rmsnorm_rows(x_ref[...], g_ref[...])


def _final_norm_call(h2d, gain):
    n = h2d.shape[0]
    return pl.pallas_call(
        _final_norm_kernel,
        out_shape=jax.ShapeDtypeStruct(h2d.shape, F32),
        grid=(n // FFN_ROWS,),
        in_specs=[pl.BlockSpec((FFN_ROWS, D_MODEL), lambda i: (i, 0)), _resident((1, D_MODEL))],
        out_specs=pl.BlockSpec((FFN_ROWS, D_MODEL), lambda i: (i, 0)),
        compiler_params=pltpu.CompilerParams(dimension_semantics=("parallel",)),
        name="final_norm",
    )(h2d, gain)


def _t5_bucket_table():
    qi = np.arange(WINDOW)[:, None]
    sj = np.arange(2 * WINDOW)[None, :]
    dist = qi + WINDOW - sj
    n = np.maximum(dist, 0)
    max_exact = NUM_BUCKETS // 2
    nf = np.maximum(n, max_exact).astype(np.float32)
    large = max_exact + (np.log(nf / max_exact) / math.log(MAX_DISTANCE / max_exact)
                         * (NUM_BUCKETS - max_exact)).astype(np.int32)
    large = np.minimum(large, NUM_BUCKETS - 1)
    bucket = np.where(n < max_exact, n, large)
    in_window = (dist >= 0) & (dist < WINDOW)
    return bucket, in_window, sj


def _attention_bias_tables(rel_bias):
    bucket, in_window, sj = _t5_bucket_table()
    onehot = (jnp.asarray(bucket, jnp.int32)[..., None] == jnp.arange(NUM_BUCKETS)).astype(F32)
    bias = jnp.einsum("qsb,bh->hqs", onehot, rel_bias.astype(F32),
                      precision=lax.Precision.HIGHEST)
    normal = jnp.where(in_window[None], bias, NEG_INF)
    first = jnp.where((in_window & (sj >= WINDOW))[None], bias, NEG_INF)
    return jnp.stack([normal, first])


def kernel(x, norm_ffn1, ffn1_wg, ffn1_wu, ffn1_wd, norm_mix, norm_ffn2, ffn2_wg, ffn2_wu, ffn2_wd,
           rel_bias, even_w_in, attn_sinks, conv_b_w, conv_b_b, conv_ln_g, conv_ln_b, even_w_out,
           odd_w_in, lru_conv_w, lru_conv_b, gate_a_w, gate_a_b, gate_x_w, gate_x_b, lru_lambda,
           odd_w_out, norm_final):
    bsz, seq, d = x.shape
    depth = norm_ffn1.shape[0]
    bias_tbl = _attention_bias_tables(rel_bias)
    row = lambda v: v.reshape(1, -1)

    def ffn(h, gain, wg, wu, wd):
        return _ffn_call(h.reshape(bsz * seq, d), row(gain), wg.astype(BF16), wu.astype(BF16),
                         wd.astype(BF16)).reshape(bsz, seq, d)

    h = x
    for layer in range(depth):
        h = ffn(h, norm_ffn1[layer], ffn1_wg[layer], ffn1_wu[layer], ffn1_wd[layer])
        if layer % 2 == 0:
            e = layer // 2
            h = _even_call(h, row(norm_mix[layer]), even_w_in[e].astype(BF16), attn_sinks[e],
                           bias_tbl, conv_b_w[e], row(conv_b_b[e]), row(conv_ln_g[e]),
                           row(conv_ln_b[e]), even_w_out[e].astype(BF16))
        else:
            o = layer // 2
            w_gate = jnp.concatenate([gate_a_w[o], gate_x_w[o]], axis=-1).astype(BF16)
            h = _odd_call(h, row(norm_mix[layer]), odd_w_in[o].astype(BF16), lru_conv_w[o],
                          row(lru_conv_b[o]), w_gate, row(gate_a_b[o]), row(gate_x_b[o]),
                          row(lru_lambda[o]), odd_w_out[o].astype(BF16))
        h = ffn(h, norm_ffn2[layer], ffn2_wg[layer], ffn2_wu[layer], ffn2_wd[layer])
    return _final_norm_call(h.reshape(bsz * seq, d), row(norm_final)).reshape(bsz, seq, d)
```

```python
import math

import jax
import jax.numpy as jnp
import numpy as np
from jax import lax
from jax.experimental import pallas as pl
from jax.experimental.pallas import tpu as pltpu

D_MODEL = 1024
D_FF = 2816
HEAD_DIM = 64
N_Q_HEADS = 8
N_KV_HEADS = 2
WINDOW = 128
Q_WIDTH = N_Q_HEADS * HEAD_DIM
KV_WIDTH = N_KV_HEADS * HEAD_DIM
NUM_BUCKETS = 32
MAX_DISTANCE = 128
CONV_CHANNELS = 512
CONV_WIDTH = 31
EVEN_IN = Q_WIDTH + 2 * KV_WIDTH + 2 * CONV_CHANNELS
LRU_WIDTH = 1024
LRU_HEADS = 8
LRU_BLOCK = 128
LRU_CONV_WIDTH = 4
RG_LRU_C = 8.0
RMS_EPS = 1e-6
LN_EPS = 1e-5
NEG_INF = -1e30

LANES = 128
SUBLANES = 8
MXU_DIM = 256
VMEM_LIMIT_BYTES = 56 * 1024 * 1024

FF_CHUNK = MXU_DIM
N_FF_CHUNKS = D_FF // FF_CHUNK
FFN_ROWS = 1024
ROW_CHUNK = 64
MIX_BATCH = SUBLANES
STEPS_PER_CHUNK = ROW_CHUNK // MIX_BATCH
EVEN_STEPS = WINDOW
ODD_STEPS = 64
CONV_HALO = 32
LRU_HALO = LRU_CONV_WIDTH - 1
SCAN_ROWS = 2 * MIX_BATCH

BF16 = jnp.bfloat16
F32 = jnp.float32


def _resident(shape):
    nd = len(shape)
    return pl.BlockSpec(shape, lambda *_: (0,) * nd, pipeline_mode=pl.Buffered(1))


def _sigmoid(x):
    return 0.5 * jnp.tanh(0.5 * x) + 0.5


def _rmsnorm_rows(x, g):
    ms = jnp.mean(x * x, axis=-1, keepdims=True)
    return x * lax.rsqrt(ms + RMS_EPS) * g


def _norm_tile(h_ref, g_ref, hn_scr):
    steps = h_ref.shape[0]
    for i in range(steps // STEPS_PER_CHUNK):
        x = h_ref[i * STEPS_PER_CHUNK:(i + 1) * STEPS_PER_CHUNK].reshape(ROW_CHUNK, D_MODEL)
        hn_scr[i * ROW_CHUNK:(i + 1) * ROW_CHUNK, :] = _rmsnorm_rows(x, g_ref[...]).astype(BF16)


def _ffn_kernel(x_ref, xnext_ref, g_ref, wg_ref, wu_ref, wd_ref, o_ref, xn_scr, a_scr):
    rows = x_ref.shape[0]
    i = pl.program_id(0)
    slot = lax.rem(i, 2)

    def norm_chunk(src_ref, dst_slot, k):
        r = slice(k * ROW_CHUNK, (k + 1) * ROW_CHUNK)
        xn_scr[dst_slot, r, :] = _rmsnorm_rows(src_ref[r, :], g_ref[...]).astype(BF16)

    n_norm = rows // ROW_CHUNK
    norm_per_dot = pl.cdiv(n_norm, N_FF_CHUNKS)

    @pl.when(i == 0)
    def _():
        for k in range(n_norm):
            norm_chunk(x_ref, 0, k)

    for c in range(N_FF_CHUNKS):
        cols = slice(c * FF_CHUNK, (c + 1) * FF_CHUNK)
        g = jnp.dot(xn_scr[slot], wg_ref[:, cols], preferred_element_type=F32)
        u = jnp.dot(xn_scr[slot], wu_ref[:, cols], preferred_element_type=F32)
        a_scr[:, cols] = (g * _sigmoid(g) * u).astype(BF16)
        for k in range(c * norm_per_dot, min((c + 1) * norm_per_dot, n_norm)):
            norm_chunk(xnext_ref, 1 - slot, k)
    y = jnp.dot(a_scr[...], wd_ref[...], preferred_element_type=F32)
    o_ref[...] = x_ref[...] + 0.5 * y


def _ffn_call(x, gain, wg, wu, wd, batch_major_in=False):
    if batch_major_in:
        bsz, seq, _ = x.shape
        n_st = seq // FFN_ROWS
        n_tiles = bsz * n_st
        x_spec = pl.BlockSpec((None, FFN_ROWS, D_MODEL), lambda i: (i // n_st, i % n_st, 0))

        def next_map(i):
            j = jnp.minimum(i + 1, n_tiles - 1)
            return (j // n_st, j % n_st, 0)

        xnext_spec = pl.BlockSpec((None, FFN_ROWS, D_MODEL), next_map)
        out_shape = jax.ShapeDtypeStruct((seq, bsz * D_MODEL), F32)
        out_spec = pl.BlockSpec((FFN_ROWS, D_MODEL), lambda i: (i % n_st, i // n_st))
    else:
        n_tiles = x.shape[0] // FFN_ROWS
        x_spec = pl.BlockSpec((FFN_ROWS, D_MODEL), lambda i: (i, 0))
        xnext_spec = pl.BlockSpec((FFN_ROWS, D_MODEL),
                                  lambda i: (jnp.minimum(i + 1, n_tiles - 1), 0))
        out_shape = jax.ShapeDtypeStruct(x.shape, F32)
        out_spec = pl.BlockSpec((FFN_ROWS, D_MODEL), lambda i: (i, 0))
    out = pl.pallas_call(
        _ffn_kernel,
        out_shape=out_shape,
        grid=(n_tiles,),
        in_specs=[
            x_spec,
            xnext_spec,
            _resident((1, D_MODEL)),
            _resident(wg.shape),
            _resident(wu.shape),
            _resident(wd.shape),
        ],
        out_specs=out_spec,
        scratch_shapes=[
            pltpu.VMEM((2, FFN_ROWS, D_MODEL), BF16),
            pltpu.VMEM((FFN_ROWS, D_FF), BF16),
        ],
        compiler_params=pltpu.CompilerParams(
            dimension_semantics=("arbitrary",), vmem_limit_bytes=VMEM_LIMIT_BYTES),
        name="ffn",
    )(x, x, gain, wg, wu, wd)
    return out.reshape(-1, D_MODEL)


def _even_kernel(sinks_ref, h_ref, g_ref, win_ref, bias_ref, cw_ref, cb_ref, lng_ref, lnb_ref,
                 wout_ref, o_ref, hn_scr, qkv_scr, kvprev_scr, attn_scr, xpad_scr, cat_scr):
    steps = h_ref.shape[0]
    rows = steps * MIX_BATCH
    halo_rows = CONV_HALO * MIX_BATCH
    s = pl.program_id(1)

    @pl.when(s == 0)
    def _():
        kvprev_scr[...] = jnp.zeros(kvprev_scr.shape, F32)
        xpad_scr[0:halo_rows, :] = jnp.zeros((halo_rows, CONV_CHANNELS), F32)

    _norm_tile(h_ref, g_ref, hn_scr)

    for j in range((Q_WIDTH + 2 * KV_WIDTH) // MXU_DIM):
        res = jnp.dot(hn_scr[...], win_ref[:, j * MXU_DIM:(j + 1) * MXU_DIM],
                      preferred_element_type=F32)
        qkv_scr[2 * j] = res[:, :LANES]
        qkv_scr[2 * j + 1] = res[:, LANES:]
    o_g = Q_WIDTH + 2 * KV_WIDTH
    glu_a = jnp.dot(hn_scr[...], win_ref[:, o_g:o_g + CONV_CHANNELS], preferred_element_type=F32)
    glu_b = jnp.dot(hn_scr[...], win_ref[:, o_g + CONV_CHANNELS:], preferred_element_type=F32)
    xpad_scr[halo_rows:, :] = glu_a * _sigmoid(glu_b)

    k_slab = Q_WIDTH // LANES
    v_slab = k_slab + 1
    lane = lax.broadcasted_iota(jnp.int32, (2 * WINDOW, LANES), 1)
    qlane = lax.broadcasted_iota(jnp.int32, (WINDOW, LANES), 1)
    scale = 1.0 / math.sqrt(HEAD_DIM)
    first = (s == 0).astype(jnp.int32)
    for b in range(MIX_BATCH):
        seq_rows = pl.ds(b, steps, stride=MIX_BATCH)
        k_cur = qkv_scr[k_slab, seq_rows, :]
        v_cur = qkv_scr[v_slab, seq_rows, :]
        k_cat = jnp.concatenate([kvprev_scr[b, 0], k_cur], axis=0)
        v_cat = jnp.concatenate([kvprev_scr[b, 1], v_cur], axis=0)
        kvprev_scr[b, 0] = k_cur
        kvprev_scr[b, 1] = v_cur
        k_sw = pltpu.roll(k_cat, HEAD_DIM, axis=1)
        v_sw = pltpu.roll(v_cat, HEAD_DIM, axis=1)
        k_dup = (jnp.where(lane < HEAD_DIM, k_cat, k_sw).astype(BF16),
                 jnp.where(lane < HEAD_DIM, k_sw, k_cat).astype(BF16))
        v_dup = (jnp.where(lane < HEAD_DIM, v_cat, v_sw).astype(BF16),
                 jnp.where(lane < HEAD_DIM, v_sw, v_cat).astype(BF16))
        for c in range(Q_WIDTH // LANES):
            kvh = (2 * c) // (N_Q_HEADS // N_KV_HEADS)
            qcol = qkv_scr[c, seq_rows, :] * scale
            outs = []
            for half in range(2):
                head = 2 * c + half
                keep = (qlane < HEAD_DIM) if half == 0 else (qlane >= HEAD_DIM)
                q = jnp.where(keep, qcol, 0.0).astype(BF16)
                sc = lax.dot_general(q, k_dup[kvh], (((1,), (1,)), ((), ())),
                                     preferred_element_type=F32)
                sc = sc + bias_ref[first, head]
                sink = sinks_ref[head]
                m = jnp.maximum(jnp.max(sc, axis=-1, keepdims=True), sink)
                e = jnp.exp(sc - m)
                den = jnp.sum(e, axis=-1, keepdims=True) + jnp.exp(sink - m)
                p = (e / den).astype(BF16)
                outs.append(jnp.dot(p, v_dup[kvh], preferred_element_type=F32))
            attn_scr[c, seq_rows, :] = jnp.where(qlane < HEAD_DIM, outs[0], outs[1])
    for c in range(Q_WIDTH // LANES):
        cat_scr[:, c * LANES:(c + 1) * LANES] = attn_scr[c].astype(BF16)

    first_step = CONV_HALO - (CONV_WIDTH - 1)

    for i in range(rows // ROW_CHUNK):
        acc = jnp.zeros((ROW_CHUNK, CONV_CHANNELS), F32)
        for k in range(CONV_WIDTH):
            r0 = (i * STEPS_PER_CHUNK + first_step + k) * MIX_BATCH
            acc = acc + xpad_scr[r0:r0 + ROW_CHUNK, :] * cw_ref[k:k + 1, :]
        y = acc + cb_ref[...]
        mu = jnp.mean(y, axis=-1, keepdims=True)
        yc = y - mu
        var = jnp.mean(yc * yc, axis=-1, keepdims=True)
        z = yc * lax.rsqrt(var + LN_EPS) * lng_ref[...] + lnb_ref[...]
        cat_scr[i * ROW_CHUNK:(i + 1) * ROW_CHUNK, Q_WIDTH:] = (z * _sigmoid(z)).astype(BF16)

    mixed = jnp.dot(cat_scr[...], wout_ref[...], preferred_element_type=F32)
    o_ref[...] = h_ref[...] + mixed.reshape(steps, MIX_BATCH, D_MODEL)

    xpad_scr[0:halo_rows, :] = xpad_scr[rows:rows + halo_rows, :]


def _even_call(h, gain, w_in, sinks, bias_tbl, conv_w, conv_b, ln_g, ln_b, w_out):
    seq, bsz, _ = h.shape
    rows = EVEN_STEPS * MIX_BATCH
    tile = pl.BlockSpec((EVEN_STEPS, MIX_BATCH, D_MODEL), lambda bg, s: (s, bg, 0))
    return pl.pallas_call(
        _even_kernel,
        out_shape=jax.ShapeDtypeStruct(h.shape, F32),
        grid=(bsz // MIX_BATCH, seq // EVEN_STEPS),
        in_specs=[
            pl.BlockSpec(memory_space=pltpu.SMEM),
            tile,
            _resident((1, D_MODEL)),
            _resident(w_in.shape),
            _resident(bias_tbl.shape),
            _resident(conv_w.shape),
            _resident(conv_b.shape),
            _resident(ln_g.shape),
            _resident(ln_b.shape),
            _resident(w_out.shape),
        ],
        out_specs=tile,
        scratch_shapes=[
            pltpu.VMEM((rows, D_MODEL), BF16),
            pltpu.VMEM(((Q_WIDTH + 2 * KV_WIDTH) // LANES, rows, LANES), F32),
            pltpu.VMEM((MIX_BATCH, 2, WINDOW, LANES), F32),
            pltpu.VMEM((Q_WIDTH // LANES, rows, LANES), F32),
            pltpu.VMEM(((CONV_HALO + EVEN_STEPS) * MIX_BATCH, CONV_CHANNELS), F32),
            pltpu.VMEM((rows, D_MODEL), BF16),
        ],
        compiler_params=pltpu.CompilerParams(
            dimension_semantics=("parallel", "arbitrary"), vmem_limit_bytes=VMEM_LIMIT_BYTES),
        name="even_mixer",
    )(sinks, h, gain, w_in, bias_tbl, conv_w, conv_b, ln_g, ln_b, w_out)


def _odd_kernel(h_ref, g_ref, win_ref, cw_ref, cb_ref, wgate_ref, gab_ref, gxb_ref, lam_ref,
                wout_ref, o_ref, hn_scr, ug_scr, xpad_scr, rec_scr, gpre_scr, y_scr, state_scr):
    steps = h_ref.shape[0]
    rows = steps * MIX_BATCH
    halo_rows = LRU_HALO * MIX_BATCH
    s = pl.program_id(1)

    @pl.when(s == 0)
    def _():
        xpad_scr[0:halo_rows, :] = jnp.zeros((halo_rows, LRU_WIDTH), F32)
        state_scr[...] = jnp.zeros((MIX_BATCH, LRU_WIDTH), F32)

    _norm_tile(h_ref, g_ref, hn_scr)
    ug_scr[...] = jnp.dot(hn_scr[...], win_ref[:, 0:LRU_WIDTH], preferred_element_type=F32)
    xpad_scr[halo_rows:, :] = jnp.dot(hn_scr[...], win_ref[:, LRU_WIDTH:],
                                      preferred_element_type=F32)

    for i in range(rows // ROW_CHUNK):
        acc = jnp.zeros((ROW_CHUNK, LRU_WIDTH), F32)
        for k in range(LRU_CONV_WIDTH):
            r0 = (i * STEPS_PER_CHUNK + k) * MIX_BATCH
            acc = acc + xpad_scr[r0:r0 + ROW_CHUNK, :] * cw_ref[k:k + 1, :]
        rec_scr[i * ROW_CHUNK:(i + 1) * ROW_CHUNK, :] = acc + cb_ref[...]

    for hd in range(LRU_HEADS):
        cs = slice(hd * LRU_BLOCK, (hd + 1) * LRU_BLOCK)
        pre = jnp.dot(rec_scr[:, cs].astype(BF16), wgate_ref[hd], preferred_element_type=F32)
        gpre_scr[:, cs] = pre[:, :LRU_BLOCK]
        gpre_scr[:, LRU_WIDTH + hd * LRU_BLOCK:LRU_WIDTH + (hd + 1) * LRU_BLOCK] = pre[:, LRU_BLOCK:]

    lam = lam_ref[...]
    log_sig_lam = -(jnp.maximum(-lam, 0.0) + jnp.log1p(jnp.exp(-jnp.abs(lam))))

    def scan_body(i, hprev):
        r = pl.ds(pl.multiple_of(i * SCAN_ROWS, SCAN_ROWS), SCAN_ROWS)
        x = rec_scr[r, :]
        rg = _sigmoid(gpre_scr[r, 0:LRU_WIDTH] + gab_ref[...])
        ig = _sigmoid(gpre_scr[r, LRU_WIDTH:] + gxb_ref[...])
        log_a = RG_LRU_C * rg * log_sig_lam
        a = jnp.exp(log_a)
        bx = jnp.sqrt(-jnp.tanh(log_a) * (a * a + 1.0)) * (ig * x)
        hs = []
        for t in range(SCAN_ROWS // MIX_BATCH):
            rt = slice(t * MIX_BATCH, (t + 1) * MIX_BATCH)
            hprev = a[rt, :] * hprev + bx[rt, :]
            hs.append(hprev)
        hs = jnp.concatenate(hs, axis=0)
        y_scr[r, :] = (jax.nn.gelu(ug_scr[r, :]) * hs).astype(BF16)
        return hprev

    state_scr[...] = lax.fori_loop(0, rows // SCAN_ROWS, scan_body, state_scr[...])

    mixed = jnp.dot(y_scr[...], wout_ref[...], preferred_element_type=F32)
    o_ref[...] = h_ref[...] + mixed.reshape(steps, MIX_BATCH, D_MODEL)
    xpad_scr[0:halo_rows, :] = xpad_scr[rows:rows + halo_rows, :]


def _odd_call(h, gain, w_in, conv_w, conv_b, w_gate, ga_b, gx_b, lam, w_out):
    seq, bsz, _ = h.shape
    rows = ODD_STEPS * MIX_BATCH
    tile = pl.BlockSpec((ODD_STEPS, MIX_BATCH, D_MODEL), lambda bg, s: (s, bg, 0))
    return pl.pallas_call(
        _odd_kernel,
        out_shape=jax.ShapeDtypeStruct(h.shape, F32),
        grid=(bsz // MIX_BATCH, seq // ODD_STEPS),
        in_specs=[
            tile,
            _resident((1, D_MODEL)),
            _resident(w_in.shape),
            _resident(conv_w.shape),
            _resident(conv_b.shape),
            _resident(w_gate.shape),
            _resident(ga_b.shape),
            _resident(gx_b.shape),
            _resident(lam.shape),
            _resident(w_out.shape),
        ],
        out_specs=tile,
        scratch_shapes=[
            pltpu.VMEM((rows, D_MODEL), BF16),
            pltpu.VMEM((rows, LRU_WIDTH), F32),
            pltpu.VMEM(((LRU_HALO + ODD_STEPS) * MIX_BATCH, LRU_WIDTH), F32),
            pltpu.VMEM((rows, LRU_WIDTH), F32),
            pltpu.VMEM((rows, 2 * LRU_WIDTH), F32),
            pltpu.VMEM((rows, LRU_WIDTH), BF16),
            pltpu.VMEM((MIX_BATCH, LRU_WIDTH), F32),
        ],
        compiler_params=pltpu.CompilerParams(
            dimension_semantics=("parallel", "arbitrary"), vmem_limit_bytes=VMEM_LIMIT_BYTES),
        name="odd_mixer",
    )(h, gain, w_in, conv_w, conv_b, w_gate, ga_b, gx_b, lam, w_out)


def _final_norm_kernel(x_ref, g_ref, o_ref):
    o_ref[...] = _rmsnorm_rows(x_ref[...], g_ref[...])


def _final_norm_call(h, gain):
    seq, bsz, _ = h.shape
    return pl.pallas_call(
        _final_norm_kernel,
        out_shape=jax.ShapeDtypeStruct((bsz, seq, D_MODEL), F32),
        grid=(bsz, seq // FFN_ROWS),
        in_specs=[pl.BlockSpec((FFN_ROWS, D_MODEL), lambda b, st: (st, b)),
                  _resident((1, D_MODEL))],
        out_specs=pl.BlockSpec((None, FFN_ROWS, D_MODEL), lambda b, st: (b, st, 0)),
        compiler_params=pltpu.CompilerParams(dimension_semantics=("parallel", "parallel")),
        name="final_norm",
    )(h.reshape(seq, bsz * D_MODEL), gain)


def _t5_bucket_table():
    qi = np.arange(WINDOW)[:, None]
    sj = np.arange(2 * WINDOW)[None, :]
    dist = qi + WINDOW - sj
    n = np.maximum(dist, 0)
    max_exact = NUM_BUCKETS // 2
    nf = np.maximum(n, max_exact).astype(np.float32)
    large = max_exact + (np.log(nf / max_exact) / math.log(MAX_DISTANCE / max_exact)
                         * (NUM_BUCKETS - max_exact)).astype(np.int32)
    large = np.minimum(large, NUM_BUCKETS - 1)
    bucket = np.where(n < max_exact, n, large)
    in_window = (dist >= 0) & (dist < WINDOW)
    return bucket, in_window, sj


def _attention_bias_tables(rel_bias):
    bucket, in_window, sj = _t5_bucket_table()
    onehot = (jnp.asarray(bucket, jnp.int32)[..., None] == jnp.arange(NUM_BUCKETS)).astype(F32)
    bias = jnp.einsum("qsb,bh->hqs", onehot, rel_bias.astype(F32),
                      precision=lax.Precision.HIGHEST)
    normal = jnp.where(in_window[None], bias, NEG_INF)
    first = jnp.where((in_window & (sj >= WINDOW))[None], bias, NEG_INF)
    return jnp.stack([normal, first])


def kernel(x, norm_ffn1, ffn1_wg, ffn1_wu, ffn1_wd, norm_mix, norm_ffn2, ffn2_wg, ffn2_wu, ffn2_wd,
           rel_bias, even_w_in, attn_sinks, conv_b_w, conv_b_b, conv_ln_g, conv_ln_b, even_w_out,
           odd_w_in, lru_conv_w, lru_conv_b, gate_a_w, gate_a_b, gate_x_w, gate_x_b, lru_lambda,
           odd_w_out, norm_final):
    bsz, seq, d = x.shape
    depth = norm_ffn1.shape[0]
    bias_tbl = _attention_bias_tables(rel_bias)
    row = lambda v: v.reshape(1, -1)

    def ffn(h, gain, wg, wu, wd, batch_major_in=False):
        return _ffn_call(h, row(gain), wg.astype(BF16), wu.astype(BF16), wd.astype(BF16),
                         batch_major_in)

    h = x
    for layer in range(depth):
        h = ffn(h, norm_ffn1[layer], ffn1_wg[layer], ffn1_wu[layer], ffn1_wd[layer],
                batch_major_in=(layer == 0))
        h = h.reshape(seq, bsz, d)
        if layer % 2 == 0:
            e = layer // 2
            h = _even_call(h, row(norm_mix[layer]), even_w_in[e].astype(BF16), attn_sinks[e],
                           bias_tbl, conv_b_w[e], row(conv_b_b[e]), row(conv_ln_g[e]),
                           row(conv_ln_b[e]), even_w_out[e].astype(BF16))
        else:
            o = layer // 2
            w_gate = jnp.concatenate([gate_a_w[o], gate_x_w[o]], axis=-1).astype(BF16)
            h = _odd_call(h, row(norm_mix[layer]), odd_w_in[o].astype(BF16), lru_conv_w[o],
                          row(lru_conv_b[o]), w_gate, row(gate_a_b[o]), row(gate_x_b[o]),
                          row(lru_lambda[o]), odd_w_out[o].astype(BF16))
        h = ffn(h.reshape(seq * bsz, d), norm_ffn2[layer], ffn2_wg[layer], ffn2_wu[layer],
                ffn2_wd[layer])
    return _final_norm_call(h.reshape(seq, bsz, d), row(norm_final))
```

```python
import functools
import math

import jax
import jax.numpy as jnp
import numpy as np
from jax import lax
from jax.experimental import pallas as pl
from jax.experimental.pallas import tpu as pltpu

D_MODEL = 1024
D_FF = 2816
HEAD_DIM = 64
N_Q_HEADS = 8
N_KV_HEADS = 2
WINDOW = 128
Q_WIDTH = N_Q_HEADS * HEAD_DIM
KV_WIDTH = N_KV_HEADS * HEAD_DIM
NUM_BUCKETS = 32
MAX_DISTANCE = 128
CONV_CHANNELS = 512
CONV_WIDTH = 31
EVEN_IN = Q_WIDTH + 2 * KV_WIDTH + 2 * CONV_CHANNELS
LRU_WIDTH = 1024
LRU_HEADS = 8
LRU_BLOCK = 128
LRU_CONV_WIDTH = 4
RG_LRU_C = 8.0
RMS_EPS = 1e-6
LN_EPS = 1e-5
NEG_INF = -1e30

LANES = 128
SUBLANES = 8
MXU_DIM = 256
VMEM_LIMIT_BYTES = 56 * 1024 * 1024

FF_CHUNK = MXU_DIM
N_FF_CHUNKS = D_FF // FF_CHUNK
FFN_ROWS = 1024
ROW_CHUNK = 64
MIX_BATCH = SUBLANES
STEPS_PER_CHUNK = ROW_CHUNK // MIX_BATCH
EVEN_STEPS = WINDOW
ODD_STEPS = 64
CONV_HALO = 32
LRU_HALO = LRU_CONV_WIDTH - 1
SCAN_ROWS = 2 * MIX_BATCH

BF16 = jnp.bfloat16
F32 = jnp.float32


def _resident(shape):
    nd = len(shape)
    return pl.BlockSpec(shape, lambda *_: (0,) * nd, pipeline_mode=pl.Buffered(1))


def _sigmoid(x):
    return 0.5 * jnp.tanh(0.5 * x) + 0.5


def _rmsnorm_rows(x, g):
    ms = jnp.mean(x * x, axis=-1, keepdims=True)
    return x * lax.rsqrt(ms + RMS_EPS) * g


def _norm_tile(h_ref, g_ref, hn_scr):
    steps = h_ref.shape[0]
    for i in range(steps // STEPS_PER_CHUNK):
        x = h_ref[i * STEPS_PER_CHUNK:(i + 1) * STEPS_PER_CHUNK].reshape(ROW_CHUNK, D_MODEL)
        hn_scr[i * ROW_CHUNK:(i + 1) * ROW_CHUNK, :] = _rmsnorm_rows(x, g_ref[...]).astype(BF16)


def _ffn_kernel(x_ref, xnext_ref, g_ref, wg_ref, wu_ref, wd_ref, o_ref, xn_scr, a_scr, *,
                batch_major_in):
    rows = xn_scr.shape[1]
    i = pl.program_id(0)
    slot = lax.rem(i, 2)

    def load_rows(src_ref, r0, n):
        if batch_major_in:
            steps = src_ref.shape[1]
            return src_ref[r0 // steps, r0 % steps:r0 % steps + n, :]
        return src_ref[r0:r0 + n, :]

    def norm_chunk(src_ref, dst_slot, k):
        x = load_rows(src_ref, k * ROW_CHUNK, ROW_CHUNK)
        xn_scr[dst_slot, k * ROW_CHUNK:(k + 1) * ROW_CHUNK, :] = (
            _rmsnorm_rows(x, g_ref[...]).astype(BF16))

    n_norm = rows // ROW_CHUNK
    norm_per_dot = pl.cdiv(n_norm, N_FF_CHUNKS)

    @pl.when(i == 0)
    def _():
        for k in range(n_norm):
            norm_chunk(x_ref, 0, k)

    for c in range(N_FF_CHUNKS):
        cols = slice(c * FF_CHUNK, (c + 1) * FF_CHUNK)
        g = jnp.dot(xn_scr[slot], wg_ref[:, cols], preferred_element_type=F32)
        u = jnp.dot(xn_scr[slot], wu_ref[:, cols], preferred_element_type=F32)
        a_scr[:, cols] = (g * _sigmoid(g) * u).astype(BF16)
        for k in range(c * norm_per_dot, min((c + 1) * norm_per_dot, n_norm)):
            norm_chunk(xnext_ref, 1 - slot, k)
    y = jnp.dot(a_scr[...], wd_ref[...], preferred_element_type=F32)
    if batch_major_in:
        steps = x_ref.shape[1]
        for b in range(x_ref.shape[0]):
            o_ref[:, b, :] = x_ref[b] + 0.5 * y[b * steps:(b + 1) * steps, :]
    else:
        o_ref[...] = x_ref[...] + 0.5 * y


def _layer_weight(w, layer):
    return pl.BlockSpec((None,) + w.shape[1:], lambda *_: (layer, 0, 0),
                        pipeline_mode=pl.Buffered(1))


def _ffn_call(x, gain, wg, wu, wd, layer, batch_major_in=False):
    if batch_major_in:
        bsz, seq, _ = x.shape
        steps = FFN_ROWS // MIX_BATCH
        n_st = seq // steps
        n_tiles = (bsz // MIX_BATCH) * n_st
        block = (MIX_BATCH, steps, D_MODEL)
        x_spec = pl.BlockSpec(block, lambda i: (i // n_st, i % n_st, 0))

        def next_map(i):
            j = jnp.minimum(i + 1, n_tiles - 1)
            return (j // n_st, j % n_st, 0)

        xnext_spec = pl.BlockSpec(block, next_map)
        out_shape = jax.ShapeDtypeStruct((seq, bsz, D_MODEL), F32)
        out_spec = pl.BlockSpec((steps, MIX_BATCH, D_MODEL), lambda i: (i % n_st, i // n_st, 0))
    else:
        seq, bsz, _ = x.shape
        x = x.reshape(seq * bsz, D_MODEL)
        n_tiles = x.shape[0] // FFN_ROWS
        x_spec = pl.BlockSpec((FFN_ROWS, D_MODEL), lambda i: (i, 0))
        xnext_spec = pl.BlockSpec((FFN_ROWS, D_MODEL),
                                  lambda i: (jnp.minimum(i + 1, n_tiles - 1), 0))
        out_shape = jax.ShapeDtypeStruct(x.shape, F32)
        out_spec = pl.BlockSpec((FFN_ROWS, D_MODEL), lambda i: (i, 0))
    out = pl.pallas_call(
        functools.partial(_ffn_kernel, batch_major_in=batch_major_in),
        out_shape=out_shape,
        grid=(n_tiles,),
        in_specs=[
            x_spec,
            xnext_spec,
            _resident((1, D_MODEL)),
            _layer_weight(wg, layer),
            _layer_weight(wu, layer),
            _layer_weight(wd, layer),
        ],
        out_specs=out_spec,
        scratch_shapes=[
            pltpu.VMEM((2, FFN_ROWS, D_MODEL), BF16),
            pltpu.VMEM((FFN_ROWS, D_FF), BF16),
        ],
        compiler_params=pltpu.CompilerParams(
            dimension_semantics=("arbitrary",), vmem_limit_bytes=VMEM_LIMIT_BYTES),
        name="ffn",
    )(x, x, gain, wg, wu, wd)
    return out.reshape(seq, bsz, D_MODEL)


def _even_kernel(sinks_ref, h_ref, g_ref, win_ref, bias_ref, cw_ref, cb_ref, lng_ref, lnb_ref,
                 wout_ref, o_ref, hn_scr, qkv_scr, kvprev_scr, attn_scr, xpad_scr, cat_scr):
    steps = h_ref.shape[0]
    rows = steps * MIX_BATCH
    halo_rows = CONV_HALO * MIX_BATCH
    s = pl.program_id(1)

    @pl.when(s == 0)
    def _():
        kvprev_scr[...] = jnp.zeros(kvprev_scr.shape, F32)
        xpad_scr[0:halo_rows, :] = jnp.zeros((halo_rows, CONV_CHANNELS), F32)

    _norm_tile(h_ref, g_ref, hn_scr)

    for j in range((Q_WIDTH + 2 * KV_WIDTH) // MXU_DIM):
        res = jnp.dot(hn_scr[...], win_ref[:, j * MXU_DIM:(j + 1) * MXU_DIM],
                      preferred_element_type=F32)
        qkv_scr[2 * j] = res[:, :LANES]
        qkv_scr[2 * j + 1] = res[:, LANES:]
    o_g = Q_WIDTH + 2 * KV_WIDTH
    glu_a = jnp.dot(hn_scr[...], win_ref[:, o_g:o_g + CONV_CHANNELS], preferred_element_type=F32)
    glu_b = jnp.dot(hn_scr[...], win_ref[:, o_g + CONV_CHANNELS:], preferred_element_type=F32)
    xpad_scr[halo_rows:, :] = glu_a * _sigmoid(glu_b)

    k_slab = Q_WIDTH // LANES
    v_slab = k_slab + 1
    lane = lax.broadcasted_iota(jnp.int32, (2 * WINDOW, LANES), 1)
    qlane = lax.broadcasted_iota(jnp.int32, (WINDOW, LANES), 1)
    scale = 1.0 / math.sqrt(HEAD_DIM)
    first = (s == 0).astype(jnp.int32)
    for b in range(MIX_BATCH):
        seq_rows = pl.ds(b, steps, stride=MIX_BATCH)
        k_cur = qkv_scr[k_slab, seq_rows, :]
        v_cur = qkv_scr[v_slab, seq_rows, :]
        k_cat = jnp.concatenate([kvprev_scr[b, 0], k_cur], axis=0)
        v_cat = jnp.concatenate([kvprev_scr[b, 1], v_cur], axis=0)
        kvprev_scr[b, 0] = k_cur
        kvprev_scr[b, 1] = v_cur
        k_sw = pltpu.roll(k_cat, HEAD_DIM, axis=1)
        v_sw = pltpu.roll(v_cat, HEAD_DIM, axis=1)
        k_dup = (jnp.where(lane < HEAD_DIM, k_cat, k_sw).astype(BF16),
                 jnp.where(lane < HEAD_DIM, k_sw, k_cat).astype(BF16))
        v_dup = (jnp.where(lane < HEAD_DIM, v_cat, v_sw).astype(BF16),
                 jnp.where(lane < HEAD_DIM, v_sw, v_cat).astype(BF16))
        for c in range(Q_WIDTH // LANES):
            kvh = (2 * c) // (N_Q_HEADS // N_KV_HEADS)
            qcol = qkv_scr[c, seq_rows, :] * scale
            outs = []
            for half in range(2):
                head = 2 * c + half
                keep = (qlane < HEAD_DIM) if half == 0 else (qlane >= HEAD_DIM)
                q = jnp.where(keep, qcol, 0.0).astype(BF16)
                sc = lax.dot_general(q, k_dup[kvh], (((1,), (1,)), ((), ())),
                                     preferred_element_type=F32)
                sc = sc + bias_ref[first, head]
                sink = sinks_ref[head]
                m = jnp.maximum(jnp.max(sc, axis=-1, keepdims=True), sink)
                e = jnp.exp(sc - m)
                den = jnp.sum(e, axis=-1, keepdims=True) + jnp.exp(sink - m)
                p = (e / den).astype(BF16)
                outs.append(jnp.dot(p, v_dup[kvh], preferred_element_type=F32))
            attn_scr[c, seq_rows, :] = jnp.where(qlane < HEAD_DIM, outs[0], outs[1])
    for c in range(Q_WIDTH // LANES):
        cat_scr[:, c * LANES:(c + 1) * LANES] = attn_scr[c].astype(BF16)

    first_step = CONV_HALO - (CONV_WIDTH - 1)

    for i in range(rows // ROW_CHUNK):
        acc = jnp.zeros((ROW_CHUNK, CONV_CHANNELS), F32)
        for k in range(CONV_WIDTH):
            r0 = (i * STEPS_PER_CHUNK + first_step + k) * MIX_BATCH
            acc = acc + xpad_scr[r0:r0 + ROW_CHUNK, :] * cw_ref[k:k + 1, :]
        y = acc + cb_ref[...]
        mu = jnp.mean(y, axis=-1, keepdims=True)
        yc = y - mu
        var = jnp.mean(yc * yc, axis=-1, keepdims=True)
        z = yc * lax.rsqrt(var + LN_EPS) * lng_ref[...] + lnb_ref[...]
        cat_scr[i * ROW_CHUNK:(i + 1) * ROW_CHUNK, Q_WIDTH:] = (z * _sigmoid(z)).astype(BF16)

    mixed = jnp.dot(cat_scr[...], wout_ref[...], preferred_element_type=F32)
    o_ref[...] = h_ref[...] + mixed.reshape(steps, MIX_BATCH, D_MODEL)

    xpad_scr[0:halo_rows, :] = xpad_scr[rows:rows + halo_rows, :]


def _even_call(h, gain, w_in, sinks, bias_tbl, conv_w, conv_b, ln_g, ln_b, w_out):
    seq, bsz, _ = h.shape
    rows = EVEN_STEPS * MIX_BATCH
    tile = pl.BlockSpec((EVEN_STEPS, MIX_BATCH, D_MODEL), lambda bg, s: (s, bg, 0))
    return pl.pallas_call(
        _even_kernel,
        out_shape=jax.ShapeDtypeStruct(h.shape, F32),
        grid=(bsz // MIX_BATCH, seq // EVEN_STEPS),
        in_specs=[
            pl.BlockSpec(memory_space=pltpu.SMEM),
            tile,
            _resident((1, D_MODEL)),
            _resident(w_in.shape),
            _resident(bias_tbl.shape),
            _resident(conv_w.shape),
            _resident(conv_b.shape),
            _resident(ln_g.shape),
            _resident(ln_b.shape),
            _resident(w_out.shape),
        ],
        out_specs=tile,
        scratch_shapes=[
            pltpu.VMEM((rows, D_MODEL), BF16),
            pltpu.VMEM(((Q_WIDTH + 2 * KV_WIDTH) // LANES, rows, LANES), F32),
            pltpu.VMEM((MIX_BATCH, 2, WINDOW, LANES), F32),
            pltpu.VMEM((Q_WIDTH // LANES, rows, LANES), F32),
            pltpu.VMEM(((CONV_HALO + EVEN_STEPS) * MIX_BATCH, CONV_CHANNELS), F32),
            pltpu.VMEM((rows, D_MODEL), BF16),
        ],
        compiler_params=pltpu.CompilerParams(
            dimension_semantics=("parallel", "arbitrary"), vmem_limit_bytes=VMEM_LIMIT_BYTES),
        name="even_mixer",
    )(sinks, h, gain, w_in, bias_tbl, conv_w, conv_b, ln_g, ln_b, w_out)


def _odd_kernel(h_ref, g_ref, win_ref, cw_ref, cb_ref, wgate_ref, gab_ref, gxb_ref, lam_ref,
                wout_ref, o_ref, hn_scr, ug_scr, xpad_scr, rec_scr, gpre_scr, y_scr, state_scr):
    steps = h_ref.shape[0]
    rows = steps * MIX_BATCH
    halo_rows = LRU_HALO * MIX_BATCH
    s = pl.program_id(1)

    @pl.when(s == 0)
    def _():
        xpad_scr[0:halo_rows, :] = jnp.zeros((halo_rows, LRU_WIDTH), F32)
        state_scr[...] = jnp.zeros((MIX_BATCH, LRU_WIDTH), F32)

    _norm_tile(h_ref, g_ref, hn_scr)
    ug_scr[...] = jnp.dot(hn_scr[...], win_ref[:, 0:LRU_WIDTH], preferred_element_type=F32)
    xpad_scr[halo_rows:, :] = jnp.dot(hn_scr[...], win_ref[:, LRU_WIDTH:],
                                      preferred_element_type=F32)

    for i in range(rows // ROW_CHUNK):
        acc = jnp.zeros((ROW_CHUNK, LRU_WIDTH), F32)
        for k in range(LRU_CONV_WIDTH):
            r0 = (i * STEPS_PER_CHUNK + k) * MIX_BATCH
            acc = acc + xpad_scr[r0:r0 + ROW_CHUNK, :] * cw_ref[k:k + 1, :]
        rec_scr[i * ROW_CHUNK:(i + 1) * ROW_CHUNK, :] = acc + cb_ref[...]

    for hd in range(LRU_HEADS):
        cs = slice(hd * LRU_BLOCK, (hd + 1) * LRU_BLOCK)
        pre = jnp.dot(rec_scr[:, cs].astype(BF16), wgate_ref[hd], preferred_element_type=F32)
        gpre_scr[:, cs] = pre[:, :LRU_BLOCK]
        gpre_scr[:, LRU_WIDTH + hd * LRU_BLOCK:LRU_WIDTH + (hd + 1) * LRU_BLOCK] = pre[:, LRU_BLOCK:]

    lam = lam_ref[...]
    log_sig_lam = -(jnp.maximum(-lam, 0.0) + jnp.log1p(jnp.exp(-jnp.abs(lam))))

    def scan_body(i, hprev):
        r = pl.ds(pl.multiple_of(i * SCAN_ROWS, SCAN_ROWS), SCAN_ROWS)
        x = rec_scr[r, :]
        rg = _sigmoid(gpre_scr[r, 0:LRU_WIDTH] + gab_ref[...])
        ig = _sigmoid(gpre_scr[r, LRU_WIDTH:] + gxb_ref[...])
        log_a = RG_LRU_C * rg * log_sig_lam
        a = jnp.exp(log_a)
        bx = jnp.sqrt(-jnp.tanh(log_a) * (a * a + 1.0)) * (ig * x)
        hs = []
        for t in range(SCAN_ROWS // MIX_BATCH):
            rt = slice(t * MIX_BATCH, (t + 1) * MIX_BATCH)
            hprev = a[rt, :] * hprev + bx[rt, :]
            hs.append(hprev)
        hs = jnp.concatenate(hs, axis=0)
        y_scr[r, :] = (jax.nn.gelu(ug_scr[r, :]) * hs).astype(BF16)
        return hprev

    state_scr[...] = lax.fori_loop(0, rows // SCAN_ROWS, scan_body, state_scr[...])

    mixed = jnp.dot(y_scr[...], wout_ref[...], preferred_element_type=F32)
    o_ref[...] = h_ref[...] + mixed.reshape(steps, MIX_BATCH, D_MODEL)
    xpad_scr[0:halo_rows, :] = xpad_scr[rows:rows + halo_rows, :]


def _odd_call(h, gain, w_in, conv_w, conv_b, w_gate, ga_b, gx_b, lam, w_out):
    seq, bsz, _ = h.shape
    rows = ODD_STEPS * MIX_BATCH
    tile = pl.BlockSpec((ODD_STEPS, MIX_BATCH, D_MODEL), lambda bg, s: (s, bg, 0))
    return pl.pallas_call(
        _odd_kernel,
        out_shape=jax.ShapeDtypeStruct(h.shape, F32),
        grid=(bsz // MIX_BATCH, seq // ODD_STEPS),
        in_specs=[
            tile,
            _resident((1, D_MODEL)),
            _resident(w_in.shape),
            _resident(conv_w.shape),
            _resident(conv_b.shape),
            _resident(w_gate.shape),
            _resident(ga_b.shape),
            _resident(gx_b.shape),
            _resident(lam.shape),
            _resident(w_out.shape),
        ],
        out_specs=tile,
        scratch_shapes=[
            pltpu.VMEM((rows, D_MODEL), BF16),
            pltpu.VMEM((rows, LRU_WIDTH), F32),
            pltpu.VMEM(((LRU_HALO + ODD_STEPS) * MIX_BATCH, LRU_WIDTH), F32),
            pltpu.VMEM((rows, LRU_WIDTH), F32),
            pltpu.VMEM((rows, 2 * LRU_WIDTH), F32),
            pltpu.VMEM((rows, LRU_WIDTH), BF16),
            pltpu.VMEM((MIX_BATCH, LRU_WIDTH), F32),
        ],
        compiler_params=pltpu.CompilerParams(
            dimension_semantics=("parallel", "arbitrary"), vmem_limit_bytes=VMEM_LIMIT_BYTES),
        name="odd_mixer",
    )(h, gain, w_in, conv_w, conv_b, w_gate, ga_b, gx_b, lam, w_out)


def _final_norm_kernel(x_ref, g_ref, o_ref):
    for b in range(o_ref.shape[0]):
        o_ref[b] = _rmsnorm_rows(x_ref[:, b, :], g_ref[...])


def _final_norm_call(h, gain):
    seq, bsz, _ = h.shape
    steps = FFN_ROWS // MIX_BATCH
    return pl.pallas_call(
        _final_norm_kernel,
        out_shape=jax.ShapeDtypeStruct((bsz, seq, D_MODEL), F32),
        grid=(bsz // MIX_BATCH, seq // steps),
        in_specs=[pl.BlockSpec((steps, MIX_BATCH, D_MODEL), lambda bg, st: (st, bg, 0)),
                  _resident((1, D_MODEL))],
        out_specs=pl.BlockSpec((MIX_BATCH, steps, D_MODEL), lambda bg, st: (bg, st, 0)),
        compiler_params=pltpu.CompilerParams(dimension_semantics=("parallel", "parallel")),
        name="final_norm",
    )(h, gain)


def _t5_bucket_table():
    qi = np.arange(WINDOW)[:, None]
    sj = np.arange(2 * WINDOW)[None, :]
    dist = qi + WINDOW - sj
    n = np.maximum(dist, 0)
    max_exact = NUM_BUCKETS // 2
    nf = np.maximum(n, max_exact).astype(np.float32)
    large = max_exact + (np.log(nf / max_exact) / math.log(MAX_DISTANCE / max_exact)
                         * (NUM_BUCKETS - max_exact)).astype(np.int32)
    large = np.minimum(large, NUM_BUCKETS - 1)
    bucket = np.where(n < max_exact, n, large)
    in_window = (dist >= 0) & (dist < WINDOW)
    return bucket, in_window, sj


def _attention_bias_tables(rel_bias):
    bucket, in_window, sj = _t5_bucket_table()
    onehot = (jnp.asarray(bucket, jnp.int32)[..., None] == jnp.arange(NUM_BUCKETS)).astype(F32)
    bias = jnp.einsum("qsb,bh->hqs", onehot, rel_bias.astype(F32),
                      precision=lax.Precision.HIGHEST)
    normal = jnp.where(in_window[None], bias, NEG_INF)
    first = jnp.where((in_window & (sj >= WINDOW))[None], bias, NEG_INF)
    return jnp.stack([normal, first])


def kernel(x, norm_ffn1, ffn1_wg, ffn1_wu, ffn1_wd, norm_mix, norm_ffn2, ffn2_wg, ffn2_wu, ffn2_wd,
           rel_bias, even_w_in, attn_sinks, conv_b_w, conv_b_b, conv_ln_g, conv_ln_b, even_w_out,
           odd_w_in, lru_conv_w, lru_conv_b, gate_a_w, gate_a_b, gate_x_w, gate_x_b, lru_lambda,
           odd_w_out, norm_final):
    bsz, seq, d = x.shape
    depth = norm_ffn1.shape[0]
    bias_tbl = _attention_bias_tables(rel_bias)
    row = lambda v: v.reshape(1, -1)

    bf16 = lambda w: w.astype(BF16)
    ffn1 = (bf16(ffn1_wg), bf16(ffn1_wu), bf16(ffn1_wd))
    ffn2 = (bf16(ffn2_wg), bf16(ffn2_wu), bf16(ffn2_wd))

    h = x
    for layer in range(depth):
        h = _ffn_call(h, row(norm_ffn1[layer]), *ffn1, layer,
                      batch_major_in=(layer == 0))
        if layer % 2 == 0:
            e = layer // 2
            h = _even_call(h, row(norm_mix[layer]), even_w_in[e].astype(BF16), attn_sinks[e],
                           bias_tbl, conv_b_w[e], row(conv_b_b[e]), row(conv_ln_g[e]),
                           row(conv_ln_b[e]), even_w_out[e].astype(BF16))
        else:
            o = layer // 2
            w_gate = jnp.concatenate([gate_a_w[o], gate_x_w[o]], axis=-1).astype(BF16)
            h = _odd_call(h, row(norm_mix[layer]), odd_w_in[o].astype(BF16), lru_conv_w[o],
                          row(lru_conv_b[o]), w_gate, row(gate_a_b[o]), row(gate_x_b[o]),
                          row(lru_lambda[o]), odd_w_out[o].astype(BF16))
        h = _ffn_call(h, row(norm_ffn2[layer]), *ffn2, layer)
    return _final_norm_call(h, row(norm_final))
```

```python
import functools
import math

import jax
import jax.numpy as jnp
import numpy as np
from jax import lax
from jax.experimental import pallas as pl
from jax.experimental.pallas import tpu as pltpu

D_MODEL = 1024
D_FF = 2816
HEAD_DIM = 64
N_Q_HEADS = 8
N_KV_HEADS = 2
GROUP = N_Q_HEADS // N_KV_HEADS
WINDOW = 128
Q_WIDTH = N_Q_HEADS * HEAD_DIM
KV_WIDTH = N_KV_HEADS * HEAD_DIM
NUM_BUCKETS = 32
MAX_DISTANCE = 128
CONV_CHANNELS = 512
CONV_WIDTH = 31
EVEN_IN = Q_WIDTH + 2 * KV_WIDTH + 2 * CONV_CHANNELS
LRU_WIDTH = 1024
LRU_HEADS = 8
LRU_BLOCK = 128
LRU_CONV_WIDTH = 4
RG_LRU_C = 8.0
RMS_EPS = 1e-6
LN_EPS = 1e-5
NEG_INF = -1e30

LANES = 128
SUBLANES = 8
MXU_DIM = 256
VMEM_LIMIT_BYTES = 56 * 1024 * 1024
FUSED_VMEM_LIMIT_BYTES = 60 * 1024 * 1024

FF_CHUNK = MXU_DIM
N_FF_CHUNKS = D_FF // FF_CHUNK
FFN_ROWS = 1024
ROW_CHUNK = 64
MIX_BATCH = SUBLANES
STEPS_PER_CHUNK = ROW_CHUNK // MIX_BATCH
EVEN_STEPS = WINDOW
ODD_STEPS = 64
CONV_HALO = 32
LRU_HALO = LRU_CONV_WIDTH - 1
SCAN_ROWS = 2 * MIX_BATCH
CONV_ROWS = 64

BF16 = jnp.bfloat16
F32 = jnp.float32


def _resident(shape):
    nd = len(shape)
    return pl.BlockSpec(shape, lambda *_: (0,) * nd, pipeline_mode=pl.Buffered(1))


def _sigmoid(x):
    return 0.5 * jnp.tanh(0.5 * x) + 0.5


def _rmsnorm_rows(x, g):
    ms = jnp.mean(x * x, axis=-1, keepdims=True)
    return x * lax.rsqrt(ms + RMS_EPS) * g


def _norm_tile(h_ref, g_ref, hn_scr):
    steps = h_ref.shape[0]
    for i in range(steps // STEPS_PER_CHUNK):
        x = h_ref[i * STEPS_PER_CHUNK:(i + 1) * STEPS_PER_CHUNK].reshape(ROW_CHUNK, D_MODEL)
        hn_scr[i * ROW_CHUNK:(i + 1) * ROW_CHUNK, :] = _rmsnorm_rows(x, g_ref[...]).astype(BF16)


def _ffn_kernel(x_ref, xnext_ref, g_ref, wg_ref, wu_ref, wd_ref, o_ref, xn_scr, a_scr, *,
                batch_major_in):
    rows = xn_scr.shape[1]
    i = pl.program_id(0)
    slot = lax.rem(i, 2)

    def load_rows(src_ref, r0, n):
        if batch_major_in:
            steps = src_ref.shape[1]
            return src_ref[r0 // steps, r0 % steps:r0 % steps + n, :]
        return src_ref[r0:r0 + n, :]

    def norm_chunk(src_ref, dst_slot, k):
        x = load_rows(src_ref, k * ROW_CHUNK, ROW_CHUNK)
        xn_scr[dst_slot, k * ROW_CHUNK:(k + 1) * ROW_CHUNK, :] = (
            _rmsnorm_rows(x, g_ref[...]).astype(BF16))

    n_norm = rows // ROW_CHUNK
    norm_per_dot = pl.cdiv(n_norm, N_FF_CHUNKS)

    @pl.when(i == 0)
    def _():
        for k in range(n_norm):
            norm_chunk(x_ref, 0, k)

    for c in range(N_FF_CHUNKS):
        cols = slice(c * FF_CHUNK, (c + 1) * FF_CHUNK)
        g = jnp.dot(xn_scr[slot], wg_ref[:, cols], preferred_element_type=F32)
        u = jnp.dot(xn_scr[slot], wu_ref[:, cols], preferred_element_type=F32)
        a_scr[:, cols] = (g * _sigmoid(g) * u).astype(BF16)
        for k in range(c * norm_per_dot, min((c + 1) * norm_per_dot, n_norm)):
            norm_chunk(xnext_ref, 1 - slot, k)
    y = jnp.dot(a_scr[...], wd_ref[...], preferred_element_type=F32)
    if batch_major_in:
        steps = x_ref.shape[1]
        for b in range(x_ref.shape[0]):
            o_ref[:, b, :] = x_ref[b] + 0.5 * y[b * steps:(b + 1) * steps, :]
    else:
        o_ref[...] = x_ref[...] + 0.5 * y


def _layer_weight(w, layer):
    return pl.BlockSpec((None,) + w.shape[1:], lambda *_: (layer, 0, 0),
                        pipeline_mode=pl.Buffered(1))


def _ffn_call(x, gain, wg, wu, wd, layer, batch_major_in=False):
    if batch_major_in:
        bsz, seq, _ = x.shape
        steps = FFN_ROWS // MIX_BATCH
        n_st = seq // steps
        n_tiles = (bsz // MIX_BATCH) * n_st
        block = (MIX_BATCH, steps, D_MODEL)
        x_spec = pl.BlockSpec(block, lambda i: (i // n_st, i % n_st, 0))

        def next_map(i):
            j = jnp.minimum(i + 1, n_tiles - 1)
            return (j // n_st, j % n_st, 0)

        xnext_spec = pl.BlockSpec(block, next_map)
        out_shape = jax.ShapeDtypeStruct((seq, bsz, D_MODEL), F32)
        out_spec = pl.BlockSpec((steps, MIX_BATCH, D_MODEL), lambda i: (i % n_st, i // n_st, 0))
    else:
        seq, bsz, _ = x.shape
        x = x.reshape(seq * bsz, D_MODEL)
        n_tiles = x.shape[0] // FFN_ROWS
        x_spec = pl.BlockSpec((FFN_ROWS, D_MODEL), lambda i: (i, 0))
        xnext_spec = pl.BlockSpec((FFN_ROWS, D_MODEL),
                                  lambda i: (jnp.minimum(i + 1, n_tiles - 1), 0))
        out_shape = jax.ShapeDtypeStruct(x.shape, F32)
        out_spec = pl.BlockSpec((FFN_ROWS, D_MODEL), lambda i: (i, 0))
    out = pl.pallas_call(
        functools.partial(_ffn_kernel, batch_major_in=batch_major_in),
        out_shape=out_shape,
        grid=(n_tiles,),
        in_specs=[
            x_spec,
            xnext_spec,
            _resident((1, D_MODEL)),
            _layer_weight(wg, layer),
            _layer_weight(wu, layer),
            _layer_weight(wd, layer),
        ],
        out_specs=out_spec,
        scratch_shapes=[
            pltpu.VMEM((2, FFN_ROWS, D_MODEL), BF16),
            pltpu.VMEM((FFN_ROWS, D_FF), BF16),
        ],
        compiler_params=pltpu.CompilerParams(
            dimension_semantics=("arbitrary",), vmem_limit_bytes=VMEM_LIMIT_BYTES),
        name="ffn",
    )(x, x, gain, wg, wu, wd)
    return out.reshape(seq, bsz, D_MODEL)


def _even_kernel(sinks_ref, h_ref, g_ref, win_ref, bias_ref, cw_ref, cb_ref, lng_ref, lnb_ref,
                 wout_ref, o_ref, hn_scr, qkv_scr, kvprev_scr, attn_scr, xpad_scr, cat_scr):
    steps = h_ref.shape[0]
    rows = steps * MIX_BATCH
    halo_rows = CONV_HALO * MIX_BATCH
    s = pl.program_id(1)

    @pl.when(s == 0)
    def _():
        kvprev_scr[...] = jnp.zeros(kvprev_scr.shape, F32)
        xpad_scr[0:halo_rows, :] = jnp.zeros((halo_rows, CONV_CHANNELS), F32)

    _norm_tile(h_ref, g_ref, hn_scr)

    for j in range((Q_WIDTH + 2 * KV_WIDTH) // MXU_DIM):
        res = jnp.dot(hn_scr[...], win_ref[:, j * MXU_DIM:(j + 1) * MXU_DIM],
                      preferred_element_type=F32)
        qkv_scr[2 * j] = res[:, :LANES]
        qkv_scr[2 * j + 1] = res[:, LANES:]
    o_g = Q_WIDTH + 2 * KV_WIDTH
    glu_a = jnp.dot(hn_scr[...], win_ref[:, o_g:o_g + CONV_CHANNELS], preferred_element_type=F32)
    glu_b = jnp.dot(hn_scr[...], win_ref[:, o_g + CONV_CHANNELS:], preferred_element_type=F32)
    xpad_scr[halo_rows:, :] = glu_a * _sigmoid(glu_b)

    k_slab = Q_WIDTH // LANES
    v_slab = k_slab + 1
    lane = lax.broadcasted_iota(jnp.int32, (2 * WINDOW, LANES), 1)
    qlane = lax.broadcasted_iota(jnp.int32, (WINDOW, LANES), 1)
    scale = 1.0 / math.sqrt(HEAD_DIM)
    first = (s == 0).astype(jnp.int32)
    for b in range(MIX_BATCH):
        seq_rows = pl.ds(b, steps, stride=MIX_BATCH)
        k_cur = qkv_scr[k_slab, seq_rows, :]
        v_cur = qkv_scr[v_slab, seq_rows, :]
        k_cat = jnp.concatenate([kvprev_scr[b, 0], k_cur], axis=0)
        v_cat = jnp.concatenate([kvprev_scr[b, 1], v_cur], axis=0)
        kvprev_scr[b, 0] = k_cur
        kvprev_scr[b, 1] = v_cur
        k_sw = pltpu.roll(k_cat, HEAD_DIM, axis=1)
        v_sw = pltpu.roll(v_cat, HEAD_DIM, axis=1)
        k_dup = (jnp.where(lane < HEAD_DIM, k_cat, k_sw).astype(BF16),
                 jnp.where(lane < HEAD_DIM, k_sw, k_cat).astype(BF16))
        v_dup = (jnp.where(lane < HEAD_DIM, v_cat, v_sw).astype(BF16),
                 jnp.where(lane < HEAD_DIM, v_sw, v_cat).astype(BF16))
        for c in range(Q_WIDTH // LANES):
            kvh = (2 * c) // GROUP
            qcol = qkv_scr[c, seq_rows, :] * scale
            outs = []
            for half in range(2):
                head = 2 * c + half
                keep = (qlane < HEAD_DIM) if half == 0 else (qlane >= HEAD_DIM)
                q = jnp.where(keep, qcol, 0.0).astype(BF16)
                sc = lax.dot_general(q, k_dup[kvh], (((1,), (1,)), ((), ())),
                                     preferred_element_type=F32)
                sc = sc + bias_ref[first, head]
                sink = sinks_ref[head]
                m = jnp.maximum(jnp.max(sc, axis=-1, keepdims=True), sink)
                e = jnp.exp(sc - m)
                den = jnp.sum(e, axis=-1, keepdims=True) + jnp.exp(sink - m)
                p = (e / den).astype(BF16)
                outs.append(jnp.dot(p, v_dup[kvh], preferred_element_type=F32))
            attn_scr[c, seq_rows, :] = jnp.where(qlane < HEAD_DIM, outs[0], outs[1])
    for c in range(Q_WIDTH // LANES):
        cat_scr[:, c * LANES:(c + 1) * LANES] = attn_scr[c].astype(BF16)

    first_step = CONV_HALO - (CONV_WIDTH - 1)

    for i in range(rows // CONV_ROWS):
        acc = jnp.zeros((CONV_ROWS, CONV_CHANNELS), F32)
        for k in range(CONV_WIDTH):
            r0 = i * CONV_ROWS + (first_step + k) * MIX_BATCH
            acc = acc + xpad_scr[r0:r0 + CONV_ROWS, :] * cw_ref[k:k + 1, :]
        y = acc + cb_ref[...]
        mu = jnp.mean(y, axis=-1, keepdims=True)
        yc = y - mu
        var = jnp.mean(yc * yc, axis=-1, keepdims=True)
        z = yc * lax.rsqrt(var + LN_EPS) * lng_ref[...] + lnb_ref[...]
        cat_scr[i * CONV_ROWS:(i + 1) * CONV_ROWS, Q_WIDTH:] = (z * _sigmoid(z)).astype(BF16)

    mixed = jnp.dot(cat_scr[...], wout_ref[...], preferred_element_type=F32)
    o_ref[...] = h_ref[...] + mixed.reshape(steps, MIX_BATCH, D_MODEL)

    xpad_scr[0:halo_rows, :] = xpad_scr[rows:rows + halo_rows, :]


def _even_call(h, gain, w_in, sinks, bias_tbl, conv_w, conv_b, ln_g, ln_b, w_out):
    seq, bsz, _ = h.shape
    rows = EVEN_STEPS * MIX_BATCH
    tile = pl.BlockSpec((EVEN_STEPS, MIX_BATCH, D_MODEL), lambda bg, s: (s, bg, 0))
    return pl.pallas_call(
        _even_kernel,
        out_shape=jax.ShapeDtypeStruct(h.shape, F32),
        grid=(bsz // MIX_BATCH, seq // EVEN_STEPS),
        in_specs=[
            pl.BlockSpec(memory_space=pltpu.SMEM),
            tile,
            _resident((1, D_MODEL)),
            _resident(w_in.shape),
            _resident(bias_tbl.shape),
            _resident(conv_w.shape),
            _resident(conv_b.shape),
            _resident(ln_g.shape),
            _resident(ln_b.shape),
            _resident(w_out.shape),
        ],
        out_specs=tile,
        scratch_shapes=[
            pltpu.VMEM((rows, D_MODEL), BF16),
            pltpu.VMEM(((Q_WIDTH + 2 * KV_WIDTH) // LANES, rows, LANES), F32),
            pltpu.VMEM((MIX_BATCH, 2, WINDOW, LANES), F32),
            pltpu.VMEM((Q_WIDTH // LANES, rows, LANES), F32),
            pltpu.VMEM(((CONV_HALO + EVEN_STEPS) * MIX_BATCH, CONV_CHANNELS), F32),
            pltpu.VMEM((rows, D_MODEL), BF16),
        ],
        compiler_params=pltpu.CompilerParams(
            dimension_semantics=("parallel", "arbitrary"), vmem_limit_bytes=VMEM_LIMIT_BYTES),
        name="even_mixer",
    )(sinks, h, gain, w_in, bias_tbl, conv_w, conv_b, ln_g, ln_b, w_out)


def _ffn_odd_kernel(x_ref, g1_ref, wg_ref, wu_ref, wd_ref, gm_ref, win_ref, cw_ref, cb_ref,
                    wgate_ref, gab_ref, gxb_ref, lam_ref, wout_ref, o_ref,
                    h1_new, h1_prev, xn_scr, a_scr, hn_scr, ug_scr, xpad_scr, rec_scr, gpre_scr,
                    state_scr, *, tiles_per_seq):
    steps = x_ref.shape[0]
    rows = steps * MIX_BATCH
    halo_rows = LRU_HALO * MIX_BATCH
    i = pl.program_id(0)

    @pl.when(i == 0)
    def _():
        h1_prev[...] = jnp.zeros(h1_prev.shape, F32)

    @pl.when(jnp.logical_or(i == 0, lax.rem(i + tiles_per_seq - 1, tiles_per_seq) == 0))
    def _():
        xpad_scr[0:halo_rows, :] = jnp.zeros((halo_rows, LRU_WIDTH), F32)
        state_scr[...] = jnp.zeros((MIX_BATCH, LRU_WIDTH), F32)

    for k in range(rows // ROW_CHUNK):
        r = slice(k * ROW_CHUNK, (k + 1) * ROW_CHUNK)
        hn_scr[r, :] = _rmsnorm_rows(h1_prev[r, :], gm_ref[...]).astype(BF16)
    ug_scr[...] = jnp.dot(hn_scr[...], win_ref[:, 0:LRU_WIDTH], preferred_element_type=F32)
    xpad_scr[halo_rows:, :] = jnp.dot(hn_scr[...], win_ref[:, LRU_WIDTH:],
                                      preferred_element_type=F32)

    for k in range(rows // ROW_CHUNK):
        acc = jnp.zeros((ROW_CHUNK, LRU_WIDTH), F32)
        for tap in range(LRU_CONV_WIDTH):
            r0 = (k * STEPS_PER_CHUNK + tap) * MIX_BATCH
            acc = acc + xpad_scr[r0:r0 + ROW_CHUNK, :] * cw_ref[tap:tap + 1, :]
        rec_scr[k * ROW_CHUNK:(k + 1) * ROW_CHUNK, :] = acc + cb_ref[...]

    for hd in range(LRU_HEADS):
        cs = slice(hd * LRU_BLOCK, (hd + 1) * LRU_BLOCK)
        pre = jnp.dot(rec_scr[:, cs].astype(BF16), wgate_ref[hd], preferred_element_type=F32)
        gpre_scr[:, cs] = pre[:, :LRU_BLOCK]
        gpre_scr[:, LRU_WIDTH + hd * LRU_BLOCK:LRU_WIDTH + (hd + 1) * LRU_BLOCK] = pre[:, LRU_BLOCK:]

    lam = lam_ref[...]
    log_sig_lam = -(jnp.maximum(-lam, 0.0) + jnp.log1p(jnp.exp(-jnp.abs(lam))))

    hprev = state_scr[...]
    for k in range(rows // SCAN_ROWS):
        r = slice(k * SCAN_ROWS, (k + 1) * SCAN_ROWS)
        x = rec_scr[r, :]
        rg = _sigmoid(gpre_scr[r, 0:LRU_WIDTH] + gab_ref[...])
        ig = _sigmoid(gpre_scr[r, LRU_WIDTH:] + gxb_ref[...])
        log_a = RG_LRU_C * rg * log_sig_lam
        a = jnp.exp(log_a)
        bx = jnp.sqrt(-jnp.tanh(log_a) * (a * a + 1.0)) * (ig * x)
        hs = []
        for t in range(SCAN_ROWS // MIX_BATCH):
            rt = slice(t * MIX_BATCH, (t + 1) * MIX_BATCH)
            hprev = a[rt, :] * hprev + bx[rt, :]
            hs.append(hprev)
        hs = jnp.concatenate(hs, axis=0)
        hn_scr[r, :] = (jax.nn.gelu(ug_scr[r, :]) * hs).astype(BF16)
    state_scr[...] = hprev

    mixed = jnp.dot(hn_scr[...], wout_ref[...], preferred_element_type=F32)
    o_ref[...] = (h1_prev[...] + mixed).reshape(steps, MIX_BATCH, D_MODEL)
    xpad_scr[0:halo_rows, :] = xpad_scr[rows:rows + halo_rows, :]

    for k in range(rows // ROW_CHUNK):
        x = x_ref[k * STEPS_PER_CHUNK:(k + 1) * STEPS_PER_CHUNK].reshape(ROW_CHUNK, D_MODEL)
        xn_scr[k * ROW_CHUNK:(k + 1) * ROW_CHUNK, :] = _rmsnorm_rows(x, g1_ref[...]).astype(BF16)
    for c in range(N_FF_CHUNKS):
        cols = slice(c * FF_CHUNK, (c + 1) * FF_CHUNK)
        g = jnp.dot(xn_scr[...], wg_ref[:, cols], preferred_element_type=F32)
        u = jnp.dot(xn_scr[...], wu_ref[:, cols], preferred_element_type=F32)
        a_scr[:, cols] = (g * _sigmoid(g) * u).astype(BF16)
    y = jnp.dot(a_scr[...], wd_ref[...], preferred_element_type=F32)
    h1_new[...] = x_ref[...].reshape(rows, D_MODEL) + 0.5 * y

    h1_prev[...] = h1_new[...]


def _ffn_odd_call(h, gain1, wg, wu, wd, layer, gain_mix, w_in, conv_w, conv_b, w_gate, ga_b, gx_b,
                  lam, w_out):
    seq, bsz, _ = h.shape
    rows = ODD_STEPS * MIX_BATCH
    tiles_per_seq = seq // ODD_STEPS
    n_tiles = (bsz // MIX_BATCH) * tiles_per_seq
    block = (ODD_STEPS, MIX_BATCH, D_MODEL)

    def in_map(i):
        j = jnp.minimum(i, n_tiles - 1)
        return (j % tiles_per_seq, j // tiles_per_seq, 0)

    def out_map(i):
        j = jnp.maximum(i - 1, 0)
        return (j % tiles_per_seq, j // tiles_per_seq, 0)

    return pl.pallas_call(
        functools.partial(_ffn_odd_kernel, tiles_per_seq=tiles_per_seq),
        out_shape=jax.ShapeDtypeStruct(h.shape, F32),
        grid=(n_tiles + 1,),
        in_specs=[
            pl.BlockSpec(block, in_map),
            _resident((1, D_MODEL)),
            _layer_weight(wg, layer),
            _layer_weight(wu, layer),
            _layer_weight(wd, layer),
            _resident((1, D_MODEL)),
            _resident(w_in.shape),
            _resident(conv_w.shape),
            _resident(conv_b.shape),
            _resident(w_gate.shape),
            _resident(ga_b.shape),
            _resident(gx_b.shape),
            _resident(lam.shape),
            _resident(w_out.shape),
        ],
        out_specs=pl.BlockSpec(block, out_map),
        scratch_shapes=[
            pltpu.VMEM((rows, D_MODEL), F32),
            pltpu.VMEM((rows, D_MODEL), F32),
            pltpu.VMEM((rows, D_MODEL), BF16),
            pltpu.VMEM((rows, D_FF), BF16),
            pltpu.VMEM((rows, D_MODEL), BF16),
            pltpu.VMEM((rows, LRU_WIDTH), F32),
            pltpu.VMEM(((LRU_HALO + ODD_STEPS) * MIX_BATCH, LRU_WIDTH), F32),
            pltpu.VMEM((rows, LRU_WIDTH), F32),
            pltpu.VMEM((rows, 2 * LRU_WIDTH), F32),
            pltpu.VMEM((MIX_BATCH, LRU_WIDTH), F32),
        ],
        compiler_params=pltpu.CompilerParams(
            dimension_semantics=("arbitrary",), vmem_limit_bytes=FUSED_VMEM_LIMIT_BYTES),
        name="ffn_odd",
    )(h, gain1, wg, wu, wd, gain_mix, w_in, conv_w, conv_b, w_gate, ga_b, gx_b, lam, w_out)


def _final_norm_kernel(x_ref, g_ref, o_ref):
    for b in range(o_ref.shape[0]):
        o_ref[b] = _rmsnorm_rows(x_ref[:, b, :], g_ref[...])


def _final_norm_call(h, gain):
    seq, bsz, _ = h.shape
    steps = FFN_ROWS // MIX_BATCH
    return pl.pallas_call(
        _final_norm_kernel,
        out_shape=jax.ShapeDtypeStruct((bsz, seq, D_MODEL), F32),
        grid=(bsz // MIX_BATCH, seq // steps),
        in_specs=[pl.BlockSpec((steps, MIX_BATCH, D_MODEL), lambda bg, st: (st, bg, 0)),
                  _resident((1, D_MODEL))],
        out_specs=pl.BlockSpec((MIX_BATCH, steps, D_MODEL), lambda bg, st: (bg, st, 0)),
        compiler_params=pltpu.CompilerParams(dimension_semantics=("parallel", "parallel")),
        name="final_norm",
    )(h, gain)


def _t5_bucket_table():
    qi = np.arange(WINDOW)[:, None]
    sj = np.arange(2 * WINDOW)[None, :]
    dist = qi + WINDOW - sj
    n = np.maximum(dist, 0)
    max_exact = NUM_BUCKETS // 2
    nf = np.maximum(n, max_exact).astype(np.float32)
    large = max_exact + (np.log(nf / max_exact) / math.log(MAX_DISTANCE / max_exact)
                         * (NUM_BUCKETS - max_exact)).astype(np.int32)
    large = np.minimum(large, NUM_BUCKETS - 1)
    bucket = np.where(n < max_exact, n, large)
    in_window = (dist >= 0) & (dist < WINDOW)
    return bucket, in_window, sj


def _attention_bias_tables(rel_bias):
    bucket, in_window, sj = _t5_bucket_table()
    onehot = (jnp.asarray(bucket, jnp.int32)[..., None] == jnp.arange(NUM_BUCKETS)).astype(F32)
    bias = jnp.einsum("qsb,bh->hqs", onehot, rel_bias.astype(F32),
                      precision=lax.Precision.HIGHEST)
    normal = jnp.where(in_window[None], bias, NEG_INF)
    first = jnp.where((in_window & (sj >= WINDOW))[None], bias, NEG_INF)
    return jnp.stack([normal, first])


def kernel(x, norm_ffn1, ffn1_wg, ffn1_wu, ffn1_wd, norm_mix, norm_ffn2, ffn2_wg, ffn2_wu, ffn2_wd,
           rel_bias, even_w_in, attn_sinks, conv_b_w, conv_b_b, conv_ln_g, conv_ln_b, even_w_out,
           odd_w_in, lru_conv_w, lru_conv_b, gate_a_w, gate_a_b, gate_x_w, gate_x_b, lru_lambda,
           odd_w_out, norm_final):
    bsz, seq, d = x.shape
    depth = norm_ffn1.shape[0]
    bias_tbl = _attention_bias_tables(rel_bias)
    row = lambda v: v.reshape(1, -1)

    bf16 = lambda w: w.astype(BF16)
    ffn1 = (bf16(ffn1_wg), bf16(ffn1_wu), bf16(ffn1_wd))
    ffn2 = (bf16(ffn2_wg), bf16(ffn2_wu), bf16(ffn2_wd))

    h = x
    for layer in range(depth):
        if layer % 2 == 0:
            e = layer // 2
            h = _ffn_call(h, row(norm_ffn1[layer]), *ffn1, layer,
                          batch_major_in=(layer == 0))
            h = _even_call(h, row(norm_mix[layer]), even_w_in[e].astype(BF16), attn_sinks[e],
                           bias_tbl, conv_b_w[e], row(conv_b_b[e]), row(conv_ln_g[e]),
                           row(conv_ln_b[e]), even_w_out[e].astype(BF16))
        else:
            o = layer // 2
            w_gate = jnp.concatenate([gate_a_w[o], gate_x_w[o]], axis=-1).astype(BF16)
            h = _ffn_odd_call(h, row(norm_ffn1[layer]), *ffn1, layer, row(norm_mix[layer]),
                              odd_w_in[o].astype(BF16), lru_conv_w[o], row(lru_conv_b[o]), w_gate,
                              row(gate_a_b[o]), row(gate_x_b[o]), row(lru_lambda[o]),
                              odd_w_out[o].astype(BF16))
        h = _ffn_call(h, row(norm_ffn2[layer]), *ffn2, layer)
    return _final_norm_call(h, row(norm_final))
```

```python
import functools
import math

import jax
import jax.numpy as jnp
import numpy as np
from jax import lax
from jax.experimental import pallas as pl
from jax.experimental.pallas import tpu as pltpu

D_MODEL = 1024
D_FF = 2816
HEAD_DIM = 64
N_Q_HEADS = 8
N_KV_HEADS = 2
GROUP = N_Q_HEADS // N_KV_HEADS
WINDOW = 128
Q_WIDTH = N_Q_HEADS * HEAD_DIM
KV_WIDTH = N_KV_HEADS * HEAD_DIM
NUM_BUCKETS = 32
MAX_DISTANCE = 128
CONV_CHANNELS = 512
CONV_WIDTH = 31
EVEN_IN = Q_WIDTH + 2 * KV_WIDTH + 2 * CONV_CHANNELS
LRU_WIDTH = 1024
LRU_HEADS = 8
LRU_BLOCK = 128
LRU_CONV_WIDTH = 4
RG_LRU_C = 8.0
RMS_EPS = 1e-6
LN_EPS = 1e-5
NEG_INF = -1e30

LANES = 128
SUBLANES = 8
MXU_DIM = 256
VMEM_LIMIT_BYTES = 56 * 1024 * 1024
FUSED_VMEM_LIMIT_BYTES = 60 * 1024 * 1024

FF_CHUNK = MXU_DIM
N_FF_CHUNKS = D_FF // FF_CHUNK
FFN_ROWS = 1024
ROW_CHUNK = 64
MIX_BATCH = SUBLANES
STEPS_PER_CHUNK = ROW_CHUNK // MIX_BATCH
EVEN_STEPS = 64
ODD_STEPS = 64
CONV_HALO = 32
LRU_HALO = LRU_CONV_WIDTH - 1
SCAN_ROWS = 2 * MIX_BATCH
CONV_ROWS = 64

BF16 = jnp.bfloat16
F32 = jnp.float32


def _resident(shape):
    nd = len(shape)
    return pl.BlockSpec(shape, lambda *_: (0,) * nd, pipeline_mode=pl.Buffered(1))


def _sigmoid(x):
    return 0.5 * jnp.tanh(0.5 * x) + 0.5


def _rmsnorm_rows(x, g):
    ms = jnp.mean(x * x, axis=-1, keepdims=True)
    return x * lax.rsqrt(ms + RMS_EPS) * g


def _norm_tile(h_ref, g_ref, hn_scr):
    steps = h_ref.shape[0]
    for i in range(steps // STEPS_PER_CHUNK):
        x = h_ref[i * STEPS_PER_CHUNK:(i + 1) * STEPS_PER_CHUNK].reshape(ROW_CHUNK, D_MODEL)
        hn_scr[i * ROW_CHUNK:(i + 1) * ROW_CHUNK, :] = _rmsnorm_rows(x, g_ref[...]).astype(BF16)


def _ffn_kernel(x_ref, xnext_ref, g_ref, wg_ref, wu_ref, wd_ref, o_ref, xn_scr, a_scr, *,
                batch_major_in):
    rows = xn_scr.shape[1]
    i = pl.program_id(0)
    slot = lax.rem(i, 2)

    def load_rows(src_ref, r0, n):
        if batch_major_in:
            steps = src_ref.shape[1]
            return src_ref[r0 // steps, r0 % steps:r0 % steps + n, :]
        return src_ref[r0:r0 + n, :]

    def norm_chunk(src_ref, dst_slot, k):
        x = load_rows(src_ref, k * ROW_CHUNK, ROW_CHUNK)
        xn_scr[dst_slot, k * ROW_CHUNK:(k + 1) * ROW_CHUNK, :] = (
            _rmsnorm_rows(x, g_ref[...]).astype(BF16))

    n_norm = rows // ROW_CHUNK
    norm_per_dot = pl.cdiv(n_norm, N_FF_CHUNKS)

    @pl.when(i == 0)
    def _():
        for k in range(n_norm):
            norm_chunk(x_ref, 0, k)

    for c in range(N_FF_CHUNKS):
        cols = slice(c * FF_CHUNK, (c + 1) * FF_CHUNK)
        g = jnp.dot(xn_scr[slot], wg_ref[:, cols], preferred_element_type=F32)
        u = jnp.dot(xn_scr[slot], wu_ref[:, cols], preferred_element_type=F32)
        a_scr[:, cols] = (g * _sigmoid(g) * u).astype(BF16)
        for k in range(c * norm_per_dot, min((c + 1) * norm_per_dot, n_norm)):
            norm_chunk(xnext_ref, 1 - slot, k)
    y = jnp.dot(a_scr[...], wd_ref[...], preferred_element_type=F32)
    if batch_major_in:
        steps = x_ref.shape[1]
        for b in range(x_ref.shape[0]):
            o_ref[:, b, :] = x_ref[b] + 0.5 * y[b * steps:(b + 1) * steps, :]
    else:
        o_ref[...] = x_ref[...] + 0.5 * y


def _layer_weight(w, layer):
    return pl.BlockSpec((None,) + w.shape[1:], lambda *_: (layer, 0, 0),
                        pipeline_mode=pl.Buffered(1))


def _ffn_call(x, gain, wg, wu, wd, layer, batch_major_in=False):
    if batch_major_in:
        bsz, seq, _ = x.shape
        steps = FFN_ROWS // MIX_BATCH
        n_st = seq // steps
        n_tiles = (bsz // MIX_BATCH) * n_st
        block = (MIX_BATCH, steps, D_MODEL)
        x_spec = pl.BlockSpec(block, lambda i: (i // n_st, i % n_st, 0))

        def next_map(i):
            j = jnp.minimum(i + 1, n_tiles - 1)
            return (j // n_st, j % n_st, 0)

        xnext_spec = pl.BlockSpec(block, next_map)
        out_shape = jax.ShapeDtypeStruct((seq, bsz, D_MODEL), F32)
        out_spec = pl.BlockSpec((steps, MIX_BATCH, D_MODEL), lambda i: (i % n_st, i // n_st, 0))
    else:
        seq, bsz, _ = x.shape
        x = x.reshape(seq * bsz, D_MODEL)
        n_tiles = x.shape[0] // FFN_ROWS
        x_spec = pl.BlockSpec((FFN_ROWS, D_MODEL), lambda i: (i, 0))
        xnext_spec = pl.BlockSpec((FFN_ROWS, D_MODEL),
                                  lambda i: (jnp.minimum(i + 1, n_tiles - 1), 0))
        out_shape = jax.ShapeDtypeStruct(x.shape, F32)
        out_spec = pl.BlockSpec((FFN_ROWS, D_MODEL), lambda i: (i, 0))
    out = pl.pallas_call(
        functools.partial(_ffn_kernel, batch_major_in=batch_major_in),
        out_shape=out_shape,
        grid=(n_tiles,),
        in_specs=[
            x_spec,
            xnext_spec,
            _resident((1, D_MODEL)),
            _layer_weight(wg, layer),
            _layer_weight(wu, layer),
            _layer_weight(wd, layer),
        ],
        out_specs=out_spec,
        scratch_shapes=[
            pltpu.VMEM((2, FFN_ROWS, D_MODEL), BF16),
            pltpu.VMEM((FFN_ROWS, D_FF), BF16),
        ],
        compiler_params=pltpu.CompilerParams(
            dimension_semantics=("arbitrary",), vmem_limit_bytes=VMEM_LIMIT_BYTES),
        name="ffn",
    )(x, x, gain, wg, wu, wd)
    return out.reshape(seq, bsz, D_MODEL)


def _ffn_odd_kernel(x_ref, g1_ref, wg_ref, wu_ref, wd_ref, gm_ref, win_ref, cw_ref, cb_ref,
                    wgate_ref, gab_ref, gxb_ref, lam_ref, wout_ref, o_ref,
                    h1_new, h1_prev, xn_scr, a_scr, hn_scr, ug_scr, xpad_scr, rec_scr, gpre_scr,
                    state_scr, *, tiles_per_seq):
    steps = x_ref.shape[0]
    rows = steps * MIX_BATCH
    halo_rows = LRU_HALO * MIX_BATCH
    i = pl.program_id(0)

    @pl.when(i == 0)
    def _():
        h1_prev[...] = jnp.zeros(h1_prev.shape, F32)

    @pl.when(jnp.logical_or(i == 0, lax.rem(i + tiles_per_seq - 1, tiles_per_seq) == 0))
    def _():
        xpad_scr[0:halo_rows, :] = jnp.zeros((halo_rows, LRU_WIDTH), F32)
        state_scr[...] = jnp.zeros((MIX_BATCH, LRU_WIDTH), F32)

    for k in range(rows // ROW_CHUNK):
        r = slice(k * ROW_CHUNK, (k + 1) * ROW_CHUNK)
        hn_scr[r, :] = _rmsnorm_rows(h1_prev[r, :], gm_ref[...]).astype(BF16)
    ug_scr[...] = jnp.dot(hn_scr[...], win_ref[:, 0:LRU_WIDTH], preferred_element_type=F32)
    xpad_scr[halo_rows:, :] = jnp.dot(hn_scr[...], win_ref[:, LRU_WIDTH:],
                                      preferred_element_type=F32)

    for k in range(rows // ROW_CHUNK):
        acc = jnp.zeros((ROW_CHUNK, LRU_WIDTH), F32)
        for tap in range(LRU_CONV_WIDTH):
            r0 = (k * STEPS_PER_CHUNK + tap) * MIX_BATCH
            acc = acc + xpad_scr[r0:r0 + ROW_CHUNK, :] * cw_ref[tap:tap + 1, :]
        rec_scr[k * ROW_CHUNK:(k + 1) * ROW_CHUNK, :] = acc + cb_ref[...]

    for hd in range(LRU_HEADS):
        cs = slice(hd * LRU_BLOCK, (hd + 1) * LRU_BLOCK)
        pre = jnp.dot(rec_scr[:, cs].astype(BF16), wgate_ref[hd], preferred_element_type=F32)
        gpre_scr[:, cs] = pre[:, :LRU_BLOCK]
        gpre_scr[:, LRU_WIDTH + hd * LRU_BLOCK:LRU_WIDTH + (hd + 1) * LRU_BLOCK] = pre[:, LRU_BLOCK:]

    for k in range(rows // ROW_CHUNK):
        x = x_ref[k * STEPS_PER_CHUNK:(k + 1) * STEPS_PER_CHUNK].reshape(ROW_CHUNK, D_MODEL)
        xn_scr[k * ROW_CHUNK:(k + 1) * ROW_CHUNK, :] = _rmsnorm_rows(x, g1_ref[...]).astype(BF16)
    for c in range(N_FF_CHUNKS):
        cols = slice(c * FF_CHUNK, (c + 1) * FF_CHUNK)
        g = jnp.dot(xn_scr[...], wg_ref[:, cols], preferred_element_type=F32)
        u = jnp.dot(xn_scr[...], wu_ref[:, cols], preferred_element_type=F32)
        a_scr[:, cols] = (g * _sigmoid(g) * u).astype(BF16)
    y = jnp.dot(a_scr[...], wd_ref[...], preferred_element_type=F32)
    h1_new[...] = x_ref[...].reshape(rows, D_MODEL) + 0.5 * y

    lam = lam_ref[...]
    log_sig_lam = -(jnp.maximum(-lam, 0.0) + jnp.log1p(jnp.exp(-jnp.abs(lam))))

    hprev = state_scr[...]
    for k in range(rows // SCAN_ROWS):
        r = slice(k * SCAN_ROWS, (k + 1) * SCAN_ROWS)
        x = rec_scr[r, :]
        rg = _sigmoid(gpre_scr[r, 0:LRU_WIDTH] + gab_ref[...])
        ig = _sigmoid(gpre_scr[r, LRU_WIDTH:] + gxb_ref[...])
        log_a = RG_LRU_C * rg * log_sig_lam
        a = jnp.exp(log_a)
        bx = jnp.sqrt(-jnp.tanh(log_a) * (a * a + 1.0)) * (ig * x)
        hs = []
        for t in range(SCAN_ROWS // MIX_BATCH):
            rt = slice(t * MIX_BATCH, (t + 1) * MIX_BATCH)
            hprev = a[rt, :] * hprev + bx[rt, :]
            hs.append(hprev)
        hs = jnp.concatenate(hs, axis=0)
        hn_scr[r, :] = (jax.nn.gelu(ug_scr[r, :]) * hs).astype(BF16)
    state_scr[...] = hprev

    mixed = jnp.dot(hn_scr[...], wout_ref[...], preferred_element_type=F32)
    o_ref[...] = (h1_prev[...] + mixed).reshape(steps, MIX_BATCH, D_MODEL)
    xpad_scr[0:halo_rows, :] = xpad_scr[rows:rows + halo_rows, :]

    h1_prev[...] = h1_new[...]


def _ffn_odd_call(h, gain1, wg, wu, wd, layer, gain_mix, w_in, conv_w, conv_b, w_gate, ga_b, gx_b,
                  lam, w_out):
    seq, bsz, _ = h.shape
    rows = ODD_STEPS * MIX_BATCH
    tiles_per_seq = seq // ODD_STEPS
    n_tiles = (bsz // MIX_BATCH) * tiles_per_seq
    block = (ODD_STEPS, MIX_BATCH, D_MODEL)

    def in_map(i):
        j = jnp.minimum(i, n_tiles - 1)
        return (j % tiles_per_seq, j // tiles_per_seq, 0)

    def out_map(i):
        j = jnp.maximum(i - 1, 0)
        return (j % tiles_per_seq, j // tiles_per_seq, 0)

    return pl.pallas_call(
        functools.partial(_ffn_odd_kernel, tiles_per_seq=tiles_per_seq),
        out_shape=jax.ShapeDtypeStruct(h.shape, F32),
        grid=(n_tiles + 1,),
        in_specs=[
            pl.BlockSpec(block, in_map),
            _resident((1, D_MODEL)),
            _layer_weight(wg, layer),
            _layer_weight(wu, layer),
            _layer_weight(wd, layer),
            _resident((1, D_MODEL)),
            _resident(w_in.shape),
            _resident(conv_w.shape),
            _resident(conv_b.shape),
            _resident(w_gate.shape),
            _resident(ga_b.shape),
            _resident(gx_b.shape),
            _resident(lam.shape),
            _resident(w_out.shape),
        ],
        out_specs=pl.BlockSpec(block, out_map),
        scratch_shapes=[
            pltpu.VMEM((rows, D_MODEL), F32),
            pltpu.VMEM((rows, D_MODEL), F32),
            pltpu.VMEM((rows, D_MODEL), BF16),
            pltpu.VMEM((rows, D_FF), BF16),
            pltpu.VMEM((rows, D_MODEL), BF16),
            pltpu.VMEM((rows, LRU_WIDTH), F32),
            pltpu.VMEM(((LRU_HALO + ODD_STEPS) * MIX_BATCH, LRU_WIDTH), F32),
            pltpu.VMEM((rows, LRU_WIDTH), F32),
            pltpu.VMEM((rows, 2 * LRU_WIDTH), F32),
            pltpu.VMEM((MIX_BATCH, LRU_WIDTH), F32),
        ],
        compiler_params=pltpu.CompilerParams(
            dimension_semantics=("arbitrary",), vmem_limit_bytes=FUSED_VMEM_LIMIT_BYTES),
        name="ffn_odd",
    )(h, gain1, wg, wu, wd, gain_mix, w_in, conv_w, conv_b, w_gate, ga_b, gx_b, lam, w_out)


def _ffn_even_kernel(sinks_ref, x_ref, g1_ref, wg_ref, wu_ref, wd_ref, gm_ref, win_ref, bias_ref,
                     cw_ref, cb_ref, lng_ref, lnb_ref, wout_ref, o_ref,
                     h1_new, h1_prev, xn_scr, a_scr, hn_scr, qkv_scr, qseq_scr, kvprev_scr, oseq_scr,
                     attn_scr, xpad_scr, *, tiles_per_seq, batch_major_in):
    steps = o_ref.shape[0]
    rows = steps * MIX_BATCH
    halo_rows = CONV_HALO * MIX_BATCH
    n_keys = WINDOW + steps
    i = pl.program_id(0)
    seq_tile = lax.rem(i + tiles_per_seq - 1, tiles_per_seq)

    @pl.when(i == 0)
    def _():
        h1_prev[...] = jnp.zeros(h1_prev.shape, F32)

    @pl.when(jnp.logical_or(i == 0, seq_tile == 0))
    def _():
        kvprev_scr[...] = jnp.zeros(kvprev_scr.shape, F32)
        xpad_scr[0:halo_rows, :] = jnp.zeros((halo_rows, CONV_CHANNELS), F32)

    def ffn_norm():
        for k in range(rows // ROW_CHUNK):
            if batch_major_in:
                r0 = k * ROW_CHUNK
                x = x_ref[r0 // steps, r0 % steps:r0 % steps + ROW_CHUNK, :]
            else:
                x = x_ref[k * STEPS_PER_CHUNK:(k + 1) * STEPS_PER_CHUNK].reshape(ROW_CHUNK,
                                                                                 D_MODEL)
            xn_scr[k * ROW_CHUNK:(k + 1) * ROW_CHUNK, :] = (
                _rmsnorm_rows(x, g1_ref[...]).astype(BF16))

    def ffn_up(c0, c1):
        for c in range(c0, c1):
            cols = slice(c * FF_CHUNK, (c + 1) * FF_CHUNK)
            g = jnp.dot(xn_scr[...], wg_ref[:, cols], preferred_element_type=F32)
            u = jnp.dot(xn_scr[...], wu_ref[:, cols], preferred_element_type=F32)
            a_scr[:, cols] = (g * _sigmoid(g) * u).astype(BF16)

    def ffn_down():
        y = jnp.dot(a_scr[...], wd_ref[...], preferred_element_type=F32)
        if batch_major_in:
            for b in range(MIX_BATCH):
                h1_new[:, b, :] = x_ref[b] + 0.5 * y[b * steps:(b + 1) * steps, :]
        else:
            h1_new[...] = x_ref[...] + 0.5 * y.reshape(steps, MIX_BATCH, D_MODEL)

    _norm_tile(h1_prev, gm_ref, hn_scr)
    ffn_norm()

    for j in range((Q_WIDTH + 2 * KV_WIDTH) // MXU_DIM):
        res = jnp.dot(hn_scr[...], win_ref[:, j * MXU_DIM:(j + 1) * MXU_DIM],
                      preferred_element_type=F32)
        qkv_scr[2 * j] = res[:, :LANES]
        qkv_scr[2 * j + 1] = res[:, LANES:]
    o_g = Q_WIDTH + 2 * KV_WIDTH
    glu_a = jnp.dot(hn_scr[...], win_ref[:, o_g:o_g + CONV_CHANNELS], preferred_element_type=F32)
    glu_b = jnp.dot(hn_scr[...], win_ref[:, o_g + CONV_CHANNELS:], preferred_element_type=F32)
    xpad_scr[halo_rows:, :] = glu_a * _sigmoid(glu_b)

    k_slab = Q_WIDTH // LANES
    v_slab = k_slab + 1
    lane = lax.broadcasted_iota(jnp.int32, (n_keys, LANES), 1)
    qlane = lax.broadcasted_iota(jnp.int32, (steps, LANES), 1)
    scale = 1.0 / math.sqrt(HEAD_DIM)
    variant = jnp.minimum(seq_tile, WINDOW // steps)
    for b in range(MIX_BATCH):
        for j in range(qkv_scr.shape[0]):
            qseq_scr[b, j] = qkv_scr[j, pl.ds(b, steps, stride=MIX_BATCH), :]
    score_blocks = []
    for b in range(MIX_BATCH):
        k_cat = jnp.concatenate([kvprev_scr[b, 0], qseq_scr[b, k_slab]], axis=0)
        v_cat = jnp.concatenate([kvprev_scr[b, 1], qseq_scr[b, v_slab]], axis=0)
        kvprev_scr[b, 0] = k_cat[steps:, :]
        kvprev_scr[b, 1] = v_cat[steps:, :]
        k_sw = pltpu.roll(k_cat, HEAD_DIM, axis=1)
        v_sw = pltpu.roll(v_cat, HEAD_DIM, axis=1)
        k_dup = (jnp.where(lane < HEAD_DIM, k_cat, k_sw).astype(BF16),
                 jnp.where(lane < HEAD_DIM, k_sw, k_cat).astype(BF16))
        v_dup = (jnp.where(lane < HEAD_DIM, v_cat, v_sw).astype(BF16),
                 jnp.where(lane < HEAD_DIM, v_sw, v_cat).astype(BF16))
        for c in range(Q_WIDTH // LANES):
            kvh = (2 * c) // GROUP
            qcol = qseq_scr[b, c] * scale
            for half in range(2):
                keep = (qlane < HEAD_DIM) if half == 0 else (qlane >= HEAD_DIM)
                q = jnp.where(keep, qcol, 0.0).astype(BF16)
                sc = lax.dot_general(q, k_dup[kvh], (((1,), (1,)), ((), ())),
                                     preferred_element_type=F32)
                score_blocks.append((2 * c + half, sc, v_dup[kvh]))
    ffn_up(0, N_FF_CHUNKS // 2)
    prob_blocks = []
    for head, sc, v in score_blocks:
        sc = sc + bias_ref[variant, head]
        sink = sinks_ref[head]
        m = jnp.maximum(jnp.max(sc, axis=-1, keepdims=True), sink)
        e = jnp.exp(sc - m)
        den = jnp.sum(e, axis=-1, keepdims=True) + jnp.exp(sink - m)
        prob_blocks.append(((e / den).astype(BF16), v))
    outs = [jnp.dot(p, v, preferred_element_type=F32) for p, v in prob_blocks]
    ffn_up(N_FF_CHUNKS // 2, N_FF_CHUNKS)
    ffn_down()
    for b in range(MIX_BATCH):
        for c in range(Q_WIDTH // LANES):
            n = (b * (Q_WIDTH // LANES) + c) * 2
            oseq_scr[b, c] = jnp.where(qlane < HEAD_DIM, outs[n], outs[n + 1])
    for b in range(MIX_BATCH):
        for c in range(Q_WIDTH // LANES):
            attn_scr[c, pl.ds(b, steps, stride=MIX_BATCH), :] = oseq_scr[b, c]
    for c in range(Q_WIDTH // LANES):
        hn_scr[:, c * LANES:(c + 1) * LANES] = attn_scr[c].astype(BF16)

    first_step = CONV_HALO - (CONV_WIDTH - 1)
    for k in range(rows // ROW_CHUNK):
        acc = jnp.zeros((ROW_CHUNK, CONV_CHANNELS), F32)
        for tap in range(CONV_WIDTH):
            r0 = (k * STEPS_PER_CHUNK + first_step + tap) * MIX_BATCH
            acc = acc + xpad_scr[r0:r0 + ROW_CHUNK, :] * cw_ref[tap:tap + 1, :]
        y = acc + cb_ref[...]
        mu = jnp.mean(y, axis=-1, keepdims=True)
        yc = y - mu
        var = jnp.mean(yc * yc, axis=-1, keepdims=True)
        z = yc * lax.rsqrt(var + LN_EPS) * lng_ref[...] + lnb_ref[...]
        hn_scr[k * ROW_CHUNK:(k + 1) * ROW_CHUNK, Q_WIDTH:] = (z * _sigmoid(z)).astype(BF16)

    mixed = jnp.dot(hn_scr[...], wout_ref[...], preferred_element_type=F32)
    o_ref[...] = h1_prev[...] + mixed.reshape(steps, MIX_BATCH, D_MODEL)
    xpad_scr[0:halo_rows, :] = xpad_scr[rows:rows + halo_rows, :]

    h1_prev[...] = h1_new[...]


def _ffn_even_call(x, gain1, wg, wu, wd, layer, gain_mix, w_in, sinks, bias_tbl, conv_w, conv_b,
                   ln_g, ln_b, w_out, batch_major_in=False):
    if batch_major_in:
        bsz, seq, _ = x.shape
    else:
        seq, bsz, _ = x.shape
    rows = EVEN_STEPS * MIX_BATCH
    tiles_per_seq = seq // EVEN_STEPS
    n_tiles = (bsz // MIX_BATCH) * tiles_per_seq
    block = (EVEN_STEPS, MIX_BATCH, D_MODEL)

    def in_map(i):
        j = jnp.minimum(i, n_tiles - 1)
        if batch_major_in:
            return (j // tiles_per_seq, j % tiles_per_seq, 0)
        return (j % tiles_per_seq, j // tiles_per_seq, 0)

    def out_map(i):
        j = jnp.maximum(i - 1, 0)
        return (j % tiles_per_seq, j // tiles_per_seq, 0)

    in_block = (MIX_BATCH, EVEN_STEPS, D_MODEL) if batch_major_in else block
    return pl.pallas_call(
        functools.partial(_ffn_even_kernel, tiles_per_seq=tiles_per_seq,
                          batch_major_in=batch_major_in),
        out_shape=jax.ShapeDtypeStruct((seq, bsz, D_MODEL), F32),
        grid=(n_tiles + 1,),
        in_specs=[
            pl.BlockSpec(memory_space=pltpu.SMEM),
            pl.BlockSpec(in_block, in_map),
            _resident((1, D_MODEL)),
            _layer_weight(wg, layer),
            _layer_weight(wu, layer),
            _layer_weight(wd, layer),
            _resident((1, D_MODEL)),
            _resident(w_in.shape),
            _resident(bias_tbl.shape),
            _resident(conv_w.shape),
            _resident(conv_b.shape),
            _resident(ln_g.shape),
            _resident(ln_b.shape),
            _resident(w_out.shape),
        ],
        out_specs=pl.BlockSpec(block, out_map),
        scratch_shapes=[
            pltpu.VMEM(block, F32),
            pltpu.VMEM(block, F32),
            pltpu.VMEM((rows, D_MODEL), BF16),
            pltpu.VMEM((rows, D_FF), BF16),
            pltpu.VMEM((rows, D_MODEL), BF16),
            pltpu.VMEM(((Q_WIDTH + 2 * KV_WIDTH) // LANES, rows, LANES), F32),
            pltpu.VMEM((MIX_BATCH, (Q_WIDTH + 2 * KV_WIDTH) // LANES, EVEN_STEPS, LANES), F32),
            pltpu.VMEM((MIX_BATCH, 2, WINDOW, LANES), F32),
            pltpu.VMEM((MIX_BATCH, Q_WIDTH // LANES, EVEN_STEPS, LANES), F32),
            pltpu.VMEM((Q_WIDTH // LANES, rows, LANES), F32),
            pltpu.VMEM(((CONV_HALO + EVEN_STEPS) * MIX_BATCH, CONV_CHANNELS), F32),
        ],
        compiler_params=pltpu.CompilerParams(
            dimension_semantics=("arbitrary",), vmem_limit_bytes=FUSED_VMEM_LIMIT_BYTES),
        name="ffn_even",
    )(sinks, x, gain1, wg, wu, wd, gain_mix, w_in, bias_tbl, conv_w, conv_b, ln_g, ln_b, w_out)


def _final_norm_kernel(x_ref, g_ref, o_ref):
    for b in range(o_ref.shape[0]):
        o_ref[b] = _rmsnorm_rows(x_ref[:, b, :], g_ref[...])


def _final_norm_call(h, gain):
    seq, bsz, _ = h.shape
    steps = FFN_ROWS // MIX_BATCH
    return pl.pallas_call(
        _final_norm_kernel,
        out_shape=jax.ShapeDtypeStruct((bsz, seq, D_MODEL), F32),
        grid=(bsz // MIX_BATCH, seq // steps),
        in_specs=[pl.BlockSpec((steps, MIX_BATCH, D_MODEL), lambda bg, st: (st, bg, 0)),
                  _resident((1, D_MODEL))],
        out_specs=pl.BlockSpec((MIX_BATCH, steps, D_MODEL), lambda bg, st: (bg, st, 0)),
        compiler_params=pltpu.CompilerParams(dimension_semantics=("parallel", "parallel")),
        name="final_norm",
    )(h, gain)


def _t5_bucket_table(steps):
    qi = np.arange(steps)[:, None]
    sj = np.arange(WINDOW + steps)[None, :]
    dist = qi + WINDOW - sj
    n = np.maximum(dist, 0)
    max_exact = NUM_BUCKETS // 2
    nf = np.maximum(n, max_exact).astype(np.float32)
    large = max_exact + (np.log(nf / max_exact) / math.log(MAX_DISTANCE / max_exact)
                         * (NUM_BUCKETS - max_exact)).astype(np.int32)
    large = np.minimum(large, NUM_BUCKETS - 1)
    bucket = np.where(n < max_exact, n, large)
    in_window = (dist >= 0) & (dist < WINDOW)
    return bucket, in_window, sj


def _attention_bias_tables(rel_bias, steps):
    bucket, in_window, sj = _t5_bucket_table(steps)
    onehot = (jnp.asarray(bucket, jnp.int32)[..., None] == jnp.arange(NUM_BUCKETS)).astype(F32)
    bias = jnp.einsum("qsb,bh->hqs", onehot, rel_bias.astype(F32),
                      precision=lax.Precision.HIGHEST)
    tables = []
    for v in range(WINDOW // steps + 1):
        valid = in_window & (sj >= WINDOW - v * steps)
        tables.append(jnp.where(valid[None], bias, NEG_INF))
    return jnp.stack(tables)


def kernel(x, norm_ffn1, ffn1_wg, ffn1_wu, ffn1_wd, norm_mix, norm_ffn2, ffn2_wg, ffn2_wu, ffn2_wd,
           rel_bias, even_w_in, attn_sinks, conv_b_w, conv_b_b, conv_ln_g, conv_ln_b, even_w_out,
           odd_w_in, lru_conv_w, lru_conv_b, gate_a_w, gate_a_b, gate_x_w, gate_x_b, lru_lambda,
           odd_w_out, norm_final):
    bsz, seq, d = x.shape
    depth = norm_ffn1.shape[0]
    bias_tbl = _attention_bias_tables(rel_bias, EVEN_STEPS)
    row = lambda v: v.reshape(1, -1)

    bf16 = lambda w: w.astype(BF16)
    ffn1 = (bf16(ffn1_wg), bf16(ffn1_wu), bf16(ffn1_wd))
    ffn2 = (bf16(ffn2_wg), bf16(ffn2_wu), bf16(ffn2_wd))

    h = x
    for layer in range(depth):
        if layer % 2 == 0:
            e = layer // 2
            h = _ffn_even_call(h, row(norm_ffn1[layer]), *ffn1, layer, row(norm_mix[layer]),
                               even_w_in[e].astype(BF16), attn_sinks[e], bias_tbl, conv_b_w[e],
                               row(conv_b_b[e]), row(conv_ln_g[e]), row(conv_ln_b[e]),
                               even_w_out[e].astype(BF16),
                               batch_major_in=(layer == 0))
        else:
            o = layer // 2
            w_gate = jnp.concatenate([gate_a_w[o], gate_x_w[o]], axis=-1).astype(BF16)
            h = _ffn_odd_call(h, row(norm_ffn1[layer]), *ffn1, layer, row(norm_mix[layer]),
                              odd_w_in[o].astype(BF16), lru_conv_w[o], row(lru_conv_b[o]), w_gate,
                              row(gate_a_b[o]), row(gate_x_b[o]), row(lru_lambda[o]),
                              odd_w_out[o].astype(BF16))
        h = _ffn_call(h, row(norm_ffn2[layer]), *ffn2, layer)
    return _final_norm_call(h, row(norm_final))
```

```python
import functools
import math

import jax
import jax.numpy as jnp
import numpy as np
from jax import lax
from jax.experimental import pallas as pl
from jax.experimental.pallas import tpu as pltpu

D_MODEL = 1024
D_FF = 2816
HEAD_DIM = 64
N_Q_HEADS = 8
N_KV_HEADS = 2
GROUP = N_Q_HEADS // N_KV_HEADS
WINDOW = 128
Q_WIDTH = N_Q_HEADS * HEAD_DIM
KV_WIDTH = N_KV_HEADS * HEAD_DIM
NUM_BUCKETS = 32
MAX_DISTANCE = 128
CONV_CHANNELS = 512
CONV_WIDTH = 31
EVEN_IN = Q_WIDTH + 2 * KV_WIDTH + 2 * CONV_CHANNELS
LRU_WIDTH = 1024
LRU_HEADS = 8
LRU_BLOCK = 128
LRU_CONV_WIDTH = 4
RG_LRU_C = 8.0
RMS_EPS = 1e-6
LN_EPS = 1e-5
NEG_INF = -1e30

LANES = 128
SUBLANES = 8
MXU_DIM = 256
VMEM_LIMIT_BYTES = 56 * 1024 * 1024
FUSED_VMEM_LIMIT_BYTES = 60 * 1024 * 1024

FF_CHUNK = MXU_DIM
N_FF_CHUNKS = D_FF // FF_CHUNK
FFN_ROWS = 1024
ROW_CHUNK = 64
MIX_BATCH = SUBLANES
STEPS_PER_CHUNK = ROW_CHUNK // MIX_BATCH
EVEN_STEPS = 64
ODD_STEPS = 64
CONV_HALO = 32
LRU_HALO = LRU_CONV_WIDTH - 1
SCAN_ROWS = 2 * MIX_BATCH
CONV_ROWS = 64

BF16 = jnp.bfloat16
F32 = jnp.float32


def _resident(shape):
    nd = len(shape)
    return pl.BlockSpec(shape, lambda *_: (0,) * nd, pipeline_mode=pl.Buffered(1))


def _sigmoid(x):
    return 0.5 * jnp.tanh(0.5 * x) + 0.5


def _rmsnorm_rows(x, g):
    ms = jnp.mean(x * x, axis=-1, keepdims=True)
    return x * lax.rsqrt(ms + RMS_EPS) * g


def _norm_tile(h_ref, g_ref, hn_scr):
    steps = h_ref.shape[0]
    for i in range(steps // STEPS_PER_CHUNK):
        x = h_ref[i * STEPS_PER_CHUNK:(i + 1) * STEPS_PER_CHUNK].reshape(ROW_CHUNK, D_MODEL)
        hn_scr[i * ROW_CHUNK:(i + 1) * ROW_CHUNK, :] = _rmsnorm_rows(x, g_ref[...]).astype(BF16)


def _ffn_kernel(x_ref, xnext_ref, g_ref, wg_ref, wu_ref, wd_ref, o_ref, xn_scr, a_scr, *,
                batch_major_in):
    rows = xn_scr.shape[1]
    i = pl.program_id(0)
    slot = lax.rem(i, 2)

    def load_rows(src_ref, r0, n):
        if batch_major_in:
            steps = src_ref.shape[1]
            return src_ref[r0 // steps, r0 % steps:r0 % steps + n, :]
        return src_ref[r0:r0 + n, :]

    def norm_chunk(src_ref, dst_slot, k):
        x = load_rows(src_ref, k * ROW_CHUNK, ROW_CHUNK)
        xn_scr[dst_slot, k * ROW_CHUNK:(k + 1) * ROW_CHUNK, :] = (
            _rmsnorm_rows(x, g_ref[...]).astype(BF16))

    n_norm = rows // ROW_CHUNK
    norm_per_dot = pl.cdiv(n_norm, N_FF_CHUNKS)

    @pl.when(i == 0)
    def _():
        for k in range(n_norm):
            norm_chunk(x_ref, 0, k)

    for c in range(N_FF_CHUNKS):
        cols = slice(c * FF_CHUNK, (c + 1) * FF_CHUNK)
        g = jnp.dot(xn_scr[slot], wg_ref[:, cols], preferred_element_type=F32)
        u = jnp.dot(xn_scr[slot], wu_ref[:, cols], preferred_element_type=F32)
        a_scr[:, cols] = (g * _sigmoid(g) * u).astype(BF16)
        for k in range(c * norm_per_dot, min((c + 1) * norm_per_dot, n_norm)):
            norm_chunk(xnext_ref, 1 - slot, k)
    y = jnp.dot(a_scr[...], wd_ref[...], preferred_element_type=F32)
    if batch_major_in:
        steps = x_ref.shape[1]
        for b in range(x_ref.shape[0]):
            o_ref[:, b, :] = x_ref[b] + 0.5 * y[b * steps:(b + 1) * steps, :]
    else:
        o_ref[...] = x_ref[...] + 0.5 * y


def _layer_weight(w, layer):
    return pl.BlockSpec((None,) + w.shape[1:], lambda *_: (layer, 0, 0),
                        pipeline_mode=pl.Buffered(1))


def _ffn_call(x, gain, wg, wu, wd, layer, batch_major_in=False):
    if batch_major_in:
        bsz, seq, _ = x.shape
        steps = FFN_ROWS // MIX_BATCH
        n_st = seq // steps
        n_tiles = (bsz // MIX_BATCH) * n_st
        block = (MIX_BATCH, steps, D_MODEL)
        x_spec = pl.BlockSpec(block, lambda i: (i // n_st, i % n_st, 0))

        def next_map(i):
            j = jnp.minimum(i + 1, n_tiles - 1)
            return (j // n_st, j % n_st, 0)

        xnext_spec = pl.BlockSpec(block, next_map)
        out_shape = jax.ShapeDtypeStruct((seq, bsz, D_MODEL), F32)
        out_spec = pl.BlockSpec((steps, MIX_BATCH, D_MODEL), lambda i: (i % n_st, i // n_st, 0))
    else:
        seq, bsz, _ = x.shape
        x = x.reshape(seq * bsz, D_MODEL)
        n_tiles = x.shape[0] // FFN_ROWS
        x_spec = pl.BlockSpec((FFN_ROWS, D_MODEL), lambda i: (i, 0))
        xnext_spec = pl.BlockSpec((FFN_ROWS, D_MODEL),
                                  lambda i: (jnp.minimum(i + 1, n_tiles - 1), 0))
        out_shape = jax.ShapeDtypeStruct(x.shape, F32)
        out_spec = pl.BlockSpec((FFN_ROWS, D_MODEL), lambda i: (i, 0))
    out = pl.pallas_call(
        functools.partial(_ffn_kernel, batch_major_in=batch_major_in),
        out_shape=out_shape,
        grid=(n_tiles,),
        in_specs=[
            x_spec,
            xnext_spec,
            _resident((1, D_MODEL)),
            _layer_weight(wg, layer),
            _layer_weight(wu, layer),
            _layer_weight(wd, layer),
        ],
        out_specs=out_spec,
        scratch_shapes=[
            pltpu.VMEM((2, FFN_ROWS, D_MODEL), BF16),
            pltpu.VMEM((FFN_ROWS, D_FF), BF16),
        ],
        compiler_params=pltpu.CompilerParams(
            dimension_semantics=("arbitrary",), vmem_limit_bytes=VMEM_LIMIT_BYTES),
        name="ffn",
    )(x, x, gain, wg, wu, wd)
    return out.reshape(seq, bsz, D_MODEL)


def _ffn_odd_kernel(x_ref, g1_ref, wg_ref, wu_ref, wd_ref, gm_ref, win_ref, cw_ref, cb_ref,
                    wgate_ref, gab_ref, gxb_ref, lam_ref, wout_ref, o_ref,
                    h1_new, h1_prev, xn_scr, a_scr, hn_scr, ug_scr, xpad_scr, rec_scr, gpre_scr,
                    state_scr, *, tiles_per_seq):
    steps = x_ref.shape[0]
    rows = steps * MIX_BATCH
    halo_rows = LRU_HALO * MIX_BATCH
    i = pl.program_id(0)

    @pl.when(i == 0)
    def _():
        h1_prev[...] = jnp.zeros(h1_prev.shape, F32)

    @pl.when(jnp.logical_or(i == 0, lax.rem(i + tiles_per_seq - 1, tiles_per_seq) == 0))
    def _():
        xpad_scr[0:halo_rows, :] = jnp.zeros((halo_rows, LRU_WIDTH), F32)
        state_scr[...] = jnp.zeros((MIX_BATCH, LRU_WIDTH), F32)

    def ffn_up(c0, c1):
        for c in range(c0, c1):
            cols = slice(c * FF_CHUNK, (c + 1) * FF_CHUNK)
            g = jnp.dot(xn_scr[...], wg_ref[:, cols], preferred_element_type=F32)
            u = jnp.dot(xn_scr[...], wu_ref[:, cols], preferred_element_type=F32)
            a_scr[:, cols] = (g * _sigmoid(g) * u).astype(BF16)

    for k in range(rows // ROW_CHUNK):
        x = x_ref[k * STEPS_PER_CHUNK:(k + 1) * STEPS_PER_CHUNK].reshape(ROW_CHUNK, D_MODEL)
        xn_scr[k * ROW_CHUNK:(k + 1) * ROW_CHUNK, :] = _rmsnorm_rows(x, g1_ref[...]).astype(BF16)
    for k in range(rows // ROW_CHUNK):
        r = slice(k * ROW_CHUNK, (k + 1) * ROW_CHUNK)
        hn_scr[r, :] = _rmsnorm_rows(h1_prev[r, :], gm_ref[...]).astype(BF16)
    ffn_up(0, 2)
    ug_scr[...] = jnp.dot(hn_scr[...], win_ref[:, 0:LRU_WIDTH], preferred_element_type=F32)
    xpad_scr[halo_rows:, :] = jnp.dot(hn_scr[...], win_ref[:, LRU_WIDTH:],
                                      preferred_element_type=F32)
    ffn_up(2, 5)

    for k in range(rows // ROW_CHUNK):
        acc = jnp.zeros((ROW_CHUNK, LRU_WIDTH), F32)
        for tap in range(LRU_CONV_WIDTH):
            r0 = (k * STEPS_PER_CHUNK + tap) * MIX_BATCH
            acc = acc + xpad_scr[r0:r0 + ROW_CHUNK, :] * cw_ref[tap:tap + 1, :]
        rec_scr[k * ROW_CHUNK:(k + 1) * ROW_CHUNK, :] = acc + cb_ref[...]

    for hd in range(LRU_HEADS):
        cs = slice(hd * LRU_BLOCK, (hd + 1) * LRU_BLOCK)
        pre = jnp.dot(rec_scr[:, cs].astype(BF16), wgate_ref[hd], preferred_element_type=F32)
        gpre_scr[:, cs] = pre[:, :LRU_BLOCK]
        gpre_scr[:, LRU_WIDTH + hd * LRU_BLOCK:LRU_WIDTH + (hd + 1) * LRU_BLOCK] = pre[:, LRU_BLOCK:]

    ffn_up(5, N_FF_CHUNKS)
    y = jnp.dot(a_scr[...], wd_ref[...], preferred_element_type=F32)
    h1_new[...] = x_ref[...].reshape(rows, D_MODEL) + 0.5 * y

    lam = lam_ref[...]
    log_sig_lam = -(jnp.maximum(-lam, 0.0) + jnp.log1p(jnp.exp(-jnp.abs(lam))))

    hprev = state_scr[...]
    for k in range(rows // SCAN_ROWS):
        r = slice(k * SCAN_ROWS, (k + 1) * SCAN_ROWS)
        x = rec_scr[r, :]
        rg = _sigmoid(gpre_scr[r, 0:LRU_WIDTH] + gab_ref[...])
        ig = _sigmoid(gpre_scr[r, LRU_WIDTH:] + gxb_ref[...])
        log_a = RG_LRU_C * rg * log_sig_lam
        a = jnp.exp(log_a)
        bx = jnp.sqrt(-jnp.tanh(log_a) * (a * a + 1.0)) * (ig * x)
        hs = []
        for t in range(SCAN_ROWS // MIX_BATCH):
            rt = slice(t * MIX_BATCH, (t + 1) * MIX_BATCH)
            hprev = a[rt, :] * hprev + bx[rt, :]
            hs.append(hprev)
        hs = jnp.concatenate(hs, axis=0)
        hn_scr[r, :] = (jax.nn.gelu(ug_scr[r, :]) * hs).astype(BF16)
    state_scr[...] = hprev

    mixed = jnp.dot(hn_scr[...], wout_ref[...], preferred_element_type=F32)
    o_ref[...] = (h1_prev[...] + mixed).reshape(steps, MIX_BATCH, D_MODEL)
    xpad_scr[0:halo_rows, :] = xpad_scr[rows:rows + halo_rows, :]

    h1_prev[...] = h1_new[...]


def _ffn_odd_call(h, gain1, wg, wu, wd, layer, gain_mix, w_in, conv_w, conv_b, w_gate, ga_b, gx_b,
                  lam, w_out):
    seq, bsz, _ = h.shape
    rows = ODD_STEPS * MIX_BATCH
    tiles_per_seq = seq // ODD_STEPS
    n_tiles = (bsz // MIX_BATCH) * tiles_per_seq
    block = (ODD_STEPS, MIX_BATCH, D_MODEL)

    def in_map(i):
        j = jnp.minimum(i, n_tiles - 1)
        return (j % tiles_per_seq, j // tiles_per_seq, 0)

    def out_map(i):
        j = jnp.maximum(i - 1, 0)
        return (j % tiles_per_seq, j // tiles_per_seq, 0)

    return pl.pallas_call(
        functools.partial(_ffn_odd_kernel, tiles_per_seq=tiles_per_seq),
        out_shape=jax.ShapeDtypeStruct(h.shape, F32),
        grid=(n_tiles + 1,),
        in_specs=[
            pl.BlockSpec(block, in_map),
            _resident((1, D_MODEL)),
            _layer_weight(wg, layer),
            _layer_weight(wu, layer),
            _layer_weight(wd, layer),
            _resident((1, D_MODEL)),
            _resident(w_in.shape),
            _resident(conv_w.shape),
            _resident(conv_b.shape),
            _resident(w_gate.shape),
            _resident(ga_b.shape),
            _resident(gx_b.shape),
            _resident(lam.shape),
            _resident(w_out.shape),
        ],
        out_specs=pl.BlockSpec(block, out_map),
        scratch_shapes=[
            pltpu.VMEM((rows, D_MODEL), F32),
            pltpu.VMEM((rows, D_MODEL), F32),
            pltpu.VMEM((rows, D_MODEL), BF16),
            pltpu.VMEM((rows, D_FF), BF16),
            pltpu.VMEM((rows, D_MODEL), BF16),
            pltpu.VMEM((rows, LRU_WIDTH), F32),
            pltpu.VMEM(((LRU_HALO + ODD_STEPS) * MIX_BATCH, LRU_WIDTH), F32),
            pltpu.VMEM((rows, LRU_WIDTH), F32),
            pltpu.VMEM((rows, 2 * LRU_WIDTH), F32),
            pltpu.VMEM((MIX_BATCH, LRU_WIDTH), F32),
        ],
        compiler_params=pltpu.CompilerParams(
            dimension_semantics=("arbitrary",), vmem_limit_bytes=FUSED_VMEM_LIMIT_BYTES),
        name="ffn_odd",
    )(h, gain1, wg, wu, wd, gain_mix, w_in, conv_w, conv_b, w_gate, ga_b, gx_b, lam, w_out)


def _ffn_even_kernel(sinks_ref, x_ref, g1_ref, wg_ref, wu_ref, wd_ref, gm_ref, win_ref, bias_ref,
                     cw_ref, cb_ref, lng_ref, lnb_ref, wout_ref, o_ref,
                     h1_new, h1_prev, xn_scr, a_scr, hn_scr, qkv_scr, qseq_scr, kvprev_scr, oseq_scr,
                     attn_scr, xpad_scr, *, tiles_per_seq, batch_major_in):
    steps = o_ref.shape[0]
    rows = steps * MIX_BATCH
    halo_rows = CONV_HALO * MIX_BATCH
    n_keys = WINDOW + steps
    i = pl.program_id(0)
    seq_tile = lax.rem(i + tiles_per_seq - 1, tiles_per_seq)

    @pl.when(i == 0)
    def _():
        h1_prev[...] = jnp.zeros(h1_prev.shape, F32)

    @pl.when(jnp.logical_or(i == 0, seq_tile == 0))
    def _():
        kvprev_scr[...] = jnp.zeros(kvprev_scr.shape, F32)
        xpad_scr[0:halo_rows, :] = jnp.zeros((halo_rows, CONV_CHANNELS), F32)

    def ffn_norm():
        for k in range(rows // ROW_CHUNK):
            if batch_major_in:
                r0 = k * ROW_CHUNK
                x = x_ref[r0 // steps, r0 % steps:r0 % steps + ROW_CHUNK, :]
            else:
                x = x_ref[k * STEPS_PER_CHUNK:(k + 1) * STEPS_PER_CHUNK].reshape(ROW_CHUNK,
                                                                                 D_MODEL)
            xn_scr[k * ROW_CHUNK:(k + 1) * ROW_CHUNK, :] = (
                _rmsnorm_rows(x, g1_ref[...]).astype(BF16))

    def ffn_up(c0, c1):
        for c in range(c0, c1):
            cols = slice(c * FF_CHUNK, (c + 1) * FF_CHUNK)
            g = jnp.dot(xn_scr[...], wg_ref[:, cols], preferred_element_type=F32)
            u = jnp.dot(xn_scr[...], wu_ref[:, cols], preferred_element_type=F32)
            a_scr[:, cols] = (g * _sigmoid(g) * u).astype(BF16)

    def ffn_down():
        y = jnp.dot(a_scr[...], wd_ref[...], preferred_element_type=F32)
        if batch_major_in:
            for b in range(MIX_BATCH):
                h1_new[:, b, :] = x_ref[b] + 0.5 * y[b * steps:(b + 1) * steps, :]
        else:
            h1_new[...] = x_ref[...] + 0.5 * y.reshape(steps, MIX_BATCH, D_MODEL)

    ffn_norm()
    _norm_tile(h1_prev, gm_ref, hn_scr)
    ffn_up(0, 2)

    for j in range((Q_WIDTH + 2 * KV_WIDTH) // MXU_DIM):
        res = jnp.dot(hn_scr[...], win_ref[:, j * MXU_DIM:(j + 1) * MXU_DIM],
                      preferred_element_type=F32)
        qkv_scr[2 * j] = res[:, :LANES]
        qkv_scr[2 * j + 1] = res[:, LANES:]
    o_g = Q_WIDTH + 2 * KV_WIDTH
    glu_a = jnp.dot(hn_scr[...], win_ref[:, o_g:o_g + CONV_CHANNELS], preferred_element_type=F32)
    glu_b = jnp.dot(hn_scr[...], win_ref[:, o_g + CONV_CHANNELS:], preferred_element_type=F32)
    xpad_scr[halo_rows:, :] = glu_a * _sigmoid(glu_b)

    k_slab = Q_WIDTH // LANES
    v_slab = k_slab + 1
    lane = lax.broadcasted_iota(jnp.int32, (n_keys, LANES), 1)
    qlane = lax.broadcasted_iota(jnp.int32, (steps, LANES), 1)
    scale = 1.0 / math.sqrt(HEAD_DIM)
    variant = jnp.minimum(seq_tile, WINDOW // steps)
    for b in range(MIX_BATCH):
        for j in range(qkv_scr.shape[0]):
            qseq_scr[b, j] = qkv_scr[j, pl.ds(b, steps, stride=MIX_BATCH), :]
    score_blocks = []
    for b in range(MIX_BATCH):
        k_cat = jnp.concatenate([kvprev_scr[b, 0], qseq_scr[b, k_slab]], axis=0)
        v_cat = jnp.concatenate([kvprev_scr[b, 1], qseq_scr[b, v_slab]], axis=0)
        kvprev_scr[b, 0] = k_cat[steps:, :]
        kvprev_scr[b, 1] = v_cat[steps:, :]
        k_sw = pltpu.roll(k_cat, HEAD_DIM, axis=1)
        v_sw = pltpu.roll(v_cat, HEAD_DIM, axis=1)
        k_dup = (jnp.where(lane < HEAD_DIM, k_cat, k_sw).astype(BF16),
                 jnp.where(lane < HEAD_DIM, k_sw, k_cat).astype(BF16))
        v_dup = (jnp.where(lane < HEAD_DIM, v_cat, v_sw).astype(BF16),
                 jnp.where(lane < HEAD_DIM, v_sw, v_cat).astype(BF16))
        for c in range(Q_WIDTH // LANES):
            kvh = (2 * c) // GROUP
            qcol = qseq_scr[b, c] * scale
            for half in range(2):
                keep = (qlane < HEAD_DIM) if half == 0 else (qlane >= HEAD_DIM)
                q = jnp.where(keep, qcol, 0.0).astype(BF16)
                sc = lax.dot_general(q, k_dup[kvh], (((1,), (1,)), ((), ())),
                                     preferred_element_type=F32)
                score_blocks.append((2 * c + half, sc, v_dup[kvh]))
    ffn_up(2, 9)
    prob_blocks = []
    for head, sc, v in score_blocks:
        sc = sc + bias_ref[variant, head]
        sink = sinks_ref[head]
        m = jnp.maximum(jnp.max(sc, axis=-1, keepdims=True), sink)
        e = jnp.exp(sc - m)
        den = jnp.sum(e, axis=-1, keepdims=True) + jnp.exp(sink - m)
        prob_blocks.append(((e / den).astype(BF16), v))
    outs = [jnp.dot(p, v, preferred_element_type=F32) for p, v in prob_blocks]
    ffn_up(9, N_FF_CHUNKS)
    ffn_down()
    for b in range(MIX_BATCH):
        for c in range(Q_WIDTH // LANES):
            n = (b * (Q_WIDTH // LANES) + c) * 2
            oseq_scr[b, c] = jnp.where(qlane < HEAD_DIM, outs[n], outs[n + 1])
    for b in range(MIX_BATCH):
        for c in range(Q_WIDTH // LANES):
            attn_scr[c, pl.ds(b, steps, stride=MIX_BATCH), :] = oseq_scr[b, c]
    for c in range(Q_WIDTH // LANES):
        hn_scr[:, c * LANES:(c + 1) * LANES] = attn_scr[c].astype(BF16)

    first_step = CONV_HALO - (CONV_WIDTH - 1)
    for k in range(rows // ROW_CHUNK):
        acc = jnp.zeros((ROW_CHUNK, CONV_CHANNELS), F32)
        for tap in range(CONV_WIDTH):
            r0 = (k * STEPS_PER_CHUNK + first_step + tap) * MIX_BATCH
            acc = acc + xpad_scr[r0:r0 + ROW_CHUNK, :] * cw_ref[tap:tap + 1, :]
        y = acc + cb_ref[...]
        mu = jnp.mean(y, axis=-1, keepdims=True)
        yc = y - mu
        var = jnp.mean(yc * yc, axis=-1, keepdims=True)
        z = yc * lax.rsqrt(var + LN_EPS) * lng_ref[...] + lnb_ref[...]
        hn_scr[k * ROW_CHUNK:(k + 1) * ROW_CHUNK, Q_WIDTH:] = (z * _sigmoid(z)).astype(BF16)

    mixed = jnp.dot(hn_scr[...], wout_ref[...], preferred_element_type=F32)
    o_ref[...] = h1_prev[...] + mixed.reshape(steps, MIX_BATCH, D_MODEL)
    xpad_scr[0:halo_rows, :] = xpad_scr[rows:rows + halo_rows, :]

    h1_prev[...] = h1_new[...]


def _ffn_even_call(x, gain1, wg, wu, wd, layer, gain_mix, w_in, sinks, bias_tbl, conv_w, conv_b,
                   ln_g, ln_b, w_out, batch_major_in=False):
    if batch_major_in:
        bsz, seq, _ = x.shape
    else:
        seq, bsz, _ = x.shape
    rows = EVEN_STEPS * MIX_BATCH
    tiles_per_seq = seq // EVEN_STEPS
    n_tiles = (bsz // MIX_BATCH) * tiles_per_seq
    block = (EVEN_STEPS, MIX_BATCH, D_MODEL)

    def in_map(i):
        j = jnp.minimum(i, n_tiles - 1)
        if batch_major_in:
            return (j // tiles_per_seq, j % tiles_per_seq, 0)
        return (j % tiles_per_seq, j // tiles_per_seq, 0)

    def out_map(i):
        j = jnp.maximum(i - 1, 0)
        return (j % tiles_per_seq, j // tiles_per_seq, 0)

    in_block = (MIX_BATCH, EVEN_STEPS, D_MODEL) if batch_major_in else block
    return pl.pallas_call(
        functools.partial(_ffn_even_kernel, tiles_per_seq=tiles_per_seq,
                          batch_major_in=batch_major_in),
        out_shape=jax.ShapeDtypeStruct((seq, bsz, D_MODEL), F32),
        grid=(n_tiles + 1,),
        in_specs=[
            pl.BlockSpec(memory_space=pltpu.SMEM),
            pl.BlockSpec(in_block, in_map),
            _resident((1, D_MODEL)),
            _layer_weight(wg, layer),
            _layer_weight(wu, layer),
            _layer_weight(wd, layer),
            _resident((1, D_MODEL)),
            _resident(w_in.shape),
            _resident(bias_tbl.shape),
            _resident(conv_w.shape),
            _resident(conv_b.shape),
            _resident(ln_g.shape),
            _resident(ln_b.shape),
            _resident(w_out.shape),
        ],
        out_specs=pl.BlockSpec(block, out_map),
        scratch_shapes=[
            pltpu.VMEM(block, F32),
            pltpu.VMEM(block, F32),
            pltpu.VMEM((rows, D_MODEL), BF16),
            pltpu.VMEM((rows, D_FF), BF16),
            pltpu.VMEM((rows, D_MODEL), BF16),
            pltpu.VMEM(((Q_WIDTH + 2 * KV_WIDTH) // LANES, rows, LANES), F32),
            pltpu.VMEM((MIX_BATCH, (Q_WIDTH + 2 * KV_WIDTH) // LANES, EVEN_STEPS, LANES), F32),
            pltpu.VMEM((MIX_BATCH, 2, WINDOW, LANES), F32),
            pltpu.VMEM((MIX_BATCH, Q_WIDTH // LANES, EVEN_STEPS, LANES), F32),
            pltpu.VMEM((Q_WIDTH // LANES, rows, LANES), F32),
            pltpu.VMEM(((CONV_HALO + EVEN_STEPS) * MIX_BATCH, CONV_CHANNELS), F32),
        ],
        compiler_params=pltpu.CompilerParams(
            dimension_semantics=("arbitrary",), vmem_limit_bytes=FUSED_VMEM_LIMIT_BYTES),
        name="ffn_even",
    )(sinks, x, gain1, wg, wu, wd, gain_mix, w_in, bias_tbl, conv_w, conv_b, ln_g, ln_b, w_out)


def _final_norm_kernel(x_ref, g_ref, o_ref):
    for b in range(o_ref.shape[0]):
        o_ref[b] = _rmsnorm_rows(x_ref[:, b, :], g_ref[...])


def _final_norm_call(h, gain):
    seq, bsz, _ = h.shape
    steps = FFN_ROWS // MIX_BATCH
    return pl.pallas_call(
        _final_norm_kernel,
        out_shape=jax.ShapeDtypeStruct((bsz, seq, D_MODEL), F32),
        grid=(bsz // MIX_BATCH, seq // steps),
        in_specs=[pl.BlockSpec((steps, MIX_BATCH, D_MODEL), lambda bg, st: (st, bg, 0)),
                  _resident((1, D_MODEL))],
        out_specs=pl.BlockSpec((MIX_BATCH, steps, D_MODEL), lambda bg, st: (bg, st, 0)),
        compiler_params=pltpu.CompilerParams(dimension_semantics=("parallel", "parallel")),
        name="final_norm",
    )(h, gain)


def _t5_bucket_table(steps):
    qi = np.arange(steps)[:, None]
    sj = np.arange(WINDOW + steps)[None, :]
    dist = qi + WINDOW - sj
    n = np.maximum(dist, 0)
    max_exact = NUM_BUCKETS // 2
    nf = np.maximum(n, max_exact).astype(np.float32)
    large = max_exact + (np.log(nf / max_exact) / math.log(MAX_DISTANCE / max_exact)
                         * (NUM_BUCKETS - max_exact)).astype(np.int32)
    large = np.minimum(large, NUM_BUCKETS - 1)
    bucket = np.where(n < max_exact, n, large)
    in_window = (dist >= 0) & (dist < WINDOW)
    return bucket, in_window, sj


def _attention_bias_tables(rel_bias, steps):
    bucket, in_window, sj = _t5_bucket_table(steps)
    onehot = (jnp.asarray(bucket, jnp.int32)[..., None] == jnp.arange(NUM_BUCKETS)).astype(F32)
    bias = jnp.einsum("qsb,bh->hqs", onehot, rel_bias.astype(F32),
                      precision=lax.Precision.HIGHEST)
    tables = []
    for v in range(WINDOW // steps + 1):
        valid = in_window & (sj >= WINDOW - v * steps)
        tables.append(jnp.where(valid[None], bias, NEG_INF))
    return jnp.stack(tables)


def kernel(x, norm_ffn1, ffn1_wg, ffn1_wu, ffn1_wd, norm_mix, norm_ffn2, ffn2_wg, ffn2_wu, ffn2_wd,
           rel_bias, even_w_in, attn_sinks, conv_b_w, conv_b_b, conv_ln_g, conv_ln_b, even_w_out,
           odd_w_in, lru_conv_w, lru_conv_b, gate_a_w, gate_a_b, gate_x_w, gate_x_b, lru_lambda,
           odd_w_out, norm_final):
    bsz, seq, d = x.shape
    depth = norm_ffn1.shape[0]
    bias_tbl = _attention_bias_tables(rel_bias, EVEN_STEPS)
    row = lambda v: v.reshape(1, -1)

    bf16 = lambda w: w.astype(BF16)
    ffn1 = (bf16(ffn1_wg), bf16(ffn1_wu), bf16(ffn1_wd))
    ffn2 = (bf16(ffn2_wg), bf16(ffn2_wu), bf16(ffn2_wd))

    h = x
    for layer in range(depth):
        if layer % 2 == 0:
            e = layer // 2
            h = _ffn_even_call(h, row(norm_ffn1[layer]), *ffn1, layer, row(norm_mix[layer]),
                               even_w_in[e].astype(BF16), attn_sinks[e], bias_tbl, conv_b_w[e],
                               row(conv_b_b[e]), row(conv_ln_g[e]), row(conv_ln_b[e]),
                               even_w_out[e].astype(BF16),
                               batch_major_in=(layer == 0))
        else:
            o = layer // 2
            w_gate = jnp.concatenate([gate_a_w[o], gate_x_w[o]], axis=-1).astype(BF16)
            h = _ffn_odd_call(h, row(norm_ffn1[layer]), *ffn1, layer, row(norm_mix[layer]),
                              odd_w_in[o].astype(BF16), lru_conv_w[o], row(lru_conv_b[o]), w_gate,
                              row(gate_a_b[o]), row(gate_x_b[o]), row(lru_lambda[o]),
                              odd_w_out[o].astype(BF16))
        h = _ffn_call(h, row(norm_ffn2[layer]), *ffn2, layer)
    return _final_norm_call(h, row(norm_final))
```

```python
import functools
import math

import jax
import jax.numpy as jnp
import numpy as np
from jax import lax
from jax.experimental import pallas as pl
from jax.experimental.pallas import tpu as pltpu

D_MODEL = 1024
D_FF = 2816
HEAD_DIM = 64
N_Q_HEADS = 8
N_KV_HEADS = 2
GROUP = N_Q_HEADS // N_KV_HEADS
WINDOW = 128
Q_WIDTH = N_Q_HEADS * HEAD_DIM
KV_WIDTH = N_KV_HEADS * HEAD_DIM
NUM_BUCKETS = 32
MAX_DISTANCE = 128
CONV_CHANNELS = 512
CONV_WIDTH = 31
EVEN_IN = Q_WIDTH + 2 * KV_WIDTH + 2 * CONV_CHANNELS
LRU_WIDTH = 1024
LRU_HEADS = 8
LRU_BLOCK = 128
LRU_CONV_WIDTH = 4
RG_LRU_C = 8.0
RMS_EPS = 1e-6
LN_EPS = 1e-5
NEG_INF = -1e30

LANES = 128
SUBLANES = 8
MXU_DIM = 256
VMEM_LIMIT_BYTES = 56 * 1024 * 1024
FUSED_VMEM_LIMIT_BYTES = 60 * 1024 * 1024

FF_CHUNK = MXU_DIM
N_FF_CHUNKS = D_FF // FF_CHUNK
FFN_ROWS = 1024
ROW_CHUNK = 64
MIX_BATCH = SUBLANES
STEPS_PER_CHUNK = ROW_CHUNK // MIX_BATCH
EVEN_STEPS = 64
ODD_STEPS = 64
FINAL_STEPS = 64
CONV_HALO = 32
LRU_HALO = LRU_CONV_WIDTH - 1
SCAN_ROWS = 2 * MIX_BATCH
CONV_ROWS = 64

BF16 = jnp.bfloat16
F32 = jnp.float32


def _resident(shape):
    nd = len(shape)
    return pl.BlockSpec(shape, lambda *_: (0,) * nd, pipeline_mode=pl.Buffered(1))


def _sigmoid(x):
    return 0.5 * jnp.tanh(0.5 * x) + 0.5


def _silu_mul(g, u):
    half = 0.5 * g
    return (half + half * jnp.tanh(half)) * u


def _rmsnorm_rows(x, g):
    ms = jnp.mean(x * x, axis=-1, keepdims=True)
    return x * lax.rsqrt(ms + RMS_EPS) * g


def _norm_tile(h_ref, g_ref, hn_scr):
    steps = h_ref.shape[0]
    for i in range(steps // STEPS_PER_CHUNK):
        x = h_ref[i * STEPS_PER_CHUNK:(i + 1) * STEPS_PER_CHUNK].reshape(ROW_CHUNK, D_MODEL)
        hn_scr[i * ROW_CHUNK:(i + 1) * ROW_CHUNK, :] = _rmsnorm_rows(x, g_ref[...]).astype(BF16)


def _ffn_kernel(*refs, final_norm):
    if final_norm:
        (x_ref, xnext_ref, g_ref, wg_ref, wu_ref, wd_ref, gfin_ref, o_ref,
         xn_scr, a_scr, res_scr) = refs
    else:
        x_ref, xnext_ref, g_ref, wg_ref, wu_ref, wd_ref, o_ref, xn_scr, a_scr = refs
    rows = xn_scr.shape[1]
    i = pl.program_id(0)
    slot = lax.rem(i, 2)

    def norm_chunk(src_ref, dst_slot, k):
        if final_norm:
            x = src_ref[k * STEPS_PER_CHUNK:(k + 1) * STEPS_PER_CHUNK].reshape(ROW_CHUNK, D_MODEL)
        else:
            x = src_ref[k * ROW_CHUNK:(k + 1) * ROW_CHUNK, :]
        xn_scr[dst_slot, k * ROW_CHUNK:(k + 1) * ROW_CHUNK, :] = (
            _rmsnorm_rows(x, g_ref[...]).astype(BF16))

    n_norm = rows // ROW_CHUNK
    norm_per_dot = pl.cdiv(n_norm, N_FF_CHUNKS)

    @pl.when(i == 0)
    def _():
        for k in range(n_norm):
            norm_chunk(x_ref, 0, k)

    for c in range(N_FF_CHUNKS):
        cols = slice(c * FF_CHUNK, (c + 1) * FF_CHUNK)
        g = jnp.dot(xn_scr[slot], wg_ref[:, cols], preferred_element_type=F32)
        u = jnp.dot(xn_scr[slot], wu_ref[:, cols], preferred_element_type=F32)
        a_scr[:, cols] = _silu_mul(g, u).astype(BF16)
        for k in range(c * norm_per_dot, min((c + 1) * norm_per_dot, n_norm)):
            norm_chunk(xnext_ref, 1 - slot, k)
    y = jnp.dot(a_scr[...], wd_ref[...], preferred_element_type=F32)
    if final_norm:
        res_scr[...] = x_ref[...] + 0.5 * y.reshape(res_scr.shape)
        for b in range(o_ref.shape[0]):
            o_ref[b] = _rmsnorm_rows(res_scr[:, b, :], gfin_ref[...])
    else:
        o_ref[...] = x_ref[...] + 0.5 * y


def _layer_weight(w, layer):
    return pl.BlockSpec((None,) + w.shape[1:], lambda *_: (layer, 0, 0),
                        pipeline_mode=pl.Buffered(1))


def _ffn_call(x, gain, wg, wu, wd, layer, final_gain=None):
    seq, bsz, _ = x.shape
    final_norm = final_gain is not None
    if final_norm:
        tile_rows = FINAL_STEPS * MIX_BATCH
        n_st = seq // FINAL_STEPS
        n_tiles = (bsz // MIX_BATCH) * n_st
        block = (FINAL_STEPS, MIX_BATCH, D_MODEL)
        x_spec = pl.BlockSpec(block, lambda i: (i % n_st, i // n_st, 0))

        def next_map(i):
            j = jnp.minimum(i + 1, n_tiles - 1)
            return (j % n_st, j // n_st, 0)

        xnext_spec = pl.BlockSpec(block, next_map)
        out_shape = jax.ShapeDtypeStruct((bsz, seq, D_MODEL), F32)
        out_spec = pl.BlockSpec((MIX_BATCH, FINAL_STEPS, D_MODEL),
                                lambda i: (i // n_st, i % n_st, 0))
        operands = (x, x, gain, wg, wu, wd, final_gain)
        extra_specs = [_resident((1, D_MODEL))]
        extra_scratch = [pltpu.VMEM(block, F32)]
    else:
        tile_rows = FFN_ROWS
        x = x.reshape(seq * bsz, D_MODEL)
        n_tiles = x.shape[0] // FFN_ROWS
        x_spec = pl.BlockSpec((FFN_ROWS, D_MODEL), lambda i: (i, 0))
        xnext_spec = pl.BlockSpec((FFN_ROWS, D_MODEL),
                                  lambda i: (jnp.minimum(i + 1, n_tiles - 1), 0))
        out_shape = jax.ShapeDtypeStruct(x.shape, F32)
        out_spec = pl.BlockSpec((FFN_ROWS, D_MODEL), lambda i: (i, 0))
        operands = (x, x, gain, wg, wu, wd)
        extra_specs = []
        extra_scratch = []
    out = pl.pallas_call(
        functools.partial(_ffn_kernel, final_norm=final_norm),
        out_shape=out_shape,
        grid=(n_tiles,),
        in_specs=[
            x_spec,
            xnext_spec,
            _resident((1, D_MODEL)),
            _layer_weight(wg, layer),
            _layer_weight(wu, layer),
            _layer_weight(wd, layer),
        ] + extra_specs,
        out_specs=out_spec,
        scratch_shapes=[
            pltpu.VMEM((2, tile_rows, D_MODEL), BF16),
            pltpu.VMEM((tile_rows, D_FF), BF16),
        ] + extra_scratch,
        compiler_params=pltpu.CompilerParams(
            dimension_semantics=("arbitrary",), vmem_limit_bytes=VMEM_LIMIT_BYTES),
        name="ffn_final" if final_norm else "ffn",
    )(*operands)
    return out if final_norm else out.reshape(seq, bsz, D_MODEL)


def _ffn_odd_kernel(x_ref, xnext_ref, g1_ref, wg_ref, wu_ref, wd_ref, gm_ref, win_ref, cw_ref,
                    cb_ref, wgate_ref, gab_ref, gxb_ref, lam_ref, wout_ref, o_ref,
                    h1_new, h1_prev, xn_scr, a_scr, hn_scr, ug_scr, xpad_scr, rec_scr, gpre_scr,
                    state_scr, *, tiles_per_seq):
    steps = x_ref.shape[0]
    rows = steps * MIX_BATCH
    halo_rows = LRU_HALO * MIX_BATCH
    i = pl.program_id(0)
    slot = lax.rem(i, 2)

    def ffn_norm_chunk(src_ref, dst_slot, k):
        x = src_ref[k * STEPS_PER_CHUNK:(k + 1) * STEPS_PER_CHUNK].reshape(ROW_CHUNK, D_MODEL)
        xn_scr[dst_slot, k * ROW_CHUNK:(k + 1) * ROW_CHUNK, :] = (
            _rmsnorm_rows(x, g1_ref[...]).astype(BF16))

    @pl.when(i == 0)
    def _():
        h1_prev[...] = jnp.zeros(h1_prev.shape, F32)
        for k in range(rows // ROW_CHUNK):
            ffn_norm_chunk(x_ref, 0, k)

    @pl.when(jnp.logical_or(i == 0, lax.rem(i + tiles_per_seq - 1, tiles_per_seq) == 0))
    def _():
        xpad_scr[0:halo_rows, :] = jnp.zeros((halo_rows, LRU_WIDTH), F32)
        state_scr[...] = jnp.zeros((MIX_BATCH, LRU_WIDTH), F32)

    def ffn_up(c0, c1):
        for c in range(c0, c1):
            cols = slice(c * FF_CHUNK, (c + 1) * FF_CHUNK)
            g = jnp.dot(xn_scr[slot], wg_ref[:, cols], preferred_element_type=F32)
            u = jnp.dot(xn_scr[slot], wu_ref[:, cols], preferred_element_type=F32)
            a_scr[:, cols] = _silu_mul(g, u).astype(BF16)
            if c < rows // ROW_CHUNK:
                ffn_norm_chunk(xnext_ref, 1 - slot, c)

    for k in range(rows // ROW_CHUNK):
        r = slice(k * ROW_CHUNK, (k + 1) * ROW_CHUNK)
        hn_scr[r, :] = _rmsnorm_rows(h1_prev[r, :], gm_ref[...]).astype(BF16)
    ffn_up(0, 2)
    ug_scr[...] = jnp.dot(hn_scr[...], win_ref[:, 0:LRU_WIDTH], preferred_element_type=F32)
    xpad_scr[halo_rows:, :] = jnp.dot(hn_scr[...], win_ref[:, LRU_WIDTH:],
                                      preferred_element_type=F32)
    ffn_up(2, 7)

    for k in range(rows // ROW_CHUNK):
        acc = jnp.zeros((ROW_CHUNK, LRU_WIDTH), F32)
        for tap in range(LRU_CONV_WIDTH):
            r0 = (k * STEPS_PER_CHUNK + tap) * MIX_BATCH
            acc = acc + xpad_scr[r0:r0 + ROW_CHUNK, :] * cw_ref[tap:tap + 1, :]
        rec_scr[k * ROW_CHUNK:(k + 1) * ROW_CHUNK, :] = acc + cb_ref[...]

    for hd in range(LRU_HEADS):
        cs = slice(hd * LRU_BLOCK, (hd + 1) * LRU_BLOCK)
        pre = jnp.dot(rec_scr[:, cs].astype(BF16), wgate_ref[hd], preferred_element_type=F32)
        gpre_scr[:, cs] = pre[:, :LRU_BLOCK]
        gpre_scr[:, LRU_WIDTH + hd * LRU_BLOCK:LRU_WIDTH + (hd + 1) * LRU_BLOCK] = pre[:, LRU_BLOCK:]

    ffn_up(7, N_FF_CHUNKS)
    y = jnp.dot(a_scr[...], wd_ref[...], preferred_element_type=F32)
    h1_new[...] = x_ref[...].reshape(rows, D_MODEL) + 0.5 * y

    lam = lam_ref[...]
    log_sig_lam = -(jnp.maximum(-lam, 0.0) + jnp.log1p(jnp.exp(-jnp.abs(lam))))

    hprev = state_scr[...]
    for k in range(rows // SCAN_ROWS):
        r = slice(k * SCAN_ROWS, (k + 1) * SCAN_ROWS)
        x = rec_scr[r, :]
        rg = _sigmoid(gpre_scr[r, 0:LRU_WIDTH] + gab_ref[...])
        ig = _sigmoid(gpre_scr[r, LRU_WIDTH:] + gxb_ref[...])
        log_a = RG_LRU_C * rg * log_sig_lam
        a = jnp.exp(log_a)
        bx = jnp.sqrt(-jnp.tanh(log_a) * (a * a + 1.0)) * (ig * x)
        hs = []
        for t in range(SCAN_ROWS // MIX_BATCH):
            rt = slice(t * MIX_BATCH, (t + 1) * MIX_BATCH)
            hprev = a[rt, :] * hprev + bx[rt, :]
            hs.append(hprev)
        hs = jnp.concatenate(hs, axis=0)
        hn_scr[r, :] = (jax.nn.gelu(ug_scr[r, :]) * hs).astype(BF16)
    state_scr[...] = hprev

    mixed = jnp.dot(hn_scr[...], wout_ref[...], preferred_element_type=F32)
    o_ref[...] = (h1_prev[...] + mixed).reshape(steps, MIX_BATCH, D_MODEL)
    xpad_scr[0:halo_rows, :] = xpad_scr[rows:rows + halo_rows, :]

    h1_prev[...] = h1_new[...]


def _ffn_odd_call(h, gain1, wg, wu, wd, layer, gain_mix, w_in, conv_w, conv_b, w_gate, ga_b, gx_b,
                  lam, w_out):
    seq, bsz, _ = h.shape
    rows = ODD_STEPS * MIX_BATCH
    tiles_per_seq = seq // ODD_STEPS
    n_tiles = (bsz // MIX_BATCH) * tiles_per_seq
    block = (ODD_STEPS, MIX_BATCH, D_MODEL)

    def in_map(i):
        j = jnp.minimum(i, n_tiles - 1)
        return (j % tiles_per_seq, j // tiles_per_seq, 0)

    def out_map(i):
        j = jnp.maximum(i - 1, 0)
        return (j % tiles_per_seq, j // tiles_per_seq, 0)

    return pl.pallas_call(
        functools.partial(_ffn_odd_kernel, tiles_per_seq=tiles_per_seq),
        out_shape=jax.ShapeDtypeStruct(h.shape, F32),
        grid=(n_tiles + 1,),
        in_specs=[
            pl.BlockSpec(block, in_map),
            pl.BlockSpec(block, lambda i: in_map(i + 1)),
            _resident((1, D_MODEL)),
            _layer_weight(wg, layer),
            _layer_weight(wu, layer),
            _layer_weight(wd, layer),
            _resident((1, D_MODEL)),
            _resident(w_in.shape),
            _resident(conv_w.shape),
            _resident(conv_b.shape),
            _resident(w_gate.shape),
            _resident(ga_b.shape),
            _resident(gx_b.shape),
            _resident(lam.shape),
            _resident(w_out.shape),
        ],
        out_specs=pl.BlockSpec(block, out_map),
        scratch_shapes=[
            pltpu.VMEM((rows, D_MODEL), F32),
            pltpu.VMEM((rows, D_MODEL), F32),
            pltpu.VMEM((2, rows, D_MODEL), BF16),
            pltpu.VMEM((rows, D_FF), BF16),
            pltpu.VMEM((rows, D_MODEL), BF16),
            pltpu.VMEM((rows, LRU_WIDTH), F32),
            pltpu.VMEM(((LRU_HALO + ODD_STEPS) * MIX_BATCH, LRU_WIDTH), F32),
            pltpu.VMEM((rows, LRU_WIDTH), F32),
            pltpu.VMEM((rows, 2 * LRU_WIDTH), F32),
            pltpu.VMEM((MIX_BATCH, LRU_WIDTH), F32),
        ],
        compiler_params=pltpu.CompilerParams(
            dimension_semantics=("arbitrary",), vmem_limit_bytes=FUSED_VMEM_LIMIT_BYTES),
        name="ffn_odd",
    )(h, h, gain1, wg, wu, wd, gain_mix, w_in, conv_w, conv_b, w_gate, ga_b, gx_b, lam, w_out)


def _ffn_even_kernel(sinks_ref, x_ref, xnext_ref, g1_ref, wg_ref, wu_ref, wd_ref, gm_ref, win_ref,
                     bias_ref, cw_ref, cb_ref, lng_ref, lnb_ref, wout_ref, o_ref,
                     h1_new, h1_prev, xn_scr, a_scr, hn_scr, qkv_scr, qseq_scr, kvprev_scr, oseq_scr,
                     attn_scr, xpad_scr, *, tiles_per_seq, batch_major_in):
    steps = o_ref.shape[0]
    rows = steps * MIX_BATCH
    halo_rows = CONV_HALO * MIX_BATCH
    n_keys = WINDOW + steps
    i = pl.program_id(0)
    slot = lax.rem(i, 2)
    seq_tile = lax.rem(i + tiles_per_seq - 1, tiles_per_seq)

    def ffn_norm_chunk(src_ref, dst_slot, k):
        if batch_major_in:
            r0 = k * ROW_CHUNK
            x = src_ref[r0 // steps, r0 % steps:r0 % steps + ROW_CHUNK, :]
        else:
            x = src_ref[k * STEPS_PER_CHUNK:(k + 1) * STEPS_PER_CHUNK].reshape(ROW_CHUNK, D_MODEL)
        xn_scr[dst_slot, k * ROW_CHUNK:(k + 1) * ROW_CHUNK, :] = (
            _rmsnorm_rows(x, g1_ref[...]).astype(BF16))

    @pl.when(i == 0)
    def _():
        h1_prev[...] = jnp.zeros(h1_prev.shape, F32)
        for k in range(rows // ROW_CHUNK):
            ffn_norm_chunk(x_ref, 0, k)

    @pl.when(jnp.logical_or(i == 0, seq_tile == 0))
    def _():
        kvprev_scr[...] = jnp.zeros(kvprev_scr.shape, F32)
        xpad_scr[0:halo_rows, :] = jnp.zeros((halo_rows, CONV_CHANNELS), F32)

    def ffn_up(c0, c1):
        for c in range(c0, c1):
            cols = slice(c * FF_CHUNK, (c + 1) * FF_CHUNK)
            g = jnp.dot(xn_scr[slot], wg_ref[:, cols], preferred_element_type=F32)
            u = jnp.dot(xn_scr[slot], wu_ref[:, cols], preferred_element_type=F32)
            a_scr[:, cols] = _silu_mul(g, u).astype(BF16)
            if c < rows // ROW_CHUNK:
                ffn_norm_chunk(xnext_ref, 1 - slot, c)

    def ffn_down():
        y = jnp.dot(a_scr[...], wd_ref[...], preferred_element_type=F32)
        if batch_major_in:
            for b in range(MIX_BATCH):
                h1_new[:, b, :] = x_ref[b] + 0.5 * y[b * steps:(b + 1) * steps, :]
        else:
            h1_new[...] = x_ref[...] + 0.5 * y.reshape(steps, MIX_BATCH, D_MODEL)

    _norm_tile(h1_prev, gm_ref, hn_scr)
    ffn_up(0, 2)

    for j in range((Q_WIDTH + 2 * KV_WIDTH) // MXU_DIM):
        res = jnp.dot(hn_scr[...], win_ref[:, j * MXU_DIM:(j + 1) * MXU_DIM],
                      preferred_element_type=F32)
        qkv_scr[2 * j] = res[:, :LANES]
        qkv_scr[2 * j + 1] = res[:, LANES:]
    o_g = Q_WIDTH + 2 * KV_WIDTH
    glu_a = jnp.dot(hn_scr[...], win_ref[:, o_g:o_g + CONV_CHANNELS], preferred_element_type=F32)
    glu_b = jnp.dot(hn_scr[...], win_ref[:, o_g + CONV_CHANNELS:], preferred_element_type=F32)
    xpad_scr[halo_rows:, :] = glu_a * _sigmoid(glu_b)

    k_slab = Q_WIDTH // LANES
    v_slab = k_slab + 1
    lane = lax.broadcasted_iota(jnp.int32, (n_keys, LANES), 1)
    qlane = lax.broadcasted_iota(jnp.int32, (steps, LANES), 1)
    scale = 1.0 / math.sqrt(HEAD_DIM)
    variant = jnp.minimum(seq_tile, WINDOW // steps)
    for b in range(MIX_BATCH):
        for j in range(qkv_scr.shape[0]):
            qseq_scr[b, j] = qkv_scr[j, pl.ds(b, steps, stride=MIX_BATCH), :]
    score_blocks = []
    for b in range(MIX_BATCH):
        k_cat = jnp.concatenate([kvprev_scr[b, 0], qseq_scr[b, k_slab]], axis=0)
        v_cat = jnp.concatenate([kvprev_scr[b, 1], qseq_scr[b, v_slab]], axis=0)
        kvprev_scr[b, 0] = k_cat[steps:, :]
        kvprev_scr[b, 1] = v_cat[steps:, :]
        k_sw = pltpu.roll(k_cat, HEAD_DIM, axis=1)
        v_sw = pltpu.roll(v_cat, HEAD_DIM, axis=1)
        k_dup = (jnp.where(lane < HEAD_DIM, k_cat, k_sw).astype(BF16),
                 jnp.where(lane < HEAD_DIM, k_sw, k_cat).astype(BF16))
        v_dup = (jnp.where(lane < HEAD_DIM, v_cat, v_sw).astype(BF16),
                 jnp.where(lane < HEAD_DIM, v_sw, v_cat).astype(BF16))
        for c in range(Q_WIDTH // LANES):
            kvh = (2 * c) // GROUP
            qcol = qseq_scr[b, c] * scale
            for half in range(2):
                keep = (qlane < HEAD_DIM) if half == 0 else (qlane >= HEAD_DIM)
                q = jnp.where(keep, qcol, 0.0).astype(BF16)
                sc = lax.dot_general(q, k_dup[kvh], (((1,), (1,)), ((), ())),
                                     preferred_element_type=F32)
                score_blocks.append((2 * c + half, sc, v_dup[kvh]))
    ffn_up(2, 9)
    prob_blocks = []
    for head, sc, v in score_blocks:
        sc = sc + bias_ref[variant, head]
        sink = sinks_ref[head]
        m = jnp.maximum(jnp.max(sc, axis=-1, keepdims=True), sink)
        e = jnp.exp(sc - m)
        den = jnp.sum(e, axis=-1, keepdims=True) + jnp.exp(sink - m)
        prob_blocks.append(((e / den).astype(BF16), v))
    outs = [jnp.dot(p, v, preferred_element_type=F32) for p, v in prob_blocks]
    ffn_up(9, N_FF_CHUNKS)
    ffn_down()
    for b in range(MIX_BATCH):
        for c in range(Q_WIDTH // LANES):
            n = (b * (Q_WIDTH // LANES) + c) * 2
            oseq_scr[b, c] = jnp.where(qlane < HEAD_DIM, outs[n], outs[n + 1])
    for b in range(MIX_BATCH):
        for c in range(Q_WIDTH // LANES):
            attn_scr[c, pl.ds(b, steps, stride=MIX_BATCH), :] = oseq_scr[b, c]
    for c in range(Q_WIDTH // LANES):
        hn_scr[:, c * LANES:(c + 1) * LANES] = attn_scr[c].astype(BF16)

    first_step = CONV_HALO - (CONV_WIDTH - 1)
    for k in range(rows // ROW_CHUNK):
        acc = jnp.zeros((ROW_CHUNK, CONV_CHANNELS), F32)
        for tap in range(CONV_WIDTH):
            r0 = (k * STEPS_PER_CHUNK + first_step + tap) * MIX_BATCH
            acc = acc + xpad_scr[r0:r0 + ROW_CHUNK, :] * cw_ref[tap:tap + 1, :]
        y = acc + cb_ref[...]
        mu = jnp.mean(y, axis=-1, keepdims=True)
        yc = y - mu
        var = jnp.mean(yc * yc, axis=-1, keepdims=True)
        z = yc * lax.rsqrt(var + LN_EPS) * lng_ref[...] + lnb_ref[...]
        hn_scr[k * ROW_CHUNK:(k + 1) * ROW_CHUNK, Q_WIDTH:] = (z * _sigmoid(z)).astype(BF16)

    mixed = jnp.dot(hn_scr[...], wout_ref[...], preferred_element_type=F32)
    o_ref[...] = h1_prev[...] + mixed.reshape(steps, MIX_BATCH, D_MODEL)
    xpad_scr[0:halo_rows, :] = xpad_scr[rows:rows + halo_rows, :]

    h1_prev[...] = h1_new[...]


def _ffn_even_call(x, gain1, wg, wu, wd, layer, gain_mix, w_in, sinks, bias_tbl, conv_w, conv_b,
                   ln_g, ln_b, w_out, batch_major_in=False):
    if batch_major_in:
        bsz, seq, _ = x.shape
    else:
        seq, bsz, _ = x.shape
    rows = EVEN_STEPS * MIX_BATCH
    tiles_per_seq = seq // EVEN_STEPS
    n_tiles = (bsz // MIX_BATCH) * tiles_per_seq
    block = (EVEN_STEPS, MIX_BATCH, D_MODEL)

    def in_map(i):
        j = jnp.minimum(i, n_tiles - 1)
        if batch_major_in:
            return (j // tiles_per_seq, j % tiles_per_seq, 0)
        return (j % tiles_per_seq, j // tiles_per_seq, 0)

    def out_map(i):
        j = jnp.maximum(i - 1, 0)
        return (j % tiles_per_seq, j // tiles_per_seq, 0)

    in_block = (MIX_BATCH, EVEN_STEPS, D_MODEL) if batch_major_in else block
    return pl.pallas_call(
        functools.partial(_ffn_even_kernel, tiles_per_seq=tiles_per_seq,
                          batch_major_in=batch_major_in),
        out_shape=jax.ShapeDtypeStruct((seq, bsz, D_MODEL), F32),
        grid=(n_tiles + 1,),
        in_specs=[
            pl.BlockSpec(memory_space=pltpu.SMEM),
            pl.BlockSpec(in_block, in_map),
            pl.BlockSpec(in_block, lambda i: in_map(i + 1)),
            _resident((1, D_MODEL)),
            _layer_weight(wg, layer),
            _layer_weight(wu, layer),
            _layer_weight(wd, layer),
            _resident((1, D_MODEL)),
            _resident(w_in.shape),
            _resident(bias_tbl.shape),
            _resident(conv_w.shape),
            _resident(conv_b.shape),
            _resident(ln_g.shape),
            _resident(ln_b.shape),
            _resident(w_out.shape),
        ],
        out_specs=pl.BlockSpec(block, out_map),
        scratch_shapes=[
            pltpu.VMEM(block, F32),
            pltpu.VMEM(block, F32),
            pltpu.VMEM((2, rows, D_MODEL), BF16),
            pltpu.VMEM((rows, D_FF), BF16),
            pltpu.VMEM((rows, D_MODEL), BF16),
            pltpu.VMEM(((Q_WIDTH + 2 * KV_WIDTH) // LANES, rows, LANES), F32),
            pltpu.VMEM((MIX_BATCH, (Q_WIDTH + 2 * KV_WIDTH) // LANES, EVEN_STEPS, LANES), F32),
            pltpu.VMEM((MIX_BATCH, 2, WINDOW, LANES), F32),
            pltpu.VMEM((MIX_BATCH, Q_WIDTH // LANES, EVEN_STEPS, LANES), F32),
            pltpu.VMEM((Q_WIDTH // LANES, rows, LANES), F32),
            pltpu.VMEM(((CONV_HALO + EVEN_STEPS) * MIX_BATCH, CONV_CHANNELS), F32),
        ],
        compiler_params=pltpu.CompilerParams(
            dimension_semantics=("arbitrary",), vmem_limit_bytes=FUSED_VMEM_LIMIT_BYTES),
        name="ffn_even",
    )(sinks, x, x, gain1, wg, wu, wd, gain_mix, w_in, bias_tbl, conv_w, conv_b, ln_g, ln_b, w_out)


def _t5_bucket_table(steps):
    qi = np.arange(steps)[:, None]
    sj = np.arange(WINDOW + steps)[None, :]
    dist = qi + WINDOW - sj
    n = np.maximum(dist, 0)
    max_exact = NUM_BUCKETS // 2
    nf = np.maximum(n, max_exact).astype(np.float32)
    large = max_exact + (np.log(nf / max_exact) / math.log(MAX_DISTANCE / max_exact)
                         * (NUM_BUCKETS - max_exact)).astype(np.int32)
    large = np.minimum(large, NUM_BUCKETS - 1)
    bucket = np.where(n < max_exact, n, large)
    in_window = (dist >= 0) & (dist < WINDOW)
    return bucket, in_window, sj


def _attention_bias_tables(rel_bias, steps):
    bucket, in_window, sj = _t5_bucket_table(steps)
    onehot = (jnp.asarray(bucket, jnp.int32)[..., None] == jnp.arange(NUM_BUCKETS)).astype(F32)
    bias = jnp.einsum("qsb,bh->hqs", onehot, rel_bias.astype(F32),
                      precision=lax.Precision.HIGHEST)
    tables = []
    for v in range(WINDOW // steps + 1):
        valid = in_window & (sj >= WINDOW - v * steps)
        tables.append(jnp.where(valid[None], bias, NEG_INF))
    return jnp.stack(tables)


def kernel(x, norm_ffn1, ffn1_wg, ffn1_wu, ffn1_wd, norm_mix, norm_ffn2, ffn2_wg, ffn2_wu, ffn2_wd,
           rel_bias, even_w_in, attn_sinks, conv_b_w, conv_b_b, conv_ln_g, conv_ln_b, even_w_out,
           odd_w_in, lru_conv_w, lru_conv_b, gate_a_w, gate_a_b, gate_x_w, gate_x_b, lru_lambda,
           odd_w_out, norm_final):
    bsz, seq, d = x.shape
    depth = norm_ffn1.shape[0]
    bias_tbl = _attention_bias_tables(rel_bias, EVEN_STEPS)
    row = lambda v: v.reshape(1, -1)

    bf16 = lambda w: w.astype(BF16)
    ffn1 = (bf16(ffn1_wg), bf16(ffn1_wu), bf16(ffn1_wd))
    ffn2 = (bf16(ffn2_wg), bf16(ffn2_wu), bf16(ffn2_wd))

    h = x
    for layer in range(depth):
        if layer % 2 == 0:
            e = layer // 2
            h = _ffn_even_call(h, row(norm_ffn1[layer]), *ffn1, layer, row(norm_mix[layer]),
                               even_w_in[e].astype(BF16), attn_sinks[e], bias_tbl, conv_b_w[e],
                               row(conv_b_b[e]), row(conv_ln_g[e]), row(conv_ln_b[e]),
                               even_w_out[e].astype(BF16),
                               batch_major_in=(layer == 0))
        else:
            o = layer // 2
            w_gate = jnp.concatenate([gate_a_w[o], gate_x_w[o]], axis=-1).astype(BF16)
            h = _ffn_odd_call(h, row(norm_ffn1[layer]), *ffn1, layer, row(norm_mix[layer]),
                              odd_w_in[o].astype(BF16), lru_conv_w[o], row(lru_conv_b[o]), w_gate,
                              row(gate_a_b[o]), row(gate_x_b[o]), row(lru_lambda[o]),
                              odd_w_out[o].astype(BF16))
        last = layer == depth - 1
        h = _ffn_call(h, row(norm_ffn2[layer]), *ffn2, layer,
                      final_gain=row(norm_final) if last else None)
    return h
```

```python
import functools
import math

import jax
import jax.numpy as jnp
import numpy as np
from jax import lax
from jax.experimental import pallas as pl
from jax.experimental.pallas import tpu as pltpu

D_MODEL = 1024
D_FF = 2816
HEAD_DIM = 64
N_Q_HEADS = 8
N_KV_HEADS = 2
GROUP = N_Q_HEADS // N_KV_HEADS
WINDOW = 128
Q_WIDTH = N_Q_HEADS * HEAD_DIM
KV_WIDTH = N_KV_HEADS * HEAD_DIM
NUM_BUCKETS = 32
MAX_DISTANCE = 128
CONV_CHANNELS = 512
CONV_WIDTH = 31
EVEN_IN = Q_WIDTH + 2 * KV_WIDTH + 2 * CONV_CHANNELS
LRU_WIDTH = 1024
LRU_HEADS = 8
LRU_BLOCK = 128
LRU_CONV_WIDTH = 4
RG_LRU_C = 8.0
RMS_EPS = 1e-6
LN_EPS = 1e-5
NEG_INF = -1e30

LANES = 128
SUBLANES = 8
MXU_DIM = 256
VMEM_LIMIT_BYTES = 56 * 1024 * 1024
FUSED_VMEM_LIMIT_BYTES = 60 * 1024 * 1024

FF_CHUNK = MXU_DIM
N_FF_CHUNKS = D_FF // FF_CHUNK
FFN_ROWS = 1024
ROW_CHUNK = 64
MIX_BATCH = SUBLANES
STEPS_PER_CHUNK = ROW_CHUNK // MIX_BATCH
EVEN_STEPS = 64
ODD_STEPS = 64
FINAL_STEPS = 64
CONV_HALO = 32
LRU_HALO = LRU_CONV_WIDTH - 1
SCAN_ROWS = 2 * MIX_BATCH
CONV_ROWS = 64

BF16 = jnp.bfloat16
F32 = jnp.float32


def _resident(shape):
    nd = len(shape)
    return pl.BlockSpec(shape, lambda *_: (0,) * nd, pipeline_mode=pl.Buffered(1))


def _sigmoid(x):
    return 0.5 * jnp.tanh(0.5 * x) + 0.5


def _silu_mul(g, u):
    half = 0.5 * g
    return (half + half * jnp.tanh(half)) * u


def _rmsnorm_rows(x, g):
    ms = jnp.mean(x * x, axis=-1, keepdims=True)
    return x * lax.rsqrt(ms + RMS_EPS) * g


def _norm_tile(h_ref, g_ref, hn_scr):
    steps = h_ref.shape[0]
    for i in range(steps // STEPS_PER_CHUNK):
        x = h_ref[i * STEPS_PER_CHUNK:(i + 1) * STEPS_PER_CHUNK].reshape(ROW_CHUNK, D_MODEL)
        hn_scr[i * ROW_CHUNK:(i + 1) * ROW_CHUNK, :] = _rmsnorm_rows(x, g_ref[...]).astype(BF16)


def _ffn_kernel(*refs, final_norm):
    if final_norm:
        (x_ref, xnext_ref, g_ref, wg_ref, wu_ref, wd_ref, gfin_ref, o_ref,
         xn_scr, a_scr, res_scr) = refs
    else:
        x_ref, xnext_ref, g_ref, wg_ref, wu_ref, wd_ref, o_ref, xn_scr, a_scr = refs
    rows = xn_scr.shape[1]
    i = pl.program_id(0)
    slot = lax.rem(i, 2)

    def norm_chunk(src_ref, dst_slot, k):
        if final_norm:
            x = src_ref[k * STEPS_PER_CHUNK:(k + 1) * STEPS_PER_CHUNK].reshape(ROW_CHUNK, D_MODEL)
        else:
            x = src_ref[k * ROW_CHUNK:(k + 1) * ROW_CHUNK, :]
        xn_scr[dst_slot, k * ROW_CHUNK:(k + 1) * ROW_CHUNK, :] = (
            _rmsnorm_rows(x, g_ref[...]).astype(BF16))

    n_norm = rows // ROW_CHUNK
    norm_per_dot = pl.cdiv(n_norm, N_FF_CHUNKS)

    @pl.when(i == 0)
    def _():
        for k in range(n_norm):
            norm_chunk(x_ref, 0, k)

    for c in range(N_FF_CHUNKS):
        cols = slice(c * FF_CHUNK, (c + 1) * FF_CHUNK)
        g = jnp.dot(xn_scr[slot], wg_ref[:, cols], preferred_element_type=F32)
        u = jnp.dot(xn_scr[slot], wu_ref[:, cols], preferred_element_type=F32)
        a_scr[:, cols] = _silu_mul(g, u).astype(BF16)
        for k in range(c * norm_per_dot, min((c + 1) * norm_per_dot, n_norm)):
            norm_chunk(xnext_ref, 1 - slot, k)
    y = jnp.dot(a_scr[...], wd_ref[...], preferred_element_type=F32)
    if final_norm:
        res_scr[...] = x_ref[...] + 0.5 * y.reshape(res_scr.shape)
        for b in range(o_ref.shape[0]):
            o_ref[b] = _rmsnorm_rows(res_scr[:, b, :], gfin_ref[...])
    else:
        o_ref[...] = x_ref[...] + 0.5 * y


def _layer_weight(w, layer):
    return pl.BlockSpec((None,) + w.shape[1:], lambda *_: (layer, 0, 0),
                        pipeline_mode=pl.Buffered(1))


def _ffn_call(x, gain, wg, wu, wd, layer, final_gain=None):
    seq, bsz, _ = x.shape
    final_norm = final_gain is not None
    if final_norm:
        tile_rows = FINAL_STEPS * MIX_BATCH
        n_st = seq // FINAL_STEPS
        n_tiles = (bsz // MIX_BATCH) * n_st
        block = (FINAL_STEPS, MIX_BATCH, D_MODEL)
        x_spec = pl.BlockSpec(block, lambda i: (i % n_st, i // n_st, 0))

        def next_map(i):
            j = jnp.minimum(i + 1, n_tiles - 1)
            return (j % n_st, j // n_st, 0)

        xnext_spec = pl.BlockSpec(block, next_map)
        out_shape = jax.ShapeDtypeStruct((bsz, seq, D_MODEL), F32)
        out_spec = pl.BlockSpec((MIX_BATCH, FINAL_STEPS, D_MODEL),
                                lambda i: (i // n_st, i % n_st, 0))
        operands = (x, x, gain, wg, wu, wd, final_gain)
        extra_specs = [_resident((1, D_MODEL))]
        extra_scratch = [pltpu.VMEM(block, F32)]
    else:
        tile_rows = FFN_ROWS
        x = x.reshape(seq * bsz, D_MODEL)
        n_tiles = x.shape[0] // FFN_ROWS
        x_spec = pl.BlockSpec((FFN_ROWS, D_MODEL), lambda i: (i, 0))
        xnext_spec = pl.BlockSpec((FFN_ROWS, D_MODEL),
                                  lambda i: (jnp.minimum(i + 1, n_tiles - 1), 0))
        out_shape = jax.ShapeDtypeStruct(x.shape, F32)
        out_spec = pl.BlockSpec((FFN_ROWS, D_MODEL), lambda i: (i, 0))
        operands = (x, x, gain, wg, wu, wd)
        extra_specs = []
        extra_scratch = []
    out = pl.pallas_call(
        functools.partial(_ffn_kernel, final_norm=final_norm),
        out_shape=out_shape,
        grid=(n_tiles,),
        in_specs=[
            x_spec,
            xnext_spec,
            _resident((1, D_MODEL)),
            _layer_weight(wg, layer),
            _layer_weight(wu, layer),
            _layer_weight(wd, layer),
        ] + extra_specs,
        out_specs=out_spec,
        scratch_shapes=[
            pltpu.VMEM((2, tile_rows, D_MODEL), BF16),
            pltpu.VMEM((tile_rows, D_FF), BF16),
        ] + extra_scratch,
        compiler_params=pltpu.CompilerParams(
            dimension_semantics=("arbitrary",), vmem_limit_bytes=VMEM_LIMIT_BYTES),
        name="ffn_final" if final_norm else "ffn",
    )(*operands)
    return out if final_norm else out.reshape(seq, bsz, D_MODEL)


def _ffn_odd_kernel(x_ref, g1_ref, wg_ref, wu_ref, wd_ref, gm_ref, win_ref, cw_ref, cb_ref,
                    wgate_ref, gab_ref, gxb_ref, lam_ref, wout_ref, o_ref,
                    h1_new, h1_prev, xn_scr, a_scr, hn_scr, ug_scr, xpad_scr, rec_scr, gpre_scr,
                    state_scr, *, tiles_per_seq):
    steps = x_ref.shape[0]
    rows = steps * MIX_BATCH
    halo_rows = LRU_HALO * MIX_BATCH
    i = pl.program_id(0)

    @pl.when(i == 0)
    def _():
        h1_prev[...] = jnp.zeros(h1_prev.shape, F32)

    @pl.when(jnp.logical_or(i == 0, lax.rem(i + tiles_per_seq - 1, tiles_per_seq) == 0))
    def _():
        xpad_scr[0:halo_rows, :] = jnp.zeros((halo_rows, LRU_WIDTH), F32)
        state_scr[...] = jnp.zeros((MIX_BATCH, LRU_WIDTH), F32)

    def ffn_up(c0, c1):
        for c in range(c0, c1):
            cols = slice(c * FF_CHUNK, (c + 1) * FF_CHUNK)
            g = jnp.dot(xn_scr[...], wg_ref[:, cols], preferred_element_type=F32)
            u = jnp.dot(xn_scr[...], wu_ref[:, cols], preferred_element_type=F32)
            a_scr[:, cols] = _silu_mul(g, u).astype(BF16)

    for k in range(rows // ROW_CHUNK):
        x = x_ref[k * STEPS_PER_CHUNK:(k + 1) * STEPS_PER_CHUNK].reshape(ROW_CHUNK, D_MODEL)
        xn_scr[k * ROW_CHUNK:(k + 1) * ROW_CHUNK, :] = _rmsnorm_rows(x, g1_ref[...]).astype(BF16)
    for k in range(rows // ROW_CHUNK):
        r = slice(k * ROW_CHUNK, (k + 1) * ROW_CHUNK)
        hn_scr[r, :] = _rmsnorm_rows(h1_prev[r, :], gm_ref[...]).astype(BF16)
    ffn_up(0, 2)
    ug_scr[...] = jnp.dot(hn_scr[...], win_ref[:, 0:LRU_WIDTH], preferred_element_type=F32)
    xpad_scr[halo_rows:, :] = jnp.dot(hn_scr[...], win_ref[:, LRU_WIDTH:],
                                      preferred_element_type=F32)
    ffn_up(2, 7)

    for k in range(rows // ROW_CHUNK):
        acc = jnp.zeros((ROW_CHUNK, LRU_WIDTH), F32)
        for tap in range(LRU_CONV_WIDTH):
            r0 = (k * STEPS_PER_CHUNK + tap) * MIX_BATCH
            acc = acc + xpad_scr[r0:r0 + ROW_CHUNK, :] * cw_ref[tap:tap + 1, :]
        rec_scr[k * ROW_CHUNK:(k + 1) * ROW_CHUNK, :] = acc + cb_ref[...]

    for hd in range(LRU_HEADS):
        cs = slice(hd * LRU_BLOCK, (hd + 1) * LRU_BLOCK)
        pre = jnp.dot(rec_scr[:, cs].astype(BF16), wgate_ref[hd], preferred_element_type=F32)
        gpre_scr[:, cs] = pre[:, :LRU_BLOCK]
        gpre_scr[:, LRU_WIDTH + hd * LRU_BLOCK:LRU_WIDTH + (hd + 1) * LRU_BLOCK] = pre[:, LRU_BLOCK:]

    ffn_up(7, N_FF_CHUNKS)
    y = jnp.dot(a_scr[...], wd_ref[...], preferred_element_type=F32)
    h1_new[...] = x_ref[...].reshape(rows, D_MODEL) + 0.5 * y

    lam = lam_ref[...]
    log_sig_lam = -(jnp.maximum(-lam, 0.0) + jnp.log1p(jnp.exp(-jnp.abs(lam))))

    hprev = state_scr[...]
    for k in range(rows // SCAN_ROWS):
        r = slice(k * SCAN_ROWS, (k + 1) * SCAN_ROWS)
        x = rec_scr[r, :]
        rg = _sigmoid(gpre_scr[r, 0:LRU_WIDTH] + gab_ref[...])
        ig = _sigmoid(gpre_scr[r, LRU_WIDTH:] + gxb_ref[...])
        log_a = RG_LRU_C * rg * log_sig_lam
        a = jnp.exp(log_a)
        bx = jnp.sqrt(-jnp.tanh(log_a) * (a * a + 1.0)) * (ig * x)
        hs = []
        for t in range(SCAN_ROWS // MIX_BATCH):
            rt = slice(t * MIX_BATCH, (t + 1) * MIX_BATCH)
            hprev = a[rt, :] * hprev + bx[rt, :]
            hs.append(hprev)
        hs = jnp.concatenate(hs, axis=0)
        hn_scr[r, :] = (jax.nn.gelu(ug_scr[r, :]) * hs).astype(BF16)
    state_scr[...] = hprev

    mixed = jnp.dot(hn_scr[...], wout_ref[...], preferred_element_type=F32)
    o_ref[...] = (h1_prev[...] + mixed).reshape(steps, MIX_BATCH, D_MODEL)
    xpad_scr[0:halo_rows, :] = xpad_scr[rows:rows + halo_rows, :]

    h1_prev[...] = h1_new[...]


def _ffn_odd_call(h, gain1, wg, wu, wd, layer, gain_mix, w_in, conv_w, conv_b, w_gate, ga_b, gx_b,
                  lam, w_out):
    seq, bsz, _ = h.shape
    rows = ODD_STEPS * MIX_BATCH
    tiles_per_seq = seq // ODD_STEPS
    n_tiles = (bsz // MIX_BATCH) * tiles_per_seq
    block = (ODD_STEPS, MIX_BATCH, D_MODEL)

    def in_map(i):
        j = jnp.minimum(i, n_tiles - 1)
        return (j % tiles_per_seq, j // tiles_per_seq, 0)

    def out_map(i):
        j = jnp.maximum(i - 1, 0)
        return (j % tiles_per_seq, j // tiles_per_seq, 0)

    return pl.pallas_call(
        functools.partial(_ffn_odd_kernel, tiles_per_seq=tiles_per_seq),
        out_shape=jax.ShapeDtypeStruct(h.shape, F32),
        grid=(n_tiles + 1,),
        in_specs=[
            pl.BlockSpec(block, in_map),
            _resident((1, D_MODEL)),
            _layer_weight(wg, layer),
            _layer_weight(wu, layer),
            _layer_weight(wd, layer),
            _resident((1, D_MODEL)),
            _resident(w_in.shape),
            _resident(conv_w.shape),
            _resident(conv_b.shape),
            _resident(w_gate.shape),
            _resident(ga_b.shape),
            _resident(gx_b.shape),
            _resident(lam.shape),
            _resident(w_out.shape),
        ],
        out_specs=pl.BlockSpec(block, out_map),
        scratch_shapes=[
            pltpu.VMEM((rows, D_MODEL), F32),
            pltpu.VMEM((rows, D_MODEL), F32),
            pltpu.VMEM((rows, D_MODEL), BF16),
            pltpu.VMEM((rows, D_FF), BF16),
            pltpu.VMEM((rows, D_MODEL), BF16),
            pltpu.VMEM((rows, LRU_WIDTH), F32),
            pltpu.VMEM(((LRU_HALO + ODD_STEPS) * MIX_BATCH, LRU_WIDTH), F32),
            pltpu.VMEM((rows, LRU_WIDTH), F32),
            pltpu.VMEM((rows, 2 * LRU_WIDTH), F32),
            pltpu.VMEM((MIX_BATCH, LRU_WIDTH), F32),
        ],
        compiler_params=pltpu.CompilerParams(
            dimension_semantics=("arbitrary",), vmem_limit_bytes=FUSED_VMEM_LIMIT_BYTES),
        name="ffn_odd",
    )(h, gain1, wg, wu, wd, gain_mix, w_in, conv_w, conv_b, w_gate, ga_b, gx_b, lam, w_out)


def _ffn_even_kernel(sinks_ref, x_ref, g1_ref, wg_ref, wu_ref, wd_ref, gm_ref, win_ref, bias_ref,
                     cw_ref, cb_ref, lng_ref, lnb_ref, wout_ref, o_ref,
                     h1_new, h1_prev, xn_scr, a_scr, hn_scr, qkv_scr, qseq_scr, kvprev_scr, oseq_scr,
                     attn_scr, xpad_scr, *, tiles_per_seq, batch_major_in):
    steps = o_ref.shape[0]
    rows = steps * MIX_BATCH
    halo_rows = CONV_HALO * MIX_BATCH
    n_keys = WINDOW + steps
    i = pl.program_id(0)
    seq_tile = lax.rem(i + tiles_per_seq - 1, tiles_per_seq)

    @pl.when(i == 0)
    def _():
        h1_prev[...] = jnp.zeros(h1_prev.shape, F32)

    @pl.when(jnp.logical_or(i == 0, seq_tile == 0))
    def _():
        kvprev_scr[...] = jnp.zeros(kvprev_scr.shape, F32)
        xpad_scr[0:halo_rows, :] = jnp.zeros((halo_rows, CONV_CHANNELS), F32)

    def ffn_norm():
        for k in range(rows // ROW_CHUNK):
            if batch_major_in:
                r0 = k * ROW_CHUNK
                x = x_ref[r0 // steps, r0 % steps:r0 % steps + ROW_CHUNK, :]
            else:
                x = x_ref[k * STEPS_PER_CHUNK:(k + 1) * STEPS_PER_CHUNK].reshape(ROW_CHUNK,
                                                                                 D_MODEL)
            xn_scr[k * ROW_CHUNK:(k + 1) * ROW_CHUNK, :] = (
                _rmsnorm_rows(x, g1_ref[...]).astype(BF16))

    def ffn_up(c0, c1):
        for c in range(c0, c1):
            cols = slice(c * FF_CHUNK, (c + 1) * FF_CHUNK)
            g = jnp.dot(xn_scr[...], wg_ref[:, cols], preferred_element_type=F32)
            u = jnp.dot(xn_scr[...], wu_ref[:, cols], preferred_element_type=F32)
            a_scr[:, cols] = _silu_mul(g, u).astype(BF16)

    def ffn_down():
        y = jnp.dot(a_scr[...], wd_ref[...], preferred_element_type=F32)
        if batch_major_in:
            for b in range(MIX_BATCH):
                h1_new[:, b, :] = x_ref[b] + 0.5 * y[b * steps:(b + 1) * steps, :]
        else:
            h1_new[...] = x_ref[...] + 0.5 * y.reshape(steps, MIX_BATCH, D_MODEL)

    ffn_norm()
    _norm_tile(h1_prev, gm_ref, hn_scr)
    ffn_up(0, 2)

    for j in range((Q_WIDTH + 2 * KV_WIDTH) // MXU_DIM):
        res = jnp.dot(hn_scr[...], win_ref[:, j * MXU_DIM:(j + 1) * MXU_DIM],
                      preferred_element_type=F32)
        qkv_scr[2 * j] = res[:, :LANES]
        qkv_scr[2 * j + 1] = res[:, LANES:]
    o_g = Q_WIDTH + 2 * KV_WIDTH
    glu_a = jnp.dot(hn_scr[...], win_ref[:, o_g:o_g + CONV_CHANNELS], preferred_element_type=F32)
    glu_b = jnp.dot(hn_scr[...], win_ref[:, o_g + CONV_CHANNELS:], preferred_element_type=F32)
    xpad_scr[halo_rows:, :] = glu_a * _sigmoid(glu_b)

    k_slab = Q_WIDTH // LANES
    v_slab = k_slab + 1
    lane = lax.broadcasted_iota(jnp.int32, (n_keys, LANES), 1)
    qlane = lax.broadcasted_iota(jnp.int32, (steps, LANES), 1)
    scale = 1.0 / math.sqrt(HEAD_DIM)
    variant = jnp.minimum(seq_tile, WINDOW // steps)
    for b in range(MIX_BATCH):
        for j in range(qkv_scr.shape[0]):
            qseq_scr[b, j] = qkv_scr[j, pl.ds(b, steps, stride=MIX_BATCH), :]
    score_blocks = []
    for b in range(MIX_BATCH):
        k_cat = jnp.concatenate([kvprev_scr[b, 0], qseq_scr[b, k_slab]], axis=0)
        v_cat = jnp.concatenate([kvprev_scr[b, 1], qseq_scr[b, v_slab]], axis=0)
        kvprev_scr[b, 0] = k_cat[steps:, :]
        kvprev_scr[b, 1] = v_cat[steps:, :]
        k_sw = pltpu.roll(k_cat, HEAD_DIM, axis=1)
        v_sw = pltpu.roll(v_cat, HEAD_DIM, axis=1)
        k_dup = (jnp.where(lane < HEAD_DIM, k_cat, k_sw).astype(BF16),
                 jnp.where(lane < HEAD_DIM, k_sw, k_cat).astype(BF16))
        v_dup = (jnp.where(lane < HEAD_DIM, v_cat, v_sw).astype(BF16),
                 jnp.where(lane < HEAD_DIM, v_sw, v_cat).astype(BF16))
        for c in range(Q_WIDTH // LANES):
            kvh = (2 * c) // GROUP
            qcol = qseq_scr[b, c] * scale
            for half in range(2):
                keep = (qlane < HEAD_DIM) if half == 0 else (qlane >= HEAD_DIM)
                q = jnp.where(keep, qcol, 0.0).astype(BF16)
                sc = lax.dot_general(q, k_dup[kvh], (((1,), (1,)), ((), ())),
                                     preferred_element_type=F32)
                score_blocks.append((2 * c + half, sc, v_dup[kvh]))
    ffn_up(2, 9)
    prob_blocks = []
    for head, sc, v in score_blocks:
        sc = sc + bias_ref[variant, head]
        sink = sinks_ref[head]
        m = jnp.maximum(jnp.max(sc, axis=-1, keepdims=True), sink)
        e = jnp.exp(sc - m)
        den = jnp.sum(e, axis=-1, keepdims=True) + jnp.exp(sink - m)
        prob_blocks.append(((e / den).astype(BF16), v))
    outs = [jnp.dot(p, v, preferred_element_type=F32) for p, v in prob_blocks]
    ffn_up(9, N_FF_CHUNKS)
    ffn_down()
    for b in range(MIX_BATCH):
        for c in range(Q_WIDTH // LANES):
            n = (b * (Q_WIDTH // LANES) + c) * 2
            oseq_scr[b, c] = jnp.where(qlane < HEAD_DIM, outs[n], outs[n + 1])
    for b in range(MIX_BATCH):
        for c in range(Q_WIDTH // LANES):
            attn_scr[c, pl.ds(b, steps, stride=MIX_BATCH), :] = oseq_scr[b, c]
    for c in range(Q_WIDTH // LANES):
        hn_scr[:, c * LANES:(c + 1) * LANES] = attn_scr[c].astype(BF16)

    first_step = CONV_HALO - (CONV_WIDTH - 1)
    for k in range(rows // ROW_CHUNK):
        acc = jnp.zeros((ROW_CHUNK, CONV_CHANNELS), F32)
        for tap in range(CONV_WIDTH):
            r0 = (k * STEPS_PER_CHUNK + first_step + tap) * MIX_BATCH
            acc = acc + xpad_scr[r0:r0 + ROW_CHUNK, :] * cw_ref[tap:tap + 1, :]
        y = acc + cb_ref[...]
        mu = jnp.mean(y, axis=-1, keepdims=True)
        yc = y - mu
        var = jnp.mean(yc * yc, axis=-1, keepdims=True)
        z = yc * lax.rsqrt(var + LN_EPS) * lng_ref[...] + lnb_ref[...]
        hn_scr[k * ROW_CHUNK:(k + 1) * ROW_CHUNK, Q_WIDTH:] = (z * _sigmoid(z)).astype(BF16)

    mixed = jnp.dot(hn_scr[...], wout_ref[...], preferred_element_type=F32)
    o_ref[...] = h1_prev[...] + mixed.reshape(steps, MIX_BATCH, D_MODEL)
    xpad_scr[0:halo_rows, :] = xpad_scr[rows:rows + halo_rows, :]

    h1_prev[...] = h1_new[...]


def _ffn_even_call(x, gain1, wg, wu, wd, layer, gain_mix, w_in, sinks, bias_tbl, conv_w, conv_b,
                   ln_g, ln_b, w_out, batch_major_in=False):
    if batch_major_in:
        bsz, seq, _ = x.shape
    else:
        seq, bsz, _ = x.shape
    rows = EVEN_STEPS * MIX_BATCH
    tiles_per_seq = seq // EVEN_STEPS
    n_tiles = (bsz // MIX_BATCH) * tiles_per_seq
    block = (EVEN_STEPS, MIX_BATCH, D_MODEL)

    def in_map(i):
        j = jnp.minimum(i, n_tiles - 1)
        if batch_major_in:
            return (j // tiles_per_seq, j % tiles_per_seq, 0)
        return (j % tiles_per_seq, j // tiles_per_seq, 0)

    def out_map(i):
        j = jnp.maximum(i - 1, 0)
        return (j % tiles_per_seq, j // tiles_per_seq, 0)

    in_block = (MIX_BATCH, EVEN_STEPS, D_MODEL) if batch_major_in else block
    return pl.pallas_call(
        functools.partial(_ffn_even_kernel, tiles_per_seq=tiles_per_seq,
                          batch_major_in=batch_major_in),
        out_shape=jax.ShapeDtypeStruct((seq, bsz, D_MODEL), F32),
        grid=(n_tiles + 1,),
        in_specs=[
            pl.BlockSpec(memory_space=pltpu.SMEM),
            pl.BlockSpec(in_block, in_map),
            _resident((1, D_MODEL)),
            _layer_weight(wg, layer),
            _layer_weight(wu, layer),
            _layer_weight(wd, layer),
            _resident((1, D_MODEL)),
            _resident(w_in.shape),
            _resident(bias_tbl.shape),
            _resident(conv_w.shape),
            _resident(conv_b.shape),
            _resident(ln_g.shape),
            _resident(ln_b.shape),
            _resident(w_out.shape),
        ],
        out_specs=pl.BlockSpec(block, out_map),
        scratch_shapes=[
            pltpu.VMEM(block, F32),
            pltpu.VMEM(block, F32),
            pltpu.VMEM((rows, D_MODEL), BF16),
            pltpu.VMEM((rows, D_FF), BF16),
            pltpu.VMEM((rows, D_MODEL), BF16),
            pltpu.VMEM(((Q_WIDTH + 2 * KV_WIDTH) // LANES, rows, LANES), F32),
            pltpu.VMEM((MIX_BATCH, (Q_WIDTH + 2 * KV_WIDTH) // LANES, EVEN_STEPS, LANES), F32),
            pltpu.VMEM((MIX_BATCH, 2, WINDOW, LANES), F32),
            pltpu.VMEM((MIX_BATCH, Q_WIDTH // LANES, EVEN_STEPS, LANES), F32),
            pltpu.VMEM((Q_WIDTH // LANES, rows, LANES), F32),
            pltpu.VMEM(((CONV_HALO + EVEN_STEPS) * MIX_BATCH, CONV_CHANNELS), F32),
        ],
        compiler_params=pltpu.CompilerParams(
            dimension_semantics=("arbitrary",), vmem_limit_bytes=FUSED_VMEM_LIMIT_BYTES),
        name="ffn_even",
    )(sinks, x, gain1, wg, wu, wd, gain_mix, w_in, bias_tbl, conv_w, conv_b, ln_g, ln_b, w_out)


def _t5_bucket_table(steps):
    qi = np.arange(steps)[:, None]
    sj = np.arange(WINDOW + steps)[None, :]
    dist = qi + WINDOW - sj
    n = np.maximum(dist, 0)
    max_exact = NUM_BUCKETS // 2
    nf = np.maximum(n, max_exact).astype(np.float32)
    large = max_exact + (np.log(nf / max_exact) / math.log(MAX_DISTANCE / max_exact)
                         * (NUM_BUCKETS - max_exact)).astype(np.int32)
    large = np.minimum(large, NUM_BUCKETS - 1)
    bucket = np.where(n < max_exact, n, large)
    in_window = (dist >= 0) & (dist < WINDOW)
    return bucket, in_window, sj


def _attention_bias_tables(rel_bias, steps):
    bucket, in_window, sj = _t5_bucket_table(steps)
    onehot = (jnp.asarray(bucket, jnp.int32)[..., None] == jnp.arange(NUM_BUCKETS)).astype(F32)
    bias = jnp.einsum("qsb,bh->hqs", onehot, rel_bias.astype(F32),
                      precision=lax.Precision.HIGHEST)
    tables = []
    for v in range(WINDOW // steps + 1):
        valid = in_window & (sj >= WINDOW - v * steps)
        tables.append(jnp.where(valid[None], bias, NEG_INF))
    return jnp.stack(tables)


def kernel(x, norm_ffn1, ffn1_wg, ffn1_wu, ffn1_wd, norm_mix, norm_ffn2, ffn2_wg, ffn2_wu, ffn2_wd,
           rel_bias, even_w_in, attn_sinks, conv_b_w, conv_b_b, conv_ln_g, conv_ln_b, even_w_out,
           odd_w_in, lru_conv_w, lru_conv_b, gate_a_w, gate_a_b, gate_x_w, gate_x_b, lru_lambda,
           odd_w_out, norm_final):
    bsz, seq, d = x.shape
    depth = norm_ffn1.shape[0]
    bias_tbl = _attention_bias_tables(rel_bias, EVEN_STEPS)
    row = lambda v: v.reshape(1, -1)

    bf16 = lambda w: w.astype(BF16)
    ffn1 = (bf16(ffn1_wg), bf16(ffn1_wu), bf16(ffn1_wd))
    ffn2 = (bf16(ffn2_wg), bf16(ffn2_wu), bf16(ffn2_wd))

    h = x
    for layer in range(depth):
        if layer % 2 == 0:
            e = layer // 2
            h = _ffn_even_call(h, row(norm_ffn1[layer]), *ffn1, layer, row(norm_mix[layer]),
                               even_w_in[e].astype(BF16), attn_sinks[e], bias_tbl, conv_b_w[e],
                               row(conv_b_b[e]), row(conv_ln_g[e]), row(conv_ln_b[e]),
                               even_w_out[e].astype(BF16),
                               batch_major_in=(layer == 0))
        else:
            o = layer // 2
            w_gate = jnp.concatenate([gate_a_w[o], gate_x_w[o]], axis=-1).astype(BF16)
            h = _ffn_odd_call(h, row(norm_ffn1[layer]), *ffn1, layer, row(norm_mix[layer]),
                              odd_w_in[o].astype(BF16), lru_conv_w[o], row(lru_conv_b[o]), w_gate,
                              row(gate_a_b[o]), row(gate_x_b[o]), row(lru_lambda[o]),
                              odd_w_out[o].astype(BF16))
        last = layer == depth - 1
        h = _ffn_call(h, row(norm_ffn2[layer]), *ffn2, layer,
                      final_gain=row(norm_final) if last else None)
    return h
```

```python
import functools
import math

import jax
import jax.numpy as jnp
import numpy as np
from jax import lax
from jax.experimental import pallas as pl
from jax.experimental.pallas import tpu as pltpu

D_MODEL = 1024
D_FF = 2816
HEAD_DIM = 64
N_Q_HEADS = 8
N_KV_HEADS = 2
GROUP = N_Q_HEADS // N_KV_HEADS
WINDOW = 128
Q_WIDTH = N_Q_HEADS * HEAD_DIM
KV_WIDTH = N_KV_HEADS * HEAD_DIM
NUM_BUCKETS = 32
MAX_DISTANCE = 128
CONV_CHANNELS = 512
CONV_WIDTH = 31
LRU_WIDTH = 1024
LRU_HEADS = 8
LRU_BLOCK = 128
LRU_CONV_WIDTH = 4
RG_LRU_C = 8.0
RMS_EPS = 1e-6
LN_EPS = 1e-5
NEG_INF = -1e30

LANES = 128
SUBLANES = 8
MXU_DIM = 256
VMEM_LIMIT_BYTES = 56 * 1024 * 1024
FUSED_VMEM_LIMIT_BYTES = 60 * 1024 * 1024

FF_CHUNK = MXU_DIM
N_FF_CHUNKS = D_FF // FF_CHUNK
FFN_ROWS = 1024
ROW_CHUNK = 64
MIX_BATCH = SUBLANES
STEPS_PER_CHUNK = ROW_CHUNK // MIX_BATCH
EVEN_STEPS = 64
ODD_STEPS = 64
FINAL_STEPS = 64
CONV_HALO = 32
LRU_HALO = LRU_CONV_WIDTH - 1
SCAN_ROWS = 2 * MIX_BATCH

BF16 = jnp.bfloat16
F32 = jnp.float32


def _resident(shape):
    nd = len(shape)
    return pl.BlockSpec(shape, lambda *_: (0,) * nd, pipeline_mode=pl.Buffered(1))


def _sigmoid(x):
    return 0.5 * jnp.tanh(0.5 * x) + 0.5


def _silu_mul(g, u):
    half = 0.5 * g
    return (half + half * jnp.tanh(half)) * u


def _rmsnorm_rows(x, g):
    ms = jnp.mean(x * x, axis=-1, keepdims=True)
    return x * lax.rsqrt(ms + RMS_EPS) * g


def _norm_tile(h_ref, g_ref, hn_scr):
    steps = h_ref.shape[0]
    for i in range(steps // STEPS_PER_CHUNK):
        x = h_ref[i * STEPS_PER_CHUNK:(i + 1) * STEPS_PER_CHUNK].reshape(ROW_CHUNK, D_MODEL)
        hn_scr[i * ROW_CHUNK:(i + 1) * ROW_CHUNK, :] = _rmsnorm_rows(x, g_ref[...]).astype(BF16)


def _ffn_kernel(*refs, final_norm):
    if final_norm:
        (x_ref, xnext_ref, g_ref, wgu_ref, wd_ref, gfin_ref, o_ref,
         xn_scr, a_scr, res_scr) = refs
    else:
        x_ref, xnext_ref, g_ref, wgu_ref, wd_ref, o_ref, xn_scr, a_scr = refs
    rows = xn_scr.shape[1]
    i = pl.program_id(0)
    slot = lax.rem(i, 2)

    def norm_chunk(src_ref, dst_slot, k):
        if final_norm:
            x = src_ref[k * STEPS_PER_CHUNK:(k + 1) * STEPS_PER_CHUNK].reshape(ROW_CHUNK, D_MODEL)
        else:
            x = src_ref[k * ROW_CHUNK:(k + 1) * ROW_CHUNK, :]
        xn_scr[dst_slot, k * ROW_CHUNK:(k + 1) * ROW_CHUNK, :] = (
            _rmsnorm_rows(x, g_ref[...]).astype(BF16))

    n_norm = rows // ROW_CHUNK
    norm_per_dot = pl.cdiv(n_norm, N_FF_CHUNKS)

    @pl.when(i == 0)
    def _():
        for k in range(n_norm):
            norm_chunk(x_ref, 0, k)

    for c in range(N_FF_CHUNKS):
        cols = slice(c * FF_CHUNK, (c + 1) * FF_CHUNK)
        gu = jnp.dot(xn_scr[slot], wgu_ref[:, 2 * c * FF_CHUNK:2 * (c + 1) * FF_CHUNK],
                     preferred_element_type=F32)
        a_scr[:, cols] = _silu_mul(gu[:, :FF_CHUNK], gu[:, FF_CHUNK:]).astype(BF16)
        for k in range(c * norm_per_dot, min((c + 1) * norm_per_dot, n_norm)):
            norm_chunk(xnext_ref, 1 - slot, k)
    if final_norm:
        steps = x_ref.shape[0]
        for half in range(2):
            t = slice(half * steps // 2, (half + 1) * steps // 2)
            r = slice(half * rows // 2, (half + 1) * rows // 2)
            y = jnp.dot(a_scr[r, :], wd_ref[...], preferred_element_type=F32)
            res_scr[t] = x_ref[t] + 0.5 * y.reshape(steps // 2, MIX_BATCH, D_MODEL)
            for b in range(o_ref.shape[0]):
                o_ref[b, t, :] = _rmsnorm_rows(res_scr[t, b, :], gfin_ref[...])
    else:
        y = jnp.dot(a_scr[...], wd_ref[...], preferred_element_type=F32)
        o_ref[...] = x_ref[...] + 0.5 * y


def _layer_weight(w, layer):
    return pl.BlockSpec((None,) + w.shape[1:], lambda *_: (layer, 0, 0),
                        pipeline_mode=pl.Buffered(1))


def _ffn_call(x, gain, wgu, wd, layer, final_gain=None):
    seq, bsz, _ = x.shape
    final_norm = final_gain is not None
    if final_norm:
        tile_rows = FINAL_STEPS * MIX_BATCH
        n_st = seq // FINAL_STEPS
        n_tiles = (bsz // MIX_BATCH) * n_st
        block = (FINAL_STEPS, MIX_BATCH, D_MODEL)
        x_spec = pl.BlockSpec(block, lambda i: (i % n_st, i // n_st, 0))

        def next_map(i):
            j = jnp.minimum(i + 1, n_tiles - 1)
            return (j % n_st, j // n_st, 0)

        xnext_spec = pl.BlockSpec(block, next_map)
        out_shape = jax.ShapeDtypeStruct((bsz, seq, D_MODEL), F32)
        out_spec = pl.BlockSpec((MIX_BATCH, FINAL_STEPS, D_MODEL),
                                lambda i: (i // n_st, i % n_st, 0))
        operands = (x, x, gain, wgu, wd, final_gain)
        extra_specs = [_resident((1, D_MODEL))]
        extra_scratch = [pltpu.VMEM(block, F32)]
    else:
        tile_rows = FFN_ROWS
        x = x.reshape(seq * bsz, D_MODEL)
        n_tiles = x.shape[0] // FFN_ROWS
        x_spec = pl.BlockSpec((FFN_ROWS, D_MODEL), lambda i: (i, 0))
        xnext_spec = pl.BlockSpec((FFN_ROWS, D_MODEL),
                                  lambda i: (jnp.minimum(i + 1, n_tiles - 1), 0))
        out_shape = jax.ShapeDtypeStruct(x.shape, F32)
        out_spec = pl.BlockSpec((FFN_ROWS, D_MODEL), lambda i: (i, 0))
        operands = (x, x, gain, wgu, wd)
        extra_specs = []
        extra_scratch = []
    out = pl.pallas_call(
        functools.partial(_ffn_kernel, final_norm=final_norm),
        out_shape=out_shape,
        grid=(n_tiles,),
        in_specs=[
            x_spec,
            xnext_spec,
            _resident((1, D_MODEL)),
            _layer_weight(wgu, layer),
            _layer_weight(wd, layer),
        ] + extra_specs,
        out_specs=out_spec,
        scratch_shapes=[
            pltpu.VMEM((2, tile_rows, D_MODEL), BF16),
            pltpu.VMEM((tile_rows, D_FF), BF16),
        ] + extra_scratch,
        compiler_params=pltpu.CompilerParams(
            dimension_semantics=("arbitrary",), vmem_limit_bytes=VMEM_LIMIT_BYTES),
        name="ffn_final" if final_norm else "ffn",
    )(*operands)
    return out if final_norm else out.reshape(seq, bsz, D_MODEL)


def _ffn_odd_kernel(x_ref, g1_ref, wgu_ref, wd_ref, gm_ref, win_ref, cw_ref, cb_ref,
                    wgate_ref, gab_ref, gxb_ref, lam_ref, wout_ref, o_ref,
                    h1_new, h1_prev, xn_scr, a_scr, hn_scr, ug_scr, xpad_scr, rec_scr, gpre_scr,
                    state_scr, *, tiles_per_seq):
    steps = x_ref.shape[0]
    rows = steps * MIX_BATCH
    halo_rows = LRU_HALO * MIX_BATCH
    i = pl.program_id(0)

    @pl.when(i == 0)
    def _():
        h1_prev[...] = jnp.zeros(h1_prev.shape, F32)

    @pl.when(jnp.logical_or(i == 0, lax.rem(i + tiles_per_seq - 1, tiles_per_seq) == 0))
    def _():
        xpad_scr[0:halo_rows, :] = jnp.zeros((halo_rows, LRU_WIDTH), F32)
        state_scr[...] = jnp.zeros((MIX_BATCH, LRU_WIDTH), F32)

    def ffn_up(c0, c1):
        for c in range(c0, c1):
            cols = slice(c * FF_CHUNK, (c + 1) * FF_CHUNK)
            gu = jnp.dot(xn_scr[...], wgu_ref[:, 2 * c * FF_CHUNK:2 * (c + 1) * FF_CHUNK],
                         preferred_element_type=F32)
            a_scr[:, cols] = _silu_mul(gu[:, :FF_CHUNK], gu[:, FF_CHUNK:]).astype(BF16)

    for k in range(rows // ROW_CHUNK):
        x = x_ref[k * STEPS_PER_CHUNK:(k + 1) * STEPS_PER_CHUNK].reshape(ROW_CHUNK, D_MODEL)
        xn_scr[k * ROW_CHUNK:(k + 1) * ROW_CHUNK, :] = _rmsnorm_rows(x, g1_ref[...]).astype(BF16)
    for k in range(rows // ROW_CHUNK):
        r = slice(k * ROW_CHUNK, (k + 1) * ROW_CHUNK)
        hn_scr[r, :] = _rmsnorm_rows(h1_prev[r, :], gm_ref[...]).astype(BF16)
    ffn_up(0, 2)
    ug_scr[...] = jnp.dot(hn_scr[...], win_ref[:, 0:LRU_WIDTH], preferred_element_type=F32)
    xpad_scr[halo_rows:, :] = jnp.dot(hn_scr[...], win_ref[:, LRU_WIDTH:],
                                      preferred_element_type=F32)
    ffn_up(2, 7)

    for k in range(rows // ROW_CHUNK):
        acc = jnp.zeros((ROW_CHUNK, LRU_WIDTH), F32)
        for tap in range(LRU_CONV_WIDTH):
            r0 = (k * STEPS_PER_CHUNK + tap) * MIX_BATCH
            acc = acc + xpad_scr[r0:r0 + ROW_CHUNK, :] * cw_ref[tap:tap + 1, :]
        rec_scr[k * ROW_CHUNK:(k + 1) * ROW_CHUNK, :] = acc + cb_ref[...]

    for hd in range(LRU_HEADS):
        cs = slice(hd * LRU_BLOCK, (hd + 1) * LRU_BLOCK)
        pre = jnp.dot(rec_scr[:, cs].astype(BF16), wgate_ref[hd], preferred_element_type=F32)
        gpre_scr[:, cs] = pre[:, :LRU_BLOCK]
        gpre_scr[:, LRU_WIDTH + hd * LRU_BLOCK:LRU_WIDTH + (hd + 1) * LRU_BLOCK] = pre[:, LRU_BLOCK:]

    ffn_up(7, N_FF_CHUNKS)
    y = jnp.dot(a_scr[...], wd_ref[...], preferred_element_type=F32)
    h1_new[...] = x_ref[...].reshape(rows, D_MODEL) + 0.5 * y

    lam = lam_ref[...]
    log_sig_lam = -(jnp.maximum(-lam, 0.0) + jnp.log1p(jnp.exp(-jnp.abs(lam))))
    c_log_sig_lam = RG_LRU_C * log_sig_lam

    hprev = state_scr[...]
    for k in range(rows // SCAN_ROWS):
        r = slice(k * SCAN_ROWS, (k + 1) * SCAN_ROWS)
        x = rec_scr[r, :]
        rg = _sigmoid(gpre_scr[r, 0:LRU_WIDTH] + gab_ref[...])
        ig = _sigmoid(gpre_scr[r, LRU_WIDTH:] + gxb_ref[...])
        log_a = rg * c_log_sig_lam
        a = jnp.exp(log_a)
        bx = jnp.sqrt(-jnp.tanh(log_a) * (a * a + 1.0)) * (ig * x)
        hs = []
        for t in range(SCAN_ROWS // MIX_BATCH):
            rt = slice(t * MIX_BATCH, (t + 1) * MIX_BATCH)
            hprev = a[rt, :] * hprev + bx[rt, :]
            hs.append(hprev)
        hs = jnp.concatenate(hs, axis=0)
        hn_scr[r, :] = (jax.nn.gelu(ug_scr[r, :]) * hs).astype(BF16)
    state_scr[...] = hprev

    mixed = jnp.dot(hn_scr[...], wout_ref[...], preferred_element_type=F32)
    o_ref[...] = (h1_prev[...] + mixed).reshape(steps, MIX_BATCH, D_MODEL)
    xpad_scr[0:halo_rows, :] = xpad_scr[rows:rows + halo_rows, :]

    h1_prev[...] = h1_new[...]


def _ffn_odd_call(h, gain1, wgu, wd, layer, gain_mix, w_in, conv_w, conv_b, w_gate, ga_b, gx_b,
                  lam, w_out):
    seq, bsz, _ = h.shape
    rows = ODD_STEPS * MIX_BATCH
    tiles_per_seq = seq // ODD_STEPS
    n_tiles = (bsz // MIX_BATCH) * tiles_per_seq
    block = (ODD_STEPS, MIX_BATCH, D_MODEL)

    def in_map(i):
        j = jnp.minimum(i, n_tiles - 1)
        return (j % tiles_per_seq, j // tiles_per_seq, 0)

    def out_map(i):
        j = jnp.maximum(i - 1, 0)
        return (j % tiles_per_seq, j // tiles_per_seq, 0)

    return pl.pallas_call(
        functools.partial(_ffn_odd_kernel, tiles_per_seq=tiles_per_seq),
        out_shape=jax.ShapeDtypeStruct(h.shape, F32),
        grid=(n_tiles + 1,),
        in_specs=[
            pl.BlockSpec(block, in_map),
            _resident((1, D_MODEL)),
            _layer_weight(wgu, layer),
            _layer_weight(wd, layer),
            _resident((1, D_MODEL)),
            _resident(w_in.shape),
            _resident(conv_w.shape),
            _resident(conv_b.shape),
            _resident(w_gate.shape),
            _resident(ga_b.shape),
            _resident(gx_b.shape),
            _resident(lam.shape),
            _resident(w_out.shape),
        ],
        out_specs=pl.BlockSpec(block, out_map),
        scratch_shapes=[
            pltpu.VMEM((rows, D_MODEL), F32),
            pltpu.VMEM((rows, D_MODEL), F32),
            pltpu.VMEM((rows, D_MODEL), BF16),
            pltpu.VMEM((rows, D_FF), BF16),
            pltpu.VMEM((rows, D_MODEL), BF16),
            pltpu.VMEM((rows, LRU_WIDTH), F32),
            pltpu.VMEM(((LRU_HALO + ODD_STEPS) * MIX_BATCH, LRU_WIDTH), F32),
            pltpu.VMEM((rows, LRU_WIDTH), F32),
            pltpu.VMEM((rows, 2 * LRU_WIDTH), F32),
            pltpu.VMEM((MIX_BATCH, LRU_WIDTH), F32),
        ],
        compiler_params=pltpu.CompilerParams(
            dimension_semantics=("arbitrary",), vmem_limit_bytes=FUSED_VMEM_LIMIT_BYTES),
        name="ffn_odd",
    )(h, gain1, wgu, wd, gain_mix, w_in, conv_w, conv_b, w_gate, ga_b, gx_b, lam, w_out)


def _ffn_even_kernel(sinks_ref, x_ref, g1_ref, wgu_ref, wd_ref, gm_ref, win_ref, bias_ref,
                     cw_ref, cb_ref, lng_ref, lnb_ref, wout_ref, o_ref,
                     h1_new, h1_prev, xn_scr, a_scr, hn_scr, qkv_scr, qseq_scr, kvprev_scr, oseq_scr,
                     attn_scr, xpad_scr, *, tiles_per_seq, batch_major_in):
    steps = o_ref.shape[0]
    rows = steps * MIX_BATCH
    halo_rows = CONV_HALO * MIX_BATCH
    n_keys = WINDOW + steps
    i = pl.program_id(0)
    seq_tile = lax.rem(i + tiles_per_seq - 1, tiles_per_seq)

    @pl.when(i == 0)
    def _():
        h1_prev[...] = jnp.zeros(h1_prev.shape, F32)

    @pl.when(jnp.logical_or(i == 0, seq_tile == 0))
    def _():
        kvprev_scr[...] = jnp.zeros(kvprev_scr.shape, F32)
        xpad_scr[0:halo_rows, :] = jnp.zeros((halo_rows, CONV_CHANNELS), F32)

    def ffn_norm():
        for k in range(rows // ROW_CHUNK):
            if batch_major_in:
                r0 = k * ROW_CHUNK
                x = x_ref[r0 // steps, r0 % steps:r0 % steps + ROW_CHUNK, :]
            else:
                x = x_ref[k * STEPS_PER_CHUNK:(k + 1) * STEPS_PER_CHUNK].reshape(ROW_CHUNK,
                                                                                 D_MODEL)
            xn_scr[k * ROW_CHUNK:(k + 1) * ROW_CHUNK, :] = (
                _rmsnorm_rows(x, g1_ref[...]).astype(BF16))

    def ffn_up(c0, c1):
        for c in range(c0, c1):
            cols = slice(c * FF_CHUNK, (c + 1) * FF_CHUNK)
            gu = jnp.dot(xn_scr[...], wgu_ref[:, 2 * c * FF_CHUNK:2 * (c + 1) * FF_CHUNK],
                         preferred_element_type=F32)
            a_scr[:, cols] = _silu_mul(gu[:, :FF_CHUNK], gu[:, FF_CHUNK:]).astype(BF16)

    def ffn_down():
        y = jnp.dot(a_scr[...], wd_ref[...], preferred_element_type=F32)
        if batch_major_in:
            for b in range(MIX_BATCH):
                h1_new[:, b, :] = x_ref[b] + 0.5 * y[b * steps:(b + 1) * steps, :]
        else:
            h1_new[...] = x_ref[...] + 0.5 * y.reshape(steps, MIX_BATCH, D_MODEL)

    ffn_norm()
    _norm_tile(h1_prev, gm_ref, hn_scr)
    ffn_up(0, 2)

    for j in range((Q_WIDTH + 2 * KV_WIDTH) // MXU_DIM):
        res = jnp.dot(hn_scr[...], win_ref[:, j * MXU_DIM:(j + 1) * MXU_DIM],
                      preferred_element_type=F32)
        qkv_scr[2 * j] = res[:, :LANES]
        qkv_scr[2 * j + 1] = res[:, LANES:]
    o_g = Q_WIDTH + 2 * KV_WIDTH
    glu_a = jnp.dot(hn_scr[...], win_ref[:, o_g:o_g + CONV_CHANNELS], preferred_element_type=F32)
    glu_b = jnp.dot(hn_scr[...], win_ref[:, o_g + CONV_CHANNELS:], preferred_element_type=F32)
    xpad_scr[halo_rows:, :] = glu_a * _sigmoid(glu_b)

    k_slab = Q_WIDTH // LANES
    v_slab = k_slab + 1
    lane = lax.broadcasted_iota(jnp.int32, (n_keys, LANES), 1)
    qlane = lax.broadcasted_iota(jnp.int32, (steps, LANES), 1)
    scale = 1.0 / math.sqrt(HEAD_DIM)
    variant = jnp.minimum(seq_tile, WINDOW // steps)
    for b in range(MIX_BATCH):
        for j in range(qkv_scr.shape[0]):
            qseq_scr[b, j] = qkv_scr[j, pl.ds(b, steps, stride=MIX_BATCH), :]
    score_blocks = []
    for b in range(MIX_BATCH):
        k_cat = jnp.concatenate([kvprev_scr[b, 0], qseq_scr[b, k_slab]], axis=0)
        v_cat = jnp.concatenate([kvprev_scr[b, 1], qseq_scr[b, v_slab]], axis=0)
        kvprev_scr[b, 0] = k_cat[steps:, :]
        kvprev_scr[b, 1] = v_cat[steps:, :]
        k_sw = pltpu.roll(k_cat, HEAD_DIM, axis=1)
        v_sw = pltpu.roll(v_cat, HEAD_DIM, axis=1)
        k_dup = (jnp.where(lane < HEAD_DIM, k_cat, k_sw).astype(BF16),
                 jnp.where(lane < HEAD_DIM, k_sw, k_cat).astype(BF16))
        v_dup = (jnp.where(lane < HEAD_DIM, v_cat, v_sw).astype(BF16),
                 jnp.where(lane < HEAD_DIM, v_sw, v_cat).astype(BF16))
        for c in range(Q_WIDTH // LANES):
            kvh = (2 * c) // GROUP
            qcol = qseq_scr[b, c] * scale
            for half in range(2):
                keep = (qlane < HEAD_DIM) if half == 0 else (qlane >= HEAD_DIM)
                q = jnp.where(keep, qcol, 0.0).astype(BF16)
                sc = lax.dot_general(q, k_dup[kvh], (((1,), (1,)), ((), ())),
                                     preferred_element_type=F32)
                score_blocks.append((2 * c + half, sc, v_dup[kvh]))
    ffn_up(2, 9)
    prob_blocks = []
    for head, sc, v in score_blocks:
        sc = sc + bias_ref[variant, head]
        sink = sinks_ref[head]
        m = jnp.maximum(jnp.max(sc, axis=-1, keepdims=True), sink)
        e = jnp.exp(sc - m)
        den = jnp.sum(e, axis=-1, keepdims=True) + jnp.exp(sink - m)
        prob_blocks.append(((e / den).astype(BF16), v))
    outs = [jnp.dot(p, v, preferred_element_type=F32) for p, v in prob_blocks]
    ffn_up(9, N_FF_CHUNKS)
    ffn_down()
    for b in range(MIX_BATCH):
        for c in range(Q_WIDTH // LANES):
            n = (b * (Q_WIDTH // LANES) + c) * 2
            oseq_scr[b, c] = jnp.where(qlane < HEAD_DIM, outs[n], outs[n + 1])
    for b in range(MIX_BATCH):
        for c in range(Q_WIDTH // LANES):
            attn_scr[c, pl.ds(b, steps, stride=MIX_BATCH), :] = oseq_scr[b, c]
    for c in range(Q_WIDTH // LANES):
        hn_scr[:, c * LANES:(c + 1) * LANES] = attn_scr[c].astype(BF16)

    first_step = CONV_HALO - (CONV_WIDTH - 1)
    for k in range(rows // ROW_CHUNK):
        acc = jnp.zeros((ROW_CHUNK, CONV_CHANNELS), F32)
        for tap in range(CONV_WIDTH):
            r0 = (k * STEPS_PER_CHUNK + first_step + tap) * MIX_BATCH
            acc = acc + xpad_scr[r0:r0 + ROW_CHUNK, :] * cw_ref[tap:tap + 1, :]
        y = acc + cb_ref[...]
        mu = jnp.mean(y, axis=-1, keepdims=True)
        yc = y - mu
        var = jnp.mean(yc * yc, axis=-1, keepdims=True)
        z = yc * lax.rsqrt(var + LN_EPS) * lng_ref[...] + lnb_ref[...]
        hn_scr[k * ROW_CHUNK:(k + 1) * ROW_CHUNK, Q_WIDTH:] = (z * _sigmoid(z)).astype(BF16)

    mixed = jnp.dot(hn_scr[...], wout_ref[...], preferred_element_type=F32)
    o_ref[...] = h1_prev[...] + mixed.reshape(steps, MIX_BATCH, D_MODEL)
    xpad_scr[0:halo_rows, :] = xpad_scr[rows:rows + halo_rows, :]

    h1_prev[...] = h1_new[...]


def _ffn_even_call(x, gain1, wgu, wd, layer, gain_mix, w_in, sinks, bias_tbl, conv_w, conv_b,
                   ln_g, ln_b, w_out, batch_major_in=False):
    if batch_major_in:
        bsz, seq, _ = x.shape
    else:
        seq, bsz, _ = x.shape
    rows = EVEN_STEPS * MIX_BATCH
    tiles_per_seq = seq // EVEN_STEPS
    n_tiles = (bsz // MIX_BATCH) * tiles_per_seq
    block = (EVEN_STEPS, MIX_BATCH, D_MODEL)

    def in_map(i):
        j = jnp.minimum(i, n_tiles - 1)
        if batch_major_in:
            return (j // tiles_per_seq, j % tiles_per_seq, 0)
        return (j % tiles_per_seq, j // tiles_per_seq, 0)

    def out_map(i):
        j = jnp.maximum(i - 1, 0)
        return (j % tiles_per_seq, j // tiles_per_seq, 0)

    in_block = (MIX_BATCH, EVEN_STEPS, D_MODEL) if batch_major_in else block
    return pl.pallas_call(
        functools.partial(_ffn_even_kernel, tiles_per_seq=tiles_per_seq,
                          batch_major_in=batch_major_in),
        out_shape=jax.ShapeDtypeStruct((seq, bsz, D_MODEL), F32),
        grid=(n_tiles + 1,),
        in_specs=[
            pl.BlockSpec(memory_space=pltpu.SMEM),
            pl.BlockSpec(in_block, in_map),
            _resident((1, D_MODEL)),
            _layer_weight(wgu, layer),
            _layer_weight(wd, layer),
            _resident((1, D_MODEL)),
            _resident(w_in.shape),
            _resident(bias_tbl.shape),
            _resident(conv_w.shape),
            _resident(conv_b.shape),
            _resident(ln_g.shape),
            _resident(ln_b.shape),
            _resident(w_out.shape),
        ],
        out_specs=pl.BlockSpec(block, out_map),
        scratch_shapes=[
            pltpu.VMEM(block, F32),
            pltpu.VMEM(block, F32),
            pltpu.VMEM((rows, D_MODEL), BF16),
            pltpu.VMEM((rows, D_FF), BF16),
            pltpu.VMEM((rows, D_MODEL), BF16),
            pltpu.VMEM(((Q_WIDTH + 2 * KV_WIDTH) // LANES, rows, LANES), F32),
            pltpu.VMEM((MIX_BATCH, (Q_WIDTH + 2 * KV_WIDTH) // LANES, EVEN_STEPS, LANES), F32),
            pltpu.VMEM((MIX_BATCH, 2, WINDOW, LANES), F32),
            pltpu.VMEM((MIX_BATCH, Q_WIDTH // LANES, EVEN_STEPS, LANES), F32),
            pltpu.VMEM((Q_WIDTH // LANES, rows, LANES), F32),
            pltpu.VMEM(((CONV_HALO + EVEN_STEPS) * MIX_BATCH, CONV_CHANNELS), F32),
        ],
        compiler_params=pltpu.CompilerParams(
            dimension_semantics=("arbitrary",), vmem_limit_bytes=FUSED_VMEM_LIMIT_BYTES),
        name="ffn_even",
    )(sinks, x, gain1, wgu, wd, gain_mix, w_in, bias_tbl, conv_w, conv_b, ln_g, ln_b, w_out)


def _t5_bucket_table(steps):
    qi = np.arange(steps)[:, None]
    sj = np.arange(WINDOW + steps)[None, :]
    dist = qi + WINDOW - sj
    n = np.maximum(dist, 0)
    max_exact = NUM_BUCKETS // 2
    nf = np.maximum(n, max_exact).astype(np.float32)
    large = max_exact + (np.log(nf / max_exact) / math.log(MAX_DISTANCE / max_exact)
                         * (NUM_BUCKETS - max_exact)).astype(np.int32)
    large = np.minimum(large, NUM_BUCKETS - 1)
    bucket = np.where(n < max_exact, n, large)
    in_window = (dist >= 0) & (dist < WINDOW)
    return bucket, in_window, sj


def _attention_bias_tables(rel_bias, steps):
    bucket, in_window, sj = _t5_bucket_table(steps)
    onehot = (jnp.asarray(bucket, jnp.int32)[..., None] == jnp.arange(NUM_BUCKETS)).astype(F32)
    bias = jnp.einsum("qsb,bh->hqs", onehot, rel_bias.astype(F32),
                      precision=lax.Precision.HIGHEST)
    tables = []
    for v in range(WINDOW // steps + 1):
        valid = in_window & (sj >= WINDOW - v * steps)
        tables.append(jnp.where(valid[None], bias, NEG_INF))
    return jnp.stack(tables)


def _pack_gate_up(wg, wu):
    layers = wg.shape[0]
    parts = [w.astype(BF16).reshape(layers, D_MODEL, N_FF_CHUNKS, FF_CHUNK) for w in (wg, wu)]
    return jnp.concatenate(parts, axis=-1).reshape(layers, D_MODEL, 2 * D_FF)


def kernel(x, norm_ffn1, ffn1_wg, ffn1_wu, ffn1_wd, norm_mix, norm_ffn2, ffn2_wg, ffn2_wu, ffn2_wd,
           rel_bias, even_w_in, attn_sinks, conv_b_w, conv_b_b, conv_ln_g, conv_ln_b, even_w_out,
           odd_w_in, lru_conv_w, lru_conv_b, gate_a_w, gate_a_b, gate_x_w, gate_x_b, lru_lambda,
           odd_w_out, norm_final):
    bsz, seq, d = x.shape
    depth = norm_ffn1.shape[0]
    bias_tbl = _attention_bias_tables(rel_bias, EVEN_STEPS)
    row = lambda v: v.reshape(1, -1)

    bf16 = lambda w: w.astype(BF16)
    ffn1 = (_pack_gate_up(ffn1_wg, ffn1_wu), bf16(ffn1_wd))
    ffn2 = (_pack_gate_up(ffn2_wg, ffn2_wu), bf16(ffn2_wd))

    h = x
    for layer in range(depth):
        if layer % 2 == 0:
            e = layer // 2
            h = _ffn_even_call(h, row(norm_ffn1[layer]), *ffn1, layer, row(norm_mix[layer]),
                               even_w_in[e].astype(BF16), attn_sinks[e], bias_tbl, conv_b_w[e],
                               row(conv_b_b[e]), row(conv_ln_g[e]), row(conv_ln_b[e]),
                               even_w_out[e].astype(BF16),
                               batch_major_in=(layer == 0))
        else:
            o = layer // 2
            w_gate = jnp.concatenate([gate_a_w[o], gate_x_w[o]], axis=-1).astype(BF16)
            h = _ffn_odd_call(h, row(norm_ffn1[layer]), *ffn1, layer, row(norm_mix[layer]),
                              odd_w_in[o].astype(BF16), lru_conv_w[o], row(lru_conv_b[o]), w_gate,
                              row(gate_a_b[o]), row(gate_x_b[o]), row(lru_lambda[o]),
                              odd_w_out[o].astype(BF16))
        last = layer == depth - 1
        h = _ffn_call(h, row(norm_ffn2[layer]), *ffn2, layer,
                      final_gain=row(norm_final) if last else None)
    return h
```

```python
import functools
import math

import jax
import jax.numpy as jnp
import numpy as np
from jax import lax
from jax.experimental import pallas as pl
from jax.experimental.pallas import tpu as pltpu

D_MODEL = 1024
D_FF = 2816
HEAD_DIM = 64
N_Q_HEADS = 8
N_KV_HEADS = 2
GROUP = N_Q_HEADS // N_KV_HEADS
WINDOW = 128
Q_WIDTH = N_Q_HEADS * HEAD_DIM
KV_WIDTH = N_KV_HEADS * HEAD_DIM
NUM_BUCKETS = 32
MAX_DISTANCE = 128
CONV_CHANNELS = 512
CONV_WIDTH = 31
LRU_WIDTH = 1024
LRU_HEADS = 8
LRU_BLOCK = 128
LRU_CONV_WIDTH = 4
RG_LRU_C = 8.0
RMS_EPS = 1e-6
LN_EPS = 1e-5
NEG_INF = -1e30

LANES = 128
SUBLANES = 8
MXU_DIM = 256
VMEM_LIMIT_BYTES = 56 * 1024 * 1024
FUSED_VMEM_LIMIT_BYTES = 60 * 1024 * 1024

FF_CHUNK = MXU_DIM
N_FF_CHUNKS = D_FF // FF_CHUNK
FFN_ROWS = 1024
ROW_CHUNK = 64
MIX_BATCH = SUBLANES
STEPS_PER_CHUNK = ROW_CHUNK // MIX_BATCH
EVEN_STEPS = 64
ODD_STEPS = 64
FINAL_STEPS = 64
CONV_HALO = 32
LRU_HALO = LRU_CONV_WIDTH - 1
SCAN_ROWS = 2 * MIX_BATCH

BF16 = jnp.bfloat16
F32 = jnp.float32


def _resident(shape):
    nd = len(shape)
    return pl.BlockSpec(shape, lambda *_: (0,) * nd, pipeline_mode=pl.Buffered(1))


def _sigmoid(x):
    return 0.5 * jnp.tanh(0.5 * x) + 0.5


def _silu_mul(g, u):
    half = 0.5 * g
    return (half + half * jnp.tanh(half)) * u


def _rmsnorm_rows(x, g):
    ms = jnp.mean(x * x, axis=-1, keepdims=True)
    return x * lax.rsqrt(ms + RMS_EPS) * g


def _norm_tile(h_ref, g_ref, hn_scr):
    steps = h_ref.shape[0]
    for i in range(steps // STEPS_PER_CHUNK):
        x = h_ref[i * STEPS_PER_CHUNK:(i + 1) * STEPS_PER_CHUNK].reshape(ROW_CHUNK, D_MODEL)
        hn_scr[i * ROW_CHUNK:(i + 1) * ROW_CHUNK, :] = _rmsnorm_rows(x, g_ref[...]).astype(BF16)


def _ffn_kernel(*refs, final_norm):
    if final_norm:
        (x_ref, xnext_ref, g_ref, wg_ref, wu_ref, wd_ref, gfin_ref, o_ref,
         xn_scr, a_scr, res_scr) = refs
    else:
        x_ref, xnext_ref, g_ref, wg_ref, wu_ref, wd_ref, o_ref, xn_scr, a_scr = refs
    rows = xn_scr.shape[1]
    i = pl.program_id(0)
    slot = lax.rem(i, 2)

    def norm_chunk(src_ref, dst_slot, k):
        if final_norm:
            x = src_ref[k * STEPS_PER_CHUNK:(k + 1) * STEPS_PER_CHUNK].reshape(ROW_CHUNK, D_MODEL)
        else:
            x = src_ref[k * ROW_CHUNK:(k + 1) * ROW_CHUNK, :]
        xn_scr[dst_slot, k * ROW_CHUNK:(k + 1) * ROW_CHUNK, :] = (
            _rmsnorm_rows(x, g_ref[...]).astype(BF16))

    n_norm = rows // ROW_CHUNK
    norm_per_dot = pl.cdiv(n_norm, N_FF_CHUNKS)

    @pl.when(i == 0)
    def _():
        for k in range(n_norm):
            norm_chunk(x_ref, 0, k)

    for c in range(N_FF_CHUNKS):
        cols = slice(c * FF_CHUNK, (c + 1) * FF_CHUNK)
        g = jnp.dot(xn_scr[slot], wg_ref[:, cols], preferred_element_type=F32)
        u = jnp.dot(xn_scr[slot], wu_ref[:, cols], preferred_element_type=F32)
        a_scr[:, cols] = _silu_mul(g, u).astype(BF16)
        for k in range(c * norm_per_dot, min((c + 1) * norm_per_dot, n_norm)):
            norm_chunk(xnext_ref, 1 - slot, k)
    y = jnp.dot(a_scr[...], wd_ref[...], preferred_element_type=F32)
    if final_norm:
        res_scr[...] = x_ref[...] + 0.5 * y.reshape(res_scr.shape)
        for b in range(o_ref.shape[0]):
            o_ref[b] = _rmsnorm_rows(res_scr[:, b, :], gfin_ref[...])
    else:
        o_ref[...] = x_ref[...] + 0.5 * y


def _layer_weight(w, layer):
    return pl.BlockSpec((None,) + w.shape[1:], lambda *_: (layer, 0, 0),
                        pipeline_mode=pl.Buffered(1))


def _ffn_call(x, gain, wg, wu, wd, layer, final_gain=None):
    seq, bsz, _ = x.shape
    final_norm = final_gain is not None
    if final_norm:
        tile_rows = FINAL_STEPS * MIX_BATCH
        n_st = seq // FINAL_STEPS
        n_tiles = (bsz // MIX_BATCH) * n_st
        block = (FINAL_STEPS, MIX_BATCH, D_MODEL)
        x_spec = pl.BlockSpec(block, lambda i: (i % n_st, i // n_st, 0))

        def next_map(i):
            j = jnp.minimum(i + 1, n_tiles - 1)
            return (j % n_st, j // n_st, 0)

        xnext_spec = pl.BlockSpec(block, next_map)
        out_shape = jax.ShapeDtypeStruct((bsz, seq, D_MODEL), F32)
        out_spec = pl.BlockSpec((MIX_BATCH, FINAL_STEPS, D_MODEL),
                                lambda i: (i // n_st, i % n_st, 0))
        operands = (x, x, gain, wg, wu, wd, final_gain)
        extra_specs = [_resident((1, D_MODEL))]
        extra_scratch = [pltpu.VMEM(block, F32)]
    else:
        tile_rows = FFN_ROWS
        x = x.reshape(seq * bsz, D_MODEL)
        n_tiles = x.shape[0] // FFN_ROWS
        x_spec = pl.BlockSpec((FFN_ROWS, D_MODEL), lambda i: (i, 0))
        xnext_spec = pl.BlockSpec((FFN_ROWS, D_MODEL),
                                  lambda i: (jnp.minimum(i + 1, n_tiles - 1), 0))
        out_shape = jax.ShapeDtypeStruct(x.shape, F32)
        out_spec = pl.BlockSpec((FFN_ROWS, D_MODEL), lambda i: (i, 0))
        operands = (x, x, gain, wg, wu, wd)
        extra_specs = []
        extra_scratch = []
    out = pl.pallas_call(
        functools.partial(_ffn_kernel, final_norm=final_norm),
        out_shape=out_shape,
        grid=(n_tiles,),
        in_specs=[
            x_spec,
            xnext_spec,
            _resident((1, D_MODEL)),
            _layer_weight(wg, layer),
            _layer_weight(wu, layer),
            _layer_weight(wd, layer),
        ] + extra_specs,
        out_specs=out_spec,
        scratch_shapes=[
            pltpu.VMEM((2, tile_rows, D_MODEL), BF16),
            pltpu.VMEM((tile_rows, D_FF), BF16),
        ] + extra_scratch,
        compiler_params=pltpu.CompilerParams(
            dimension_semantics=("arbitrary",), vmem_limit_bytes=VMEM_LIMIT_BYTES),
        name="ffn_final" if final_norm else "ffn",
    )(*operands)
    return out if final_norm else out.reshape(seq, bsz, D_MODEL)


def _ffn_odd_kernel(x_ref, g1_ref, wg_ref, wu_ref, wd_ref, gm_ref, win_ref, cw_ref, cb_ref,
                    wgate_ref, gab_ref, gxb_ref, lam_ref, wout_ref, o_ref,
                    h1_new, h1_prev, xn_scr, a_scr, hn_scr, ug_scr, xpad_scr, rec_scr, gpre_scr,
                    state_scr, *, tiles_per_seq):
    steps = x_ref.shape[0]
    rows = steps * MIX_BATCH
    halo_rows = LRU_HALO * MIX_BATCH
    i = pl.program_id(0)

    @pl.when(i == 0)
    def _():
        h1_prev[...] = jnp.zeros(h1_prev.shape, F32)

    @pl.when(jnp.logical_or(i == 0, lax.rem(i + tiles_per_seq - 1, tiles_per_seq) == 0))
    def _():
        xpad_scr[0:halo_rows, :] = jnp.zeros((halo_rows, LRU_WIDTH), F32)
        state_scr[...] = jnp.zeros((MIX_BATCH, LRU_WIDTH), F32)

    def ffn_up(c0, c1):
        for c in range(c0, c1):
            cols = slice(c * FF_CHUNK, (c + 1) * FF_CHUNK)
            g = jnp.dot(xn_scr[...], wg_ref[:, cols], preferred_element_type=F32)
            u = jnp.dot(xn_scr[...], wu_ref[:, cols], preferred_element_type=F32)
            a_scr[:, cols] = _silu_mul(g, u).astype(BF16)

    for k in range(rows // ROW_CHUNK):
        x = x_ref[k * STEPS_PER_CHUNK:(k + 1) * STEPS_PER_CHUNK].reshape(ROW_CHUNK, D_MODEL)
        xn_scr[k * ROW_CHUNK:(k + 1) * ROW_CHUNK, :] = _rmsnorm_rows(x, g1_ref[...]).astype(BF16)
    for k in range(rows // ROW_CHUNK):
        r = slice(k * ROW_CHUNK, (k + 1) * ROW_CHUNK)
        hn_scr[r, :] = _rmsnorm_rows(h1_prev[r, :], gm_ref[...]).astype(BF16)
    ffn_up(0, 2)
    ug_scr[...] = jnp.dot(hn_scr[...], win_ref[:, 0:LRU_WIDTH], preferred_element_type=F32)
    xpad_scr[halo_rows:, :] = jnp.dot(hn_scr[...], win_ref[:, LRU_WIDTH:],
                                      preferred_element_type=F32)
    ffn_up(2, 7)

    for k in range(rows // ROW_CHUNK):
        acc = jnp.zeros((ROW_CHUNK, LRU_WIDTH), F32)
        for tap in range(LRU_CONV_WIDTH):
            r0 = (k * STEPS_PER_CHUNK + tap) * MIX_BATCH
            acc = acc + xpad_scr[r0:r0 + ROW_CHUNK, :] * cw_ref[tap:tap + 1, :]
        rec_scr[k * ROW_CHUNK:(k + 1) * ROW_CHUNK, :] = acc + cb_ref[...]

    for hd in range(LRU_HEADS):
        cs = slice(hd * LRU_BLOCK, (hd + 1) * LRU_BLOCK)
        pre = jnp.dot(rec_scr[:, cs].astype(BF16), wgate_ref[hd], preferred_element_type=F32)
        gpre_scr[:, cs] = pre[:, :LRU_BLOCK]
        gpre_scr[:, LRU_WIDTH + hd * LRU_BLOCK:LRU_WIDTH + (hd + 1) * LRU_BLOCK] = pre[:, LRU_BLOCK:]

    ffn_up(7, N_FF_CHUNKS)
    y = jnp.dot(a_scr[...], wd_ref[...], preferred_element_type=F32)
    h1_new[...] = x_ref[...].reshape(rows, D_MODEL) + 0.5 * y

    lam = lam_ref[...]
    log_sig_lam = -(jnp.maximum(-lam, 0.0) + jnp.log1p(jnp.exp(-jnp.abs(lam))))
    c_log_sig_lam = RG_LRU_C * log_sig_lam

    hprev = state_scr[...]
    for k in range(rows // SCAN_ROWS):
        r = slice(k * SCAN_ROWS, (k + 1) * SCAN_ROWS)
        x = rec_scr[r, :]
        rg = _sigmoid(gpre_scr[r, 0:LRU_WIDTH] + gab_ref[...])
        ig = _sigmoid(gpre_scr[r, LRU_WIDTH:] + gxb_ref[...])
        log_a = rg * c_log_sig_lam
        a = jnp.exp(log_a)
        bx = jnp.sqrt(-jnp.tanh(log_a) * (a * a + 1.0)) * (ig * x)
        hs = []
        for t in range(SCAN_ROWS // MIX_BATCH):
            rt = slice(t * MIX_BATCH, (t + 1) * MIX_BATCH)
            hprev = a[rt, :] * hprev + bx[rt, :]
            hs.append(hprev)
        hs = jnp.concatenate(hs, axis=0)
        hn_scr[r, :] = (jax.nn.gelu(ug_scr[r, :]) * hs).astype(BF16)
    state_scr[...] = hprev

    mixed = jnp.dot(hn_scr[...], wout_ref[...], preferred_element_type=F32)
    o_ref[...] = (h1_prev[...] + mixed).reshape(steps, MIX_BATCH, D_MODEL)
    xpad_scr[0:halo_rows, :] = xpad_scr[rows:rows + halo_rows, :]

    h1_prev[...] = h1_new[...]


def _ffn_odd_call(h, gain1, wg, wu, wd, layer, gain_mix, w_in, conv_w, conv_b, w_gate, ga_b, gx_b,
                  lam, w_out):
    seq, bsz, _ = h.shape
    rows = ODD_STEPS * MIX_BATCH
    tiles_per_seq = seq // ODD_STEPS
    n_tiles = (bsz // MIX_BATCH) * tiles_per_seq
    block = (ODD_STEPS, MIX_BATCH, D_MODEL)

    def in_map(i):
        j = jnp.minimum(i, n_tiles - 1)
        return (j % tiles_per_seq, j // tiles_per_seq, 0)

    def out_map(i):
        j = jnp.maximum(i - 1, 0)
        return (j % tiles_per_seq, j // tiles_per_seq, 0)

    return pl.pallas_call(
        functools.partial(_ffn_odd_kernel, tiles_per_seq=tiles_per_seq),
        out_shape=jax.ShapeDtypeStruct(h.shape, F32),
        grid=(n_tiles + 1,),
        in_specs=[
            pl.BlockSpec(block, in_map),
            _resident((1, D_MODEL)),
            _layer_weight(wg, layer),
            _layer_weight(wu, layer),
            _layer_weight(wd, layer),
            _resident((1, D_MODEL)),
            _resident(w_in.shape),
            _resident(conv_w.shape),
            _resident(conv_b.shape),
            _resident(w_gate.shape),
            _resident(ga_b.shape),
            _resident(gx_b.shape),
            _resident(lam.shape),
            _resident(w_out.shape),
        ],
        out_specs=pl.BlockSpec(block, out_map),
        scratch_shapes=[
            pltpu.VMEM((rows, D_MODEL), F32),
            pltpu.VMEM((rows, D_MODEL), F32),
            pltpu.VMEM((rows, D_MODEL), BF16),
            pltpu.VMEM((rows, D_FF), BF16),
            pltpu.VMEM((rows, D_MODEL), BF16),
            pltpu.VMEM((rows, LRU_WIDTH), F32),
            pltpu.VMEM(((LRU_HALO + ODD_STEPS) * MIX_BATCH, LRU_WIDTH), F32),
            pltpu.VMEM((rows, LRU_WIDTH), F32),
            pltpu.VMEM((rows, 2 * LRU_WIDTH), F32),
            pltpu.VMEM((MIX_BATCH, LRU_WIDTH), F32),
        ],
        compiler_params=pltpu.CompilerParams(
            dimension_semantics=("arbitrary",), vmem_limit_bytes=FUSED_VMEM_LIMIT_BYTES),
        name="ffn_odd",
    )(h, gain1, wg, wu, wd, gain_mix, w_in, conv_w, conv_b, w_gate, ga_b, gx_b, lam, w_out)


def _ffn_even_kernel(sinks_ref, x_ref, g1_ref, wg_ref, wu_ref, wd_ref, gm_ref, win_ref, bias_ref,
                     cw_ref, cb_ref, lng_ref, lnb_ref, wout_ref, o_ref,
                     h1_new, h1_prev, xn_scr, a_scr, hn_scr, qkv_scr, qseq_scr, kvprev_scr, oseq_scr,
                     attn_scr, xpad_scr, *, tiles_per_seq, batch_major_in):
    steps = o_ref.shape[0]
    rows = steps * MIX_BATCH
    halo_rows = CONV_HALO * MIX_BATCH
    n_keys = WINDOW + steps
    i = pl.program_id(0)
    seq_tile = lax.rem(i + tiles_per_seq - 1, tiles_per_seq)

    @pl.when(i == 0)
    def _():
        h1_prev[...] = jnp.zeros(h1_prev.shape, F32)

    @pl.when(jnp.logical_or(i == 0, seq_tile == 0))
    def _():
        kvprev_scr[...] = jnp.zeros(kvprev_scr.shape, F32)
        xpad_scr[0:halo_rows, :] = jnp.zeros((halo_rows, CONV_CHANNELS), F32)

    def ffn_norm():
        for k in range(rows // ROW_CHUNK):
            if batch_major_in:
                r0 = k * ROW_CHUNK
                x = x_ref[r0 // steps, r0 % steps:r0 % steps + ROW_CHUNK, :]
            else:
                x = x_ref[k * STEPS_PER_CHUNK:(k + 1) * STEPS_PER_CHUNK].reshape(ROW_CHUNK,
                                                                                 D_MODEL)
            xn_scr[k * ROW_CHUNK:(k + 1) * ROW_CHUNK, :] = (
                _rmsnorm_rows(x, g1_ref[...]).astype(BF16))

    def ffn_up(c0, c1):
        for c in range(c0, c1):
            cols = slice(c * FF_CHUNK, (c + 1) * FF_CHUNK)
            g = jnp.dot(xn_scr[...], wg_ref[:, cols], preferred_element_type=F32)
            u = jnp.dot(xn_scr[...], wu_ref[:, cols], preferred_element_type=F32)
            a_scr[:, cols] = _silu_mul(g, u).astype(BF16)

    def ffn_down():
        y = jnp.dot(a_scr[...], wd_ref[...], preferred_element_type=F32)
        if batch_major_in:
            for b in range(MIX_BATCH):
                h1_new[:, b, :] = x_ref[b] + 0.5 * y[b * steps:(b + 1) * steps, :]
        else:
            h1_new[...] = x_ref[...] + 0.5 * y.reshape(steps, MIX_BATCH, D_MODEL)

    ffn_norm()
    _norm_tile(h1_prev, gm_ref, hn_scr)
    ffn_up(0, 2)

    for j in range((Q_WIDTH + 2 * KV_WIDTH) // MXU_DIM):
        res = jnp.dot(hn_scr[...], win_ref[:, j * MXU_DIM:(j + 1) * MXU_DIM],
                      preferred_element_type=F32)
        qkv_scr[2 * j] = res[:, :LANES]
        qkv_scr[2 * j + 1] = res[:, LANES:]
    o_g = Q_WIDTH + 2 * KV_WIDTH
    glu_a = jnp.dot(hn_scr[...], win_ref[:, o_g:o_g + CONV_CHANNELS], preferred_element_type=F32)
    glu_b = jnp.dot(hn_scr[...], win_ref[:, o_g + CONV_CHANNELS:], preferred_element_type=F32)
    xpad_scr[halo_rows:, :] = glu_a * _sigmoid(glu_b)

    k_slab = Q_WIDTH // LANES
    v_slab = k_slab + 1
    lane = lax.broadcasted_iota(jnp.int32, (n_keys, LANES), 1)
    qlane = lax.broadcasted_iota(jnp.int32, (steps, LANES), 1)
    scale = 1.0 / math.sqrt(HEAD_DIM)
    variant = jnp.minimum(seq_tile, WINDOW // steps)
    for b in range(MIX_BATCH):
        for j in range(qkv_scr.shape[0]):
            qseq_scr[b, j] = qkv_scr[j, pl.ds(b, steps, stride=MIX_BATCH), :]
    score_blocks = []
    for b in range(MIX_BATCH):
        k_cat = jnp.concatenate([kvprev_scr[b, 0], qseq_scr[b, k_slab]], axis=0)
        v_cat = jnp.concatenate([kvprev_scr[b, 1], qseq_scr[b, v_slab]], axis=0)
        kvprev_scr[b, 0] = k_cat[steps:, :]
        kvprev_scr[b, 1] = v_cat[steps:, :]
        k_sw = pltpu.roll(k_cat, HEAD_DIM, axis=1)
        v_sw = pltpu.roll(v_cat, HEAD_DIM, axis=1)
        k_dup = (jnp.where(lane < HEAD_DIM, k_cat, k_sw).astype(BF16),
                 jnp.where(lane < HEAD_DIM, k_sw, k_cat).astype(BF16))
        v_dup = (jnp.where(lane < HEAD_DIM, v_cat, v_sw).astype(BF16),
                 jnp.where(lane < HEAD_DIM, v_sw, v_cat).astype(BF16))
        for c in range(Q_WIDTH // LANES):
            kvh = (2 * c) // GROUP
            qcol = qseq_scr[b, c] * scale
            for half in range(2):
                keep = (qlane < HEAD_DIM) if half == 0 else (qlane >= HEAD_DIM)
                q = jnp.where(keep, qcol, 0.0).astype(BF16)
                sc = lax.dot_general(q, k_dup[kvh], (((1,), (1,)), ((), ())),
                                     preferred_element_type=F32)
                score_blocks.append((2 * c + half, sc, v_dup[kvh]))
    ffn_up(2, 9)
    prob_blocks = []
    for head, sc, v in score_blocks:
        sc = sc + bias_ref[variant, head]
        sink = sinks_ref[head]
        m = jnp.maximum(jnp.max(sc, axis=-1, keepdims=True), sink)
        e = jnp.exp(sc - m)
        den = jnp.sum(e, axis=-1, keepdims=True) + jnp.exp(sink - m)
        prob_blocks.append(((e / den).astype(BF16), v))
    outs = [jnp.dot(p, v, preferred_element_type=F32) for p, v in prob_blocks]
    ffn_up(9, N_FF_CHUNKS)
    ffn_down()
    for b in range(MIX_BATCH):
        for c in range(Q_WIDTH // LANES):
            n = (b * (Q_WIDTH // LANES) + c) * 2
            oseq_scr[b, c] = jnp.where(qlane < HEAD_DIM, outs[n], outs[n + 1])
    for b in range(MIX_BATCH):
        for c in range(Q_WIDTH // LANES):
            attn_scr[c, pl.ds(b, steps, stride=MIX_BATCH), :] = oseq_scr[b, c]
    for c in range(Q_WIDTH // LANES):
        hn_scr[:, c * LANES:(c + 1) * LANES] = attn_scr[c].astype(BF16)

    first_step = CONV_HALO - (CONV_WIDTH - 1)
    for k in range(rows // ROW_CHUNK):
        acc = jnp.zeros((ROW_CHUNK, CONV_CHANNELS), F32)
        for tap in range(CONV_WIDTH):
            r0 = (k * STEPS_PER_CHUNK + first_step + tap) * MIX_BATCH
            acc = acc + xpad_scr[r0:r0 + ROW_CHUNK, :] * cw_ref[tap:tap + 1, :]
        y = acc + cb_ref[...]
        mu = jnp.mean(y, axis=-1, keepdims=True)
        yc = y - mu
        var = jnp.mean(yc * yc, axis=-1, keepdims=True)
        z = yc * lax.rsqrt(var + LN_EPS) * lng_ref[...] + lnb_ref[...]
        hn_scr[k * ROW_CHUNK:(k + 1) * ROW_CHUNK, Q_WIDTH:] = (z * _sigmoid(z)).astype(BF16)

    mixed = jnp.dot(hn_scr[...], wout_ref[...], preferred_element_type=F32)
    o_ref[...] = h1_prev[...] + mixed.reshape(steps, MIX_BATCH, D_MODEL)
    xpad_scr[0:halo_rows, :] = xpad_scr[rows:rows + halo_rows, :]

    h1_prev[...] = h1_new[...]


def _ffn_even_call(x, gain1, wg, wu, wd, layer, gain_mix, w_in, sinks, bias_tbl, conv_w, conv_b,
                   ln_g, ln_b, w_out, batch_major_in=False):
    if batch_major_in:
        bsz, seq, _ = x.shape
    else:
        seq, bsz, _ = x.shape
    rows = EVEN_STEPS * MIX_BATCH
    tiles_per_seq = seq // EVEN_STEPS
    n_tiles = (bsz // MIX_BATCH) * tiles_per_seq
    block = (EVEN_STEPS, MIX_BATCH, D_MODEL)

    def in_map(i):
        j = jnp.minimum(i, n_tiles - 1)
        if batch_major_in:
            return (j // tiles_per_seq, j % tiles_per_seq, 0)
        return (j % tiles_per_seq, j // tiles_per_seq, 0)

    def out_map(i):
        j = jnp.maximum(i - 1, 0)
        return (j % tiles_per_seq, j // tiles_per_seq, 0)

    in_block = (MIX_BATCH, EVEN_STEPS, D_MODEL) if batch_major_in else block
    return pl.pallas_call(
        functools.partial(_ffn_even_kernel, tiles_per_seq=tiles_per_seq,
                          batch_major_in=batch_major_in),
        out_shape=jax.ShapeDtypeStruct((seq, bsz, D_MODEL), F32),
        grid=(n_tiles + 1,),
        in_specs=[
            pl.BlockSpec(memory_space=pltpu.SMEM),
            pl.BlockSpec(in_block, in_map),
            _resident((1, D_MODEL)),
            _layer_weight(wg, layer),
            _layer_weight(wu, layer),
            _layer_weight(wd, layer),
            _resident((1, D_MODEL)),
            _resident(w_in.shape),
            _resident(bias_tbl.shape),
            _resident(conv_w.shape),
            _resident(conv_b.shape),
            _resident(ln_g.shape),
            _resident(ln_b.shape),
            _resident(w_out.shape),
        ],
        out_specs=pl.BlockSpec(block, out_map),
        scratch_shapes=[
            pltpu.VMEM(block, F32),
            pltpu.VMEM(block, F32),
            pltpu.VMEM((rows, D_MODEL), BF16),
            pltpu.VMEM((rows, D_FF), BF16),
            pltpu.VMEM((rows, D_MODEL), BF16),
            pltpu.VMEM(((Q_WIDTH + 2 * KV_WIDTH) // LANES, rows, LANES), F32),
            pltpu.VMEM((MIX_BATCH, (Q_WIDTH + 2 * KV_WIDTH) // LANES, EVEN_STEPS, LANES), F32),
            pltpu.VMEM((MIX_BATCH, 2, WINDOW, LANES), F32),
            pltpu.VMEM((MIX_BATCH, Q_WIDTH // LANES, EVEN_STEPS, LANES), F32),
            pltpu.VMEM((Q_WIDTH // LANES, rows, LANES), F32),
            pltpu.VMEM(((CONV_HALO + EVEN_STEPS) * MIX_BATCH, CONV_CHANNELS), F32),
        ],
        compiler_params=pltpu.CompilerParams(
            dimension_semantics=("arbitrary",), vmem_limit_bytes=FUSED_VMEM_LIMIT_BYTES),
        name="ffn_even",
    )(sinks, x, gain1, wg, wu, wd, gain_mix, w_in, bias_tbl, conv_w, conv_b, ln_g, ln_b, w_out)


def _t5_bucket_table(steps):
    qi = np.arange(steps)[:, None]
    sj = np.arange(WINDOW + steps)[None, :]
    dist = qi + WINDOW - sj
    n = np.maximum(dist, 0)
    max_exact = NUM_BUCKETS // 2
    nf = np.maximum(n, max_exact).astype(np.float32)
    large = max_exact + (np.log(nf / max_exact) / math.log(MAX_DISTANCE / max_exact)
                         * (NUM_BUCKETS - max_exact)).astype(np.int32)
    large = np.minimum(large, NUM_BUCKETS - 1)
    bucket = np.where(n < max_exact, n, large)
    in_window = (dist >= 0) & (dist < WINDOW)
    return bucket, in_window, sj


def _attention_bias_tables(rel_bias, steps):
    bucket, in_window, sj = _t5_bucket_table(steps)
    onehot = (jnp.asarray(bucket, jnp.int32)[..., None] == jnp.arange(NUM_BUCKETS)).astype(F32)
    bias = jnp.einsum("qsb,bh->hqs", onehot, rel_bias.astype(F32),
                      precision=lax.Precision.HIGHEST)
    tables = []
    for v in range(WINDOW // steps + 1):
        valid = in_window & (sj >= WINDOW - v * steps)
        tables.append(jnp.where(valid[None], bias, NEG_INF))
    return jnp.stack(tables)


def kernel(x, norm_ffn1, ffn1_wg, ffn1_wu, ffn1_wd, norm_mix, norm_ffn2, ffn2_wg, ffn2_wu, ffn2_wd,
           rel_bias, even_w_in, attn_sinks, conv_b_w, conv_b_b, conv_ln_g, conv_ln_b, even_w_out,
           odd_w_in, lru_conv_w, lru_conv_b, gate_a_w, gate_a_b, gate_x_w, gate_x_b, lru_lambda,
           odd_w_out, norm_final):
    bsz, seq, d = x.shape
    depth = norm_ffn1.shape[0]
    bias_tbl = _attention_bias_tables(rel_bias, EVEN_STEPS)
    row = lambda v: v.reshape(1, -1)

    bf16 = lambda w: w.astype(BF16)
    ffn1 = (bf16(ffn1_wg), bf16(ffn1_wu), bf16(ffn1_wd))
    ffn2 = (bf16(ffn2_wg), bf16(ffn2_wu), bf16(ffn2_wd))

    h = x
    for layer in range(depth):
        if layer % 2 == 0:
            e = layer // 2
            h = _ffn_even_call(h, row(norm_ffn1[layer]), *ffn1, layer, row(norm_mix[layer]),
                               even_w_in[e].astype(BF16), attn_sinks[e], bias_tbl, conv_b_w[e],
                               row(conv_b_b[e]), row(conv_ln_g[e]), row(conv_ln_b[e]),
                               even_w_out[e].astype(BF16),
                               batch_major_in=(layer == 0))
        else:
            o = layer // 2
            w_gate = jnp.concatenate([gate_a_w[o], gate_x_w[o]], axis=-1).astype(BF16)
            h = _ffn_odd_call(h, row(norm_ffn1[layer]), *ffn1, layer, row(norm_mix[layer]),
                              odd_w_in[o].astype(BF16), lru_conv_w[o], row(lru_conv_b[o]), w_gate,
                              row(gate_a_b[o]), row(gate_x_b[o]), row(lru_lambda[o]),
                              odd_w_out[o].astype(BF16))
        last = layer == depth - 1
        h = _ffn_call(h, row(norm_ffn2[layer]), *ffn2, layer,
                      final_gain=row(norm_final) if last else None)
    return h
```

```python
import functools
import math

import jax
import jax.numpy as jnp
import numpy as np
from jax import lax
from jax.experimental import pallas as pl
from jax.experimental.pallas import tpu as pltpu

D_MODEL = 1024
D_FF = 2816
HEAD_DIM = 64
N_Q_HEADS = 8
N_KV_HEADS = 2
GROUP = N_Q_HEADS // N_KV_HEADS
WINDOW = 128
Q_WIDTH = N_Q_HEADS * HEAD_DIM
KV_WIDTH = N_KV_HEADS * HEAD_DIM
NUM_BUCKETS = 32
MAX_DISTANCE = 128
CONV_CHANNELS = 512
CONV_WIDTH = 31
LRU_WIDTH = 1024
LRU_HEADS = 8
LRU_BLOCK = 128
LRU_CONV_WIDTH = 4
RG_LRU_C = 8.0
RMS_EPS = 1e-6
LN_EPS = 1e-5
NEG_INF = -1e30

LANES = 128
SUBLANES = 8
MXU_DIM = 256
VMEM_LIMIT_BYTES = 56 * 1024 * 1024
FUSED_VMEM_LIMIT_BYTES = 60 * 1024 * 1024

FF_CHUNK = MXU_DIM
N_FF_CHUNKS = D_FF // FF_CHUNK
FFN_ROWS = 1024
ROW_CHUNK = 64
MIX_BATCH = SUBLANES
STEPS_PER_CHUNK = ROW_CHUNK // MIX_BATCH
EVEN_STEPS = 64
ODD_STEPS = 64
FINAL_STEPS = 64
CONV_HALO = 32
LRU_HALO = LRU_CONV_WIDTH - 1
SCAN_ROWS = 2 * MIX_BATCH

BF16 = jnp.bfloat16
F32 = jnp.float32


def _resident(shape):
    nd = len(shape)
    return pl.BlockSpec(shape, lambda *_: (0,) * nd, pipeline_mode=pl.Buffered(1))


def _sigmoid(x):
    return 0.5 * jnp.tanh(0.5 * x) + 0.5


def _silu_mul(g, u):
    half = 0.5 * g
    return (half + half * jnp.tanh(half)) * u


def _rmsnorm_rows(x, g):
    ms = jnp.mean(x * x, axis=-1, keepdims=True)
    return x * lax.rsqrt(ms + RMS_EPS) * g


def _norm_tile(h_ref, g_ref, hn_scr):
    steps = h_ref.shape[0]
    for i in range(steps // STEPS_PER_CHUNK):
        x = h_ref[i * STEPS_PER_CHUNK:(i + 1) * STEPS_PER_CHUNK].reshape(ROW_CHUNK, D_MODEL)
        hn_scr[i * ROW_CHUNK:(i + 1) * ROW_CHUNK, :] = _rmsnorm_rows(x, g_ref[...]).astype(BF16)


def _ffn_kernel(*refs, final_norm):
    if final_norm:
        (x_ref, xnext_ref, g_ref, wg_ref, wu_ref, wd_ref, gfin_ref, o_ref,
         xn_scr, a_scr, res_scr) = refs
    else:
        x_ref, xnext_ref, g_ref, wg_ref, wu_ref, wd_ref, o_ref, xn_scr, a_scr = refs
    rows = xn_scr.shape[1]
    i = pl.program_id(0)
    slot = lax.rem(i, 2)

    def norm_chunk(src_ref, dst_slot, k):
        if final_norm:
            x = src_ref[k * STEPS_PER_CHUNK:(k + 1) * STEPS_PER_CHUNK].reshape(ROW_CHUNK, D_MODEL)
        else:
            x = src_ref[k * ROW_CHUNK:(k + 1) * ROW_CHUNK, :]
        xn_scr[dst_slot, k * ROW_CHUNK:(k + 1) * ROW_CHUNK, :] = (
            _rmsnorm_rows(x, g_ref[...]).astype(BF16))

    n_norm = rows // ROW_CHUNK
    norm_per_dot = pl.cdiv(n_norm, N_FF_CHUNKS)

    @pl.when(i == 0)
    def _():
        for k in range(n_norm):
            norm_chunk(x_ref, 0, k)

    for c in range(N_FF_CHUNKS):
        cols = slice(c * FF_CHUNK, (c + 1) * FF_CHUNK)
        g = jnp.dot(xn_scr[slot], wg_ref[:, cols], preferred_element_type=F32)
        u = jnp.dot(xn_scr[slot], wu_ref[:, cols], preferred_element_type=F32)
        a_scr[:, cols] = _silu_mul(g, u).astype(BF16)
        for k in range(c * norm_per_dot, min((c + 1) * norm_per_dot, n_norm)):
            norm_chunk(xnext_ref, 1 - slot, k)
    y = jnp.dot(a_scr[...], wd_ref[...], preferred_element_type=F32)
    if final_norm:
        res_scr[...] = _rmsnorm_rows(x_ref[...] + 0.5 * y.reshape(res_scr.shape), gfin_ref[...])
        for b in range(o_ref.shape[0]):
            o_ref[b] = res_scr[:, b, :]
    else:
        o_ref[...] = x_ref[...] + 0.5 * y


def _layer_weight(w, layer):
    return pl.BlockSpec((None,) + w.shape[1:], lambda *_: (layer, 0, 0),
                        pipeline_mode=pl.Buffered(1))


def _ffn_call(x, gain, wg, wu, wd, layer, final_gain=None):
    seq, bsz, _ = x.shape
    final_norm = final_gain is not None
    if final_norm:
        tile_rows = FINAL_STEPS * MIX_BATCH
        n_st = seq // FINAL_STEPS
        n_tiles = (bsz // MIX_BATCH) * n_st
        block = (FINAL_STEPS, MIX_BATCH, D_MODEL)
        x_spec = pl.BlockSpec(block, lambda i: (i % n_st, i // n_st, 0))

        def next_map(i):
            j = jnp.minimum(i + 1, n_tiles - 1)
            return (j % n_st, j // n_st, 0)

        xnext_spec = pl.BlockSpec(block, next_map)
        out_shape = jax.ShapeDtypeStruct((bsz, seq, D_MODEL), F32)
        out_spec = pl.BlockSpec((MIX_BATCH, FINAL_STEPS, D_MODEL),
                                lambda i: (i // n_st, i % n_st, 0))
        operands = (x, x, gain, wg, wu, wd, final_gain)
        extra_specs = [_resident((1, D_MODEL))]
        extra_scratch = [pltpu.VMEM(block, F32)]
    else:
        tile_rows = FFN_ROWS
        x = x.reshape(seq * bsz, D_MODEL)
        n_tiles = x.shape[0] // FFN_ROWS
        x_spec = pl.BlockSpec((FFN_ROWS, D_MODEL), lambda i: (i, 0))
        xnext_spec = pl.BlockSpec((FFN_ROWS, D_MODEL),
                                  lambda i: (jnp.minimum(i + 1, n_tiles - 1), 0))
        out_shape = jax.ShapeDtypeStruct(x.shape, F32)
        out_spec = pl.BlockSpec((FFN_ROWS, D_MODEL), lambda i: (i, 0))
        operands = (x, x, gain, wg, wu, wd)
        extra_specs = []
        extra_scratch = []
    out = pl.pallas_call(
        functools.partial(_ffn_kernel, final_norm=final_norm),
        out_shape=out_shape,
        grid=(n_tiles,),
        in_specs=[
            x_spec,
            xnext_spec,
            _resident((1, D_MODEL)),
            _layer_weight(wg, layer),
            _layer_weight(wu, layer),
            _layer_weight(wd, layer),
        ] + extra_specs,
        out_specs=out_spec,
        scratch_shapes=[
            pltpu.VMEM((2, tile_rows, D_MODEL), BF16),
            pltpu.VMEM((tile_rows, D_FF), BF16),
        ] + extra_scratch,
        compiler_params=pltpu.CompilerParams(
            dimension_semantics=("arbitrary",), vmem_limit_bytes=VMEM_LIMIT_BYTES),
        name="ffn_final" if final_norm else "ffn",
    )(*operands)
    return out if final_norm else out.reshape(seq, bsz, D_MODEL)


def _ffn_odd_kernel(x_ref, g1_ref, wg_ref, wu_ref, wd_ref, gm_ref, win_ref, cw_ref, cb_ref,
                    wgate_ref, gab_ref, gxb_ref, lam_ref, wout_ref, o_ref,
                    h1_new, h1_prev, xn_scr, a_scr, hn_scr, ug_scr, xpad_scr, rec_scr, gpre_scr,
                    state_scr, *, tiles_per_seq):
    steps = x_ref.shape[0]
    rows = steps * MIX_BATCH
    halo_rows = LRU_HALO * MIX_BATCH
    i = pl.program_id(0)

    @pl.when(i == 0)
    def _():
        h1_prev[...] = jnp.zeros(h1_prev.shape, F32)

    @pl.when(jnp.logical_or(i == 0, lax.rem(i + tiles_per_seq - 1, tiles_per_seq) == 0))
    def _():
        xpad_scr[0:halo_rows, :] = jnp.zeros((halo_rows, LRU_WIDTH), F32)
        state_scr[...] = jnp.zeros((MIX_BATCH, LRU_WIDTH), F32)

    def ffn_up(c0, c1):
        for c in range(c0, c1):
            cols = slice(c * FF_CHUNK, (c + 1) * FF_CHUNK)
            g = jnp.dot(xn_scr[...], wg_ref[:, cols], preferred_element_type=F32)
            u = jnp.dot(xn_scr[...], wu_ref[:, cols], preferred_element_type=F32)
            a_scr[:, cols] = _silu_mul(g, u).astype(BF16)

    for k in range(rows // ROW_CHUNK):
        x = x_ref[k * STEPS_PER_CHUNK:(k + 1) * STEPS_PER_CHUNK].reshape(ROW_CHUNK, D_MODEL)
        xn_scr[k * ROW_CHUNK:(k + 1) * ROW_CHUNK, :] = _rmsnorm_rows(x, g1_ref[...]).astype(BF16)
    for k in range(rows // ROW_CHUNK):
        r = slice(k * ROW_CHUNK, (k + 1) * ROW_CHUNK)
        hn_scr[r, :] = _rmsnorm_rows(h1_prev[r, :], gm_ref[...]).astype(BF16)
    ffn_up(0, 2)
    ug_scr[...] = jnp.dot(hn_scr[...], win_ref[:, 0:LRU_WIDTH], preferred_element_type=F32)
    xpad_scr[halo_rows:, :] = jnp.dot(hn_scr[...], win_ref[:, LRU_WIDTH:],
                                      preferred_element_type=F32)
    ffn_up(2, 7)

    for k in range(rows // ROW_CHUNK):
        acc = jnp.zeros((ROW_CHUNK, LRU_WIDTH), F32)
        for tap in range(LRU_CONV_WIDTH):
            r0 = (k * STEPS_PER_CHUNK + tap) * MIX_BATCH
            acc = acc + xpad_scr[r0:r0 + ROW_CHUNK, :] * cw_ref[tap:tap + 1, :]
        rec_scr[k * ROW_CHUNK:(k + 1) * ROW_CHUNK, :] = acc + cb_ref[...]

    for hd in range(LRU_HEADS):
        cs = slice(hd * LRU_BLOCK, (hd + 1) * LRU_BLOCK)
        pre = jnp.dot(rec_scr[:, cs].astype(BF16), wgate_ref[hd], preferred_element_type=F32)
        gpre_scr[:, cs] = pre[:, :LRU_BLOCK]
        gpre_scr[:, LRU_WIDTH + hd * LRU_BLOCK:LRU_WIDTH + (hd + 1) * LRU_BLOCK] = pre[:, LRU_BLOCK:]

    ffn_up(7, N_FF_CHUNKS)
    y = jnp.dot(a_scr[...], wd_ref[...], preferred_element_type=F32)
    h1_new[...] = x_ref[...].reshape(rows, D_MODEL) + 0.5 * y

    lam = lam_ref[...]
    log_sig_lam = -(jnp.maximum(-lam, 0.0) + jnp.log1p(jnp.exp(-jnp.abs(lam))))
    c_log_sig_lam = RG_LRU_C * log_sig_lam

    hprev = state_scr[...]
    for k in range(rows // SCAN_ROWS):
        r = slice(k * SCAN_ROWS, (k + 1) * SCAN_ROWS)
        x = rec_scr[r, :]
        rg = _sigmoid(gpre_scr[r, 0:LRU_WIDTH] + gab_ref[...])
        ig = _sigmoid(gpre_scr[r, LRU_WIDTH:] + gxb_ref[...])
        log_a = rg * c_log_sig_lam
        a = jnp.exp(log_a)
        bx = jnp.sqrt(-jnp.tanh(log_a) * (a * a + 1.0)) * (ig * x)
        hs = []
        for t in range(SCAN_ROWS // MIX_BATCH):
            rt = slice(t * MIX_BATCH, (t + 1) * MIX_BATCH)
            hprev = a[rt, :] * hprev + bx[rt, :]
            hs.append(hprev)
        hs = jnp.concatenate(hs, axis=0)
        hn_scr[r, :] = (jax.nn.gelu(ug_scr[r, :]) * hs).astype(BF16)
    state_scr[...] = hprev

    mixed = jnp.dot(hn_scr[...], wout_ref[...], preferred_element_type=F32)
    o_ref[...] = (h1_prev[...] + mixed).reshape(steps, MIX_BATCH, D_MODEL)
    xpad_scr[0:halo_rows, :] = xpad_scr[rows:rows + halo_rows, :]

    h1_prev[...] = h1_new[...]


def _ffn_odd_call(h, gain1, wg, wu, wd, layer, gain_mix, w_in, conv_w, conv_b, w_gate, ga_b, gx_b,
                  lam, w_out):
    seq, bsz, _ = h.shape
    rows = ODD_STEPS * MIX_BATCH
    tiles_per_seq = seq // ODD_STEPS
    n_tiles = (bsz // MIX_BATCH) * tiles_per_seq
    block = (ODD_STEPS, MIX_BATCH, D_MODEL)

    def in_map(i):
        j = jnp.minimum(i, n_tiles - 1)
        return (j % tiles_per_seq, j // tiles_per_seq, 0)

    def out_map(i):
        j = jnp.maximum(i - 1, 0)
        return (j % tiles_per_seq, j // tiles_per_seq, 0)

    return pl.pallas_call(
        functools.partial(_ffn_odd_kernel, tiles_per_seq=tiles_per_seq),
        out_shape=jax.ShapeDtypeStruct(h.shape, F32),
        grid=(n_tiles + 1,),
        in_specs=[
            pl.BlockSpec(block, in_map),
            _resident((1, D_MODEL)),
            _layer_weight(wg, layer),
            _layer_weight(wu, layer),
            _layer_weight(wd, layer),
            _resident((1, D_MODEL)),
            _resident(w_in.shape),
            _resident(conv_w.shape),
            _resident(conv_b.shape),
            _resident(w_gate.shape),
            _resident(ga_b.shape),
            _resident(gx_b.shape),
            _resident(lam.shape),
            _resident(w_out.shape),
        ],
        out_specs=pl.BlockSpec(block, out_map),
        scratch_shapes=[
            pltpu.VMEM((rows, D_MODEL), F32),
            pltpu.VMEM((rows, D_MODEL), F32),
            pltpu.VMEM((rows, D_MODEL), BF16),
            pltpu.VMEM((rows, D_FF), BF16),
            pltpu.VMEM((rows, D_MODEL), BF16),
            pltpu.VMEM((rows, LRU_WIDTH), F32),
            pltpu.VMEM(((LRU_HALO + ODD_STEPS) * MIX_BATCH, LRU_WIDTH), F32),
            pltpu.VMEM((rows, LRU_WIDTH), F32),
            pltpu.VMEM((rows, 2 * LRU_WIDTH), F32),
            pltpu.VMEM((MIX_BATCH, LRU_WIDTH), F32),
        ],
        compiler_params=pltpu.CompilerParams(
            dimension_semantics=("arbitrary",), vmem_limit_bytes=FUSED_VMEM_LIMIT_BYTES),
        name="ffn_odd",
    )(h, gain1, wg, wu, wd, gain_mix, w_in, conv_w, conv_b, w_gate, ga_b, gx_b, lam, w_out)


def _ffn_even_kernel(sinks_ref, x_ref, g1_ref, wg_ref, wu_ref, wd_ref, gm_ref, win_ref, bias_ref,
                     cw_ref, cb_ref, lng_ref, lnb_ref, wout_ref, o_ref,
                     h1_new, h1_prev, xn_scr, a_scr, hn_scr, qkv_scr, qseq_scr, kvprev_scr, oseq_scr,
                     attn_scr, xpad_scr, *, tiles_per_seq, batch_major_in):
    steps = o_ref.shape[0]
    rows = steps * MIX_BATCH
    halo_rows = CONV_HALO * MIX_BATCH
    n_keys = WINDOW + steps
    i = pl.program_id(0)
    seq_tile = lax.rem(i + tiles_per_seq - 1, tiles_per_seq)

    @pl.when(i == 0)
    def _():
        h1_prev[...] = jnp.zeros(h1_prev.shape, F32)

    @pl.when(jnp.logical_or(i == 0, seq_tile == 0))
    def _():
        kvprev_scr[...] = jnp.zeros(kvprev_scr.shape, F32)
        xpad_scr[0:halo_rows, :] = jnp.zeros((halo_rows, CONV_CHANNELS), F32)

    def ffn_norm():
        for k in range(rows // ROW_CHUNK):
            if batch_major_in:
                r0 = k * ROW_CHUNK
                x = x_ref[r0 // steps, r0 % steps:r0 % steps + ROW_CHUNK, :]
            else:
                x = x_ref[k * STEPS_PER_CHUNK:(k + 1) * STEPS_PER_CHUNK].reshape(ROW_CHUNK,
                                                                                 D_MODEL)
            xn_scr[k * ROW_CHUNK:(k + 1) * ROW_CHUNK, :] = (
                _rmsnorm_rows(x, g1_ref[...]).astype(BF16))

    def ffn_up(c0, c1):
        for c in range(c0, c1):
            cols = slice(c * FF_CHUNK, (c + 1) * FF_CHUNK)
            g = jnp.dot(xn_scr[...], wg_ref[:, cols], preferred_element_type=F32)
            u = jnp.dot(xn_scr[...], wu_ref[:, cols], preferred_element_type=F32)
            a_scr[:, cols] = _silu_mul(g, u).astype(BF16)

    def ffn_down():
        y = jnp.dot(a_scr[...], wd_ref[...], preferred_element_type=F32)
        if batch_major_in:
            for b in range(MIX_BATCH):
                h1_new[:, b, :] = x_ref[b] + 0.5 * y[b * steps:(b + 1) * steps, :]
        else:
            h1_new[...] = x_ref[...] + 0.5 * y.reshape(steps, MIX_BATCH, D_MODEL)

    ffn_norm()
    _norm_tile(h1_prev, gm_ref, hn_scr)
    ffn_up(0, 2)

    for j in range((Q_WIDTH + 2 * KV_WIDTH) // MXU_DIM):
        res = jnp.dot(hn_scr[...], win_ref[:, j * MXU_DIM:(j + 1) * MXU_DIM],
                      preferred_element_type=F32)
        qkv_scr[2 * j] = res[:, :LANES]
        qkv_scr[2 * j + 1] = res[:, LANES:]
    o_g = Q_WIDTH + 2 * KV_WIDTH
    glu_a = jnp.dot(hn_scr[...], win_ref[:, o_g:o_g + CONV_CHANNELS], preferred_element_type=F32)
    glu_b = jnp.dot(hn_scr[...], win_ref[:, o_g + CONV_CHANNELS:], preferred_element_type=F32)
    xpad_scr[halo_rows:, :] = glu_a * _sigmoid(glu_b)

    k_slab = Q_WIDTH // LANES
    v_slab = k_slab + 1
    lane = lax.broadcasted_iota(jnp.int32, (n_keys, LANES), 1)
    qlane = lax.broadcasted_iota(jnp.int32, (steps, LANES), 1)
    scale = 1.0 / math.sqrt(HEAD_DIM)
    variant = jnp.minimum(seq_tile, WINDOW // steps)
    for b in range(MIX_BATCH):
        for j in range(qkv_scr.shape[0]):
            qseq_scr[b, j] = qkv_scr[j, pl.ds(b, steps, stride=MIX_BATCH), :]
    score_blocks = []
    for b in range(MIX_BATCH):
        k_cat = jnp.concatenate([kvprev_scr[b, 0], qseq_scr[b, k_slab]], axis=0)
        v_cat = jnp.concatenate([kvprev_scr[b, 1], qseq_scr[b, v_slab]], axis=0)
        kvprev_scr[b, 0] = k_cat[steps:, :]
        kvprev_scr[b, 1] = v_cat[steps:, :]
        k_sw = pltpu.roll(k_cat, HEAD_DIM, axis=1)
        v_sw = pltpu.roll(v_cat, HEAD_DIM, axis=1)
        k_dup = (jnp.where(lane < HEAD_DIM, k_cat, k_sw).astype(BF16),
                 jnp.where(lane < HEAD_DIM, k_sw, k_cat).astype(BF16))
        v_dup = (jnp.where(lane < HEAD_DIM, v_cat, v_sw).astype(BF16),
                 jnp.where(lane < HEAD_DIM, v_sw, v_cat).astype(BF16))
        for c in range(Q_WIDTH // LANES):
            kvh = (2 * c) // GROUP
            qcol = qseq_scr[b, c] * scale
            for half in range(2):
                keep = (qlane < HEAD_DIM) if half == 0 else (qlane >= HEAD_DIM)
                q = jnp.where(keep, qcol, 0.0).astype(BF16)
                sc = lax.dot_general(q, k_dup[kvh], (((1,), (1,)), ((), ())),
                                     preferred_element_type=F32)
                score_blocks.append((2 * c + half, sc, v_dup[kvh]))
    ffn_up(2, 9)
    prob_blocks = []
    for head, sc, v in score_blocks:
        sc = sc + bias_ref[variant, head]
        sink = sinks_ref[head]
        m = jnp.maximum(jnp.max(sc, axis=-1, keepdims=True), sink)
        e = jnp.exp(sc - m)
        den = jnp.sum(e, axis=-1, keepdims=True) + jnp.exp(sink - m)
        prob_blocks.append(((e / den).astype(BF16), v))
    outs = [jnp.dot(p, v, preferred_element_type=F32) for p, v in prob_blocks]
    ffn_up(9, N_FF_CHUNKS)
    ffn_down()
    for b in range(MIX_BATCH):
        for c in range(Q_WIDTH // LANES):
            n = (b * (Q_WIDTH // LANES) + c) * 2
            oseq_scr[b, c] = jnp.where(qlane < HEAD_DIM, outs[n], outs[n + 1])
    for b in range(MIX_BATCH):
        for c in range(Q_WIDTH // LANES):
            attn_scr[c, pl.ds(b, steps, stride=MIX_BATCH), :] = oseq_scr[b, c]
    for c in range(Q_WIDTH // LANES):
        hn_scr[:, c * LANES:(c + 1) * LANES] = attn_scr[c].astype(BF16)

    first_step = CONV_HALO - (CONV_WIDTH - 1)
    for k in range(rows // ROW_CHUNK):
        acc = jnp.zeros((ROW_CHUNK, CONV_CHANNELS), F32)
        for tap in range(CONV_WIDTH):
            r0 = (k * STEPS_PER_CHUNK + first_step + tap) * MIX_BATCH
            acc = acc + xpad_scr[r0:r0 + ROW_CHUNK, :] * cw_ref[tap:tap + 1, :]
        y = acc + cb_ref[...]
        mu = jnp.mean(y, axis=-1, keepdims=True)
        yc = y - mu
        var = jnp.mean(yc * yc, axis=-1, keepdims=True)
        z = yc * lax.rsqrt(var + LN_EPS) * lng_ref[...] + lnb_ref[...]
        hn_scr[k * ROW_CHUNK:(k + 1) * ROW_CHUNK, Q_WIDTH:] = (z * _sigmoid(z)).astype(BF16)

    mixed = jnp.dot(hn_scr[...], wout_ref[...], preferred_element_type=F32)
    o_ref[...] = h1_prev[...] + mixed.reshape(steps, MIX_BATCH, D_MODEL)
    xpad_scr[0:halo_rows, :] = xpad_scr[rows:rows + halo_rows, :]

    h1_prev[...] = h1_new[...]


def _ffn_even_call(x, gain1, wg, wu, wd, layer, gain_mix, w_in, sinks, bias_tbl, conv_w, conv_b,
                   ln_g, ln_b, w_out, batch_major_in=False):
    if batch_major_in:
        bsz, seq, _ = x.shape
    else:
        seq, bsz, _ = x.shape
    rows = EVEN_STEPS * MIX_BATCH
    tiles_per_seq = seq // EVEN_STEPS
    n_tiles = (bsz // MIX_BATCH) * tiles_per_seq
    block = (EVEN_STEPS, MIX_BATCH, D_MODEL)

    def in_map(i):
        j = jnp.minimum(i, n_tiles - 1)
        if batch_major_in:
            return (j // tiles_per_seq, j % tiles_per_seq, 0)
        return (j % tiles_per_seq, j // tiles_per_seq, 0)

    def out_map(i):
        j = jnp.maximum(i - 1, 0)
        return (j % tiles_per_seq, j // tiles_per_seq, 0)

    in_block = (MIX_BATCH, EVEN_STEPS, D_MODEL) if batch_major_in else block
    return pl.pallas_call(
        functools.partial(_ffn_even_kernel, tiles_per_seq=tiles_per_seq,
                          batch_major_in=batch_major_in),
        out_shape=jax.ShapeDtypeStruct((seq, bsz, D_MODEL), F32),
        grid=(n_tiles + 1,),
        in_specs=[
            pl.BlockSpec(memory_space=pltpu.SMEM),
            pl.BlockSpec(in_block, in_map),
            _resident((1, D_MODEL)),
            _layer_weight(wg, layer),
            _layer_weight(wu, layer),
            _layer_weight(wd, layer),
            _resident((1, D_MODEL)),
            _resident(w_in.shape),
            _resident(bias_tbl.shape),
            _resident(conv_w.shape),
            _resident(conv_b.shape),
            _resident(ln_g.shape),
            _resident(ln_b.shape),
            _resident(w_out.shape),
        ],
        out_specs=pl.BlockSpec(block, out_map),
        scratch_shapes=[
            pltpu.VMEM(block, F32),
            pltpu.VMEM(block, F32),
            pltpu.VMEM((rows, D_MODEL), BF16),
            pltpu.VMEM((rows, D_FF), BF16),
            pltpu.VMEM((rows, D_MODEL), BF16),
            pltpu.VMEM(((Q_WIDTH + 2 * KV_WIDTH) // LANES, rows, LANES), F32),
            pltpu.VMEM((MIX_BATCH, (Q_WIDTH + 2 * KV_WIDTH) // LANES, EVEN_STEPS, LANES), F32),
            pltpu.VMEM((MIX_BATCH, 2, WINDOW, LANES), F32),
            pltpu.VMEM((MIX_BATCH, Q_WIDTH // LANES, EVEN_STEPS, LANES), F32),
            pltpu.VMEM((Q_WIDTH // LANES, rows, LANES), F32),
            pltpu.VMEM(((CONV_HALO + EVEN_STEPS) * MIX_BATCH, CONV_CHANNELS), F32),
        ],
        compiler_params=pltpu.CompilerParams(
            dimension_semantics=("arbitrary",), vmem_limit_bytes=FUSED_VMEM_LIMIT_BYTES),
        name="ffn_even",
    )(sinks, x, gain1, wg, wu, wd, gain_mix, w_in, bias_tbl, conv_w, conv_b, ln_g, ln_b, w_out)


def _t5_bucket_table(steps):
    qi = np.arange(steps)[:, None]
    sj = np.arange(WINDOW + steps)[None, :]
    dist = qi + WINDOW - sj
    n = np.maximum(dist, 0)
    max_exact = NUM_BUCKETS // 2
    nf = np.maximum(n, max_exact).astype(np.float32)
    large = max_exact + (np.log(nf / max_exact) / math.log(MAX_DISTANCE / max_exact)
                         * (NUM_BUCKETS - max_exact)).astype(np.int32)
    large = np.minimum(large, NUM_BUCKETS - 1)
    bucket = np.where(n < max_exact, n, large)
    in_window = (dist >= 0) & (dist < WINDOW)
    return bucket, in_window, sj


def _attention_bias_tables(rel_bias, steps):
    bucket, in_window, sj = _t5_bucket_table(steps)
    onehot = (jnp.asarray(bucket, jnp.int32)[..., None] == jnp.arange(NUM_BUCKETS)).astype(F32)
    bias = jnp.einsum("qsb,bh->hqs", onehot, rel_bias.astype(F32),
                      precision=lax.Precision.HIGHEST)
    tables = []
    for v in range(WINDOW // steps + 1):
        valid = in_window & (sj >= WINDOW - v * steps)
        tables.append(jnp.where(valid[None], bias, NEG_INF))
    return jnp.stack(tables)


def kernel(x, norm_ffn1, ffn1_wg, ffn1_wu, ffn1_wd, norm_mix, norm_ffn2, ffn2_wg, ffn2_wu, ffn2_wd,
           rel_bias, even_w_in, attn_sinks, conv_b_w, conv_b_b, conv_ln_g, conv_ln_b, even_w_out,
           odd_w_in, lru_conv_w, lru_conv_b, gate_a_w, gate_a_b, gate_x_w, gate_x_b, lru_lambda,
           odd_w_out, norm_final):
    bsz, seq, d = x.shape
    depth = norm_ffn1.shape[0]
    bias_tbl = _attention_bias_tables(rel_bias, EVEN_STEPS)
    row = lambda v: v.reshape(1, -1)

    bf16 = lambda w: w.astype(BF16)
    ffn1 = (bf16(ffn1_wg), bf16(ffn1_wu), bf16(ffn1_wd))
    ffn2 = (bf16(ffn2_wg), bf16(ffn2_wu), bf16(ffn2_wd))

    h = x
    for layer in range(depth):
        if layer % 2 == 0:
            e = layer // 2
            h = _ffn_even_call(h, row(norm_ffn1[layer]), *ffn1, layer, row(norm_mix[layer]),
                               even_w_in[e].astype(BF16), attn_sinks[e], bias_tbl, conv_b_w[e],
                               row(conv_b_b[e]), row(conv_ln_g[e]), row(conv_ln_b[e]),
                               even_w_out[e].astype(BF16),
                               batch_major_in=(layer == 0))
        else:
            o = layer // 2
            w_gate = jnp.concatenate([gate_a_w[o], gate_x_w[o]], axis=-1).astype(BF16)
            h = _ffn_odd_call(h, row(norm_ffn1[layer]), *ffn1, layer, row(norm_mix[layer]),
                              odd_w_in[o].astype(BF16), lru_conv_w[o], row(lru_conv_b[o]), w_gate,
                              row(gate_a_b[o]), row(gate_x_b[o]), row(lru_lambda[o]),
                              odd_w_out[o].astype(BF16))
        last = layer == depth - 1
        h = _ffn_call(h, row(norm_ffn2[layer]), *ffn2, layer,
                      final_gain=row(norm_final) if last else None)
    return h
```

```python
import functools
import math

import jax
import jax.numpy as jnp
import numpy as np
from jax import lax
from jax.experimental import pallas as pl
from jax.experimental.pallas import tpu as pltpu

D_MODEL = 1024
D_FF = 2816
HEAD_DIM = 64
N_Q_HEADS = 8
N_KV_HEADS = 2
GROUP = N_Q_HEADS // N_KV_HEADS
WINDOW = 128
Q_WIDTH = N_Q_HEADS * HEAD_DIM
KV_WIDTH = N_KV_HEADS * HEAD_DIM
NUM_BUCKETS = 32
MAX_DISTANCE = 128
CONV_CHANNELS = 512
CONV_WIDTH = 31
LRU_WIDTH = 1024
LRU_HEADS = 8
LRU_BLOCK = 128
LRU_CONV_WIDTH = 4
RG_LRU_C = 8.0
RMS_EPS = 1e-6
LN_EPS = 1e-5
NEG_INF = -1e30

LANES = 128
SUBLANES = 8
MXU_DIM = 256
VMEM_LIMIT_BYTES = 56 * 1024 * 1024
FUSED_VMEM_LIMIT_BYTES = 60 * 1024 * 1024

FF_CHUNK = MXU_DIM
N_FF_CHUNKS = D_FF // FF_CHUNK
FFN_ROWS = 1024
ROW_CHUNK = 64
MIX_BATCH = SUBLANES
STEPS_PER_CHUNK = ROW_CHUNK // MIX_BATCH
EVEN_STEPS = 64
ODD_STEPS = 64
FINAL_STEPS = 128
CONV_HALO = 32
LRU_HALO = LRU_CONV_WIDTH - 1
SCAN_ROWS = 2 * MIX_BATCH

BF16 = jnp.bfloat16
F32 = jnp.float32


def _resident(shape):
    nd = len(shape)
    return pl.BlockSpec(shape, lambda *_: (0,) * nd, pipeline_mode=pl.Buffered(1))


def _sigmoid(x):
    return 0.5 * jnp.tanh(0.5 * x) + 0.5


def _silu_mul(g, u):
    half = 0.5 * g
    return (half + half * jnp.tanh(half)) * u


def _rmsnorm_rows(x, g):
    ms = jnp.mean(x * x, axis=-1, keepdims=True)
    return x * lax.rsqrt(ms + RMS_EPS) * g


def _norm_tile(h_ref, g_ref, hn_scr):
    steps = h_ref.shape[0]
    for i in range(steps // STEPS_PER_CHUNK):
        x = h_ref[i * STEPS_PER_CHUNK:(i + 1) * STEPS_PER_CHUNK].reshape(ROW_CHUNK, D_MODEL)
        hn_scr[i * ROW_CHUNK:(i + 1) * ROW_CHUNK, :] = _rmsnorm_rows(x, g_ref[...]).astype(BF16)


def _ffn_kernel(*refs, final_norm):
    if final_norm:
        (x_ref, xnext_ref, g_ref, wg_ref, wu_ref, wd_ref, gfin_ref, o_ref,
         xn_scr, a_scr, res_scr) = refs
    else:
        x_ref, xnext_ref, g_ref, wg_ref, wu_ref, wd_ref, o_ref, xn_scr, a_scr = refs
    rows = xn_scr.shape[1]
    i = pl.program_id(0)
    slot = lax.rem(i, 2)

    def norm_chunk(src_ref, dst_slot, k):
        if final_norm:
            x = src_ref[k * STEPS_PER_CHUNK:(k + 1) * STEPS_PER_CHUNK].reshape(ROW_CHUNK, D_MODEL)
        else:
            x = src_ref[k * ROW_CHUNK:(k + 1) * ROW_CHUNK, :]
        xn_scr[dst_slot, k * ROW_CHUNK:(k + 1) * ROW_CHUNK, :] = (
            _rmsnorm_rows(x, g_ref[...]).astype(BF16))

    n_norm = rows // ROW_CHUNK
    norm_per_dot = pl.cdiv(n_norm, N_FF_CHUNKS)

    @pl.when(i == 0)
    def _():
        for k in range(n_norm):
            norm_chunk(x_ref, 0, k)

    for c in range(N_FF_CHUNKS):
        cols = slice(c * FF_CHUNK, (c + 1) * FF_CHUNK)
        g = jnp.dot(xn_scr[slot], wg_ref[:, cols], preferred_element_type=F32)
        u = jnp.dot(xn_scr[slot], wu_ref[:, cols], preferred_element_type=F32)
        a_scr[:, cols] = _silu_mul(g, u).astype(BF16)
        for k in range(c * norm_per_dot, min((c + 1) * norm_per_dot, n_norm)):
            norm_chunk(xnext_ref, 1 - slot, k)
    y = jnp.dot(a_scr[...], wd_ref[...], preferred_element_type=F32)
    if final_norm:
        res_scr[...] = _rmsnorm_rows(x_ref[...] + 0.5 * y.reshape(res_scr.shape), gfin_ref[...])
        for b in range(o_ref.shape[0]):
            o_ref[b] = res_scr[:, b, :]
    else:
        o_ref[...] = x_ref[...] + 0.5 * y


def _layer_weight(w, layer):
    return pl.BlockSpec((None,) + w.shape[1:], lambda *_: (layer, 0, 0),
                        pipeline_mode=pl.Buffered(1))


def _ffn_call(x, gain, wg, wu, wd, layer, final_gain=None):
    seq, bsz, _ = x.shape
    final_norm = final_gain is not None
    if final_norm:
        tile_rows = FINAL_STEPS * MIX_BATCH
        n_st = seq // FINAL_STEPS
        n_tiles = (bsz // MIX_BATCH) * n_st
        block = (FINAL_STEPS, MIX_BATCH, D_MODEL)
        x_spec = pl.BlockSpec(block, lambda i: (i % n_st, i // n_st, 0))

        def next_map(i):
            j = jnp.minimum(i + 1, n_tiles - 1)
            return (j % n_st, j // n_st, 0)

        xnext_spec = pl.BlockSpec(block, next_map)
        out_shape = jax.ShapeDtypeStruct((bsz, seq, D_MODEL), F32)
        out_spec = pl.BlockSpec((MIX_BATCH, FINAL_STEPS, D_MODEL),
                                lambda i: (i // n_st, i % n_st, 0))
        operands = (x, x, gain, wg, wu, wd, final_gain)
        extra_specs = [_resident((1, D_MODEL))]
        extra_scratch = [pltpu.VMEM(block, F32)]
    else:
        tile_rows = FFN_ROWS
        x = x.reshape(seq * bsz, D_MODEL)
        n_tiles = x.shape[0] // FFN_ROWS
        x_spec = pl.BlockSpec((FFN_ROWS, D_MODEL), lambda i: (i, 0))
        xnext_spec = pl.BlockSpec((FFN_ROWS, D_MODEL),
                                  lambda i: (jnp.minimum(i + 1, n_tiles - 1), 0))
        out_shape = jax.ShapeDtypeStruct(x.shape, F32)
        out_spec = pl.BlockSpec((FFN_ROWS, D_MODEL), lambda i: (i, 0))
        operands = (x, x, gain, wg, wu, wd)
        extra_specs = []
        extra_scratch = []
    out = pl.pallas_call(
        functools.partial(_ffn_kernel, final_norm=final_norm),
        out_shape=out_shape,
        grid=(n_tiles,),
        in_specs=[
            x_spec,
            xnext_spec,
            _resident((1, D_MODEL)),
            _layer_weight(wg, layer),
            _layer_weight(wu, layer),
            _layer_weight(wd, layer),
        ] + extra_specs,
        out_specs=out_spec,
        scratch_shapes=[
            pltpu.VMEM((2, tile_rows, D_MODEL), BF16),
            pltpu.VMEM((tile_rows, D_FF), BF16),
        ] + extra_scratch,
        compiler_params=pltpu.CompilerParams(
            dimension_semantics=("arbitrary",), vmem_limit_bytes=VMEM_LIMIT_BYTES),
        name="ffn_final" if final_norm else "ffn",
    )(*operands)
    return out if final_norm else out.reshape(seq, bsz, D_MODEL)


def _ffn_odd_kernel(x_ref, g1_ref, wg_ref, wu_ref, wd_ref, gm_ref, win_ref, cw_ref, cb_ref,
                    wgate_ref, gab_ref, gxb_ref, lam_ref, wout_ref, o_ref,
                    h1_new, h1_prev, xn_scr, a_scr, hn_scr, ug_scr, xpad_scr, rec_scr, gpre_scr,
                    state_scr, *, tiles_per_seq):
    steps = x_ref.shape[0]
    rows = steps * MIX_BATCH
    halo_rows = LRU_HALO * MIX_BATCH
    i = pl.program_id(0)

    @pl.when(i == 0)
    def _():
        h1_prev[...] = jnp.zeros(h1_prev.shape, F32)

    @pl.when(jnp.logical_or(i == 0, lax.rem(i + tiles_per_seq - 1, tiles_per_seq) == 0))
    def _():
        xpad_scr[0:halo_rows, :] = jnp.zeros((halo_rows, LRU_WIDTH), F32)
        state_scr[...] = jnp.zeros((MIX_BATCH, LRU_WIDTH), F32)

    def ffn_up(c0, c1):
        for c in range(c0, c1):
            cols = slice(c * FF_CHUNK, (c + 1) * FF_CHUNK)
            g = jnp.dot(xn_scr[...], wg_ref[:, cols], preferred_element_type=F32)
            u = jnp.dot(xn_scr[...], wu_ref[:, cols], preferred_element_type=F32)
            a_scr[:, cols] = _silu_mul(g, u).astype(BF16)

    for k in range(rows // ROW_CHUNK):
        x = x_ref[k * STEPS_PER_CHUNK:(k + 1) * STEPS_PER_CHUNK].reshape(ROW_CHUNK, D_MODEL)
        xn_scr[k * ROW_CHUNK:(k + 1) * ROW_CHUNK, :] = _rmsnorm_rows(x, g1_ref[...]).astype(BF16)
    for k in range(rows // ROW_CHUNK):
        r = slice(k * ROW_CHUNK, (k + 1) * ROW_CHUNK)
        hn_scr[r, :] = _rmsnorm_rows(h1_prev[r, :], gm_ref[...]).astype(BF16)
    ffn_up(0, 2)
    ug_scr[...] = jnp.dot(hn_scr[...], win_ref[:, 0:LRU_WIDTH], preferred_element_type=F32)
    xpad_scr[halo_rows:, :] = jnp.dot(hn_scr[...], win_ref[:, LRU_WIDTH:],
                                      preferred_element_type=F32)
    ffn_up(2, 7)

    for k in range(rows // ROW_CHUNK):
        acc = jnp.zeros((ROW_CHUNK, LRU_WIDTH), F32)
        for tap in range(LRU_CONV_WIDTH):
            r0 = (k * STEPS_PER_CHUNK + tap) * MIX_BATCH
            acc = acc + xpad_scr[r0:r0 + ROW_CHUNK, :] * cw_ref[tap:tap + 1, :]
        rec_scr[k * ROW_CHUNK:(k + 1) * ROW_CHUNK, :] = acc + cb_ref[...]

    for hd in range(LRU_HEADS):
        cs = slice(hd * LRU_BLOCK, (hd + 1) * LRU_BLOCK)
        pre = jnp.dot(rec_scr[:, cs].astype(BF16), wgate_ref[hd], preferred_element_type=F32)
        gpre_scr[:, cs] = pre[:, :LRU_BLOCK]
        gpre_scr[:, LRU_WIDTH + hd * LRU_BLOCK:LRU_WIDTH + (hd + 1) * LRU_BLOCK] = pre[:, LRU_BLOCK:]

    ffn_up(7, N_FF_CHUNKS)
    y = jnp.dot(a_scr[...], wd_ref[...], preferred_element_type=F32)
    h1_new[...] = x_ref[...].reshape(rows, D_MODEL) + 0.5 * y

    lam = lam_ref[...]
    log_sig_lam = -(jnp.maximum(-lam, 0.0) + jnp.log1p(jnp.exp(-jnp.abs(lam))))
    c_log_sig_lam = RG_LRU_C * log_sig_lam

    hprev = state_scr[...]
    for k in range(rows // SCAN_ROWS):
        r = slice(k * SCAN_ROWS, (k + 1) * SCAN_ROWS)
        x = rec_scr[r, :]
        rg = _sigmoid(gpre_scr[r, 0:LRU_WIDTH] + gab_ref[...])
        ig = _sigmoid(gpre_scr[r, LRU_WIDTH:] + gxb_ref[...])
        log_a = rg * c_log_sig_lam
        a = jnp.exp(log_a)
        bx = jnp.sqrt(-jnp.tanh(log_a) * (a * a + 1.0)) * (ig * x)
        hs = []
        for t in range(SCAN_ROWS // MIX_BATCH):
            rt = slice(t * MIX_BATCH, (t + 1) * MIX_BATCH)
            hprev = a[rt, :] * hprev + bx[rt, :]
            hs.append(hprev)
        hs = jnp.concatenate(hs, axis=0)
        hn_scr[r, :] = (jax.nn.gelu(ug_scr[r, :]) * hs).astype(BF16)
    state_scr[...] = hprev

    mixed = jnp.dot(hn_scr[...], wout_ref[...], preferred_element_type=F32)
    o_ref[...] = (h1_prev[...] + mixed).reshape(steps, MIX_BATCH, D_MODEL)
    xpad_scr[0:halo_rows, :] = xpad_scr[rows:rows + halo_rows, :]

    h1_prev[...] = h1_new[...]


def _ffn_odd_call(h, gain1, wg, wu, wd, layer, gain_mix, w_in, conv_w, conv_b, w_gate, ga_b, gx_b,
                  lam, w_out):
    seq, bsz, _ = h.shape
    rows = ODD_STEPS * MIX_BATCH
    tiles_per_seq = seq // ODD_STEPS
    n_tiles = (bsz // MIX_BATCH) * tiles_per_seq
    block = (ODD_STEPS, MIX_BATCH, D_MODEL)

    def in_map(i):
        j = jnp.minimum(i, n_tiles - 1)
        return (j % tiles_per_seq, j // tiles_per_seq, 0)

    def out_map(i):
        j = jnp.maximum(i - 1, 0)
        return (j % tiles_per_seq, j // tiles_per_seq, 0)

    return pl.pallas_call(
        functools.partial(_ffn_odd_kernel, tiles_per_seq=tiles_per_seq),
        out_shape=jax.ShapeDtypeStruct(h.shape, F32),
        grid=(n_tiles + 1,),
        in_specs=[
            pl.BlockSpec(block, in_map),
            _resident((1, D_MODEL)),
            _layer_weight(wg, layer),
            _layer_weight(wu, layer),
            _layer_weight(wd, layer),
            _resident((1, D_MODEL)),
            _resident(w_in.shape),
            _resident(conv_w.shape),
            _resident(conv_b.shape),
            _resident(w_gate.shape),
            _resident(ga_b.shape),
            _resident(gx_b.shape),
            _resident(lam.shape),
            _resident(w_out.shape),
        ],
        out_specs=pl.BlockSpec(block, out_map),
        scratch_shapes=[
            pltpu.VMEM((rows, D_MODEL), F32),
            pltpu.VMEM((rows, D_MODEL), F32),
            pltpu.VMEM((rows, D_MODEL), BF16),
            pltpu.VMEM((rows, D_FF), BF16),
            pltpu.VMEM((rows, D_MODEL), BF16),
            pltpu.VMEM((rows, LRU_WIDTH), F32),
            pltpu.VMEM(((LRU_HALO + ODD_STEPS) * MIX_BATCH, LRU_WIDTH), F32),
            pltpu.VMEM((rows, LRU_WIDTH), F32),
            pltpu.VMEM((rows, 2 * LRU_WIDTH), F32),
            pltpu.VMEM((MIX_BATCH, LRU_WIDTH), F32),
        ],
        compiler_params=pltpu.CompilerParams(
            dimension_semantics=("arbitrary",), vmem_limit_bytes=FUSED_VMEM_LIMIT_BYTES),
        name="ffn_odd",
    )(h, gain1, wg, wu, wd, gain_mix, w_in, conv_w, conv_b, w_gate, ga_b, gx_b, lam, w_out)


def _ffn_even_kernel(sinks_ref, x_ref, g1_ref, wg_ref, wu_ref, wd_ref, gm_ref, win_ref, bias_ref,
                     cw_ref, cb_ref, lng_ref, lnb_ref, wout_ref, o_ref,
                     h1_new, h1_prev, xn_scr, a_scr, hn_scr, qkv_scr, qseq_scr, kvprev_scr, oseq_scr,
                     attn_scr, xpad_scr, *, tiles_per_seq, batch_major_in):
    steps = o_ref.shape[0]
    rows = steps * MIX_BATCH
    halo_rows = CONV_HALO * MIX_BATCH
    n_keys = WINDOW + steps
    i = pl.program_id(0)
    seq_tile = lax.rem(i + tiles_per_seq - 1, tiles_per_seq)

    @pl.when(i == 0)
    def _():
        h1_prev[...] = jnp.zeros(h1_prev.shape, F32)

    @pl.when(jnp.logical_or(i == 0, seq_tile == 0))
    def _():
        kvprev_scr[...] = jnp.zeros(kvprev_scr.shape, F32)
        xpad_scr[0:halo_rows, :] = jnp.zeros((halo_rows, CONV_CHANNELS), F32)

    def ffn_norm():
        for k in range(rows // ROW_CHUNK):
            if batch_major_in:
                r0 = k * ROW_CHUNK
                x = x_ref[r0 // steps, r0 % steps:r0 % steps + ROW_CHUNK, :]
            else:
                x = x_ref[k * STEPS_PER_CHUNK:(k + 1) * STEPS_PER_CHUNK].reshape(ROW_CHUNK,
                                                                                 D_MODEL)
            xn_scr[k * ROW_CHUNK:(k + 1) * ROW_CHUNK, :] = (
                _rmsnorm_rows(x, g1_ref[...]).astype(BF16))

    def ffn_up(c0, c1):
        for c in range(c0, c1):
            cols = slice(c * FF_CHUNK, (c + 1) * FF_CHUNK)
            g = jnp.dot(xn_scr[...], wg_ref[:, cols], preferred_element_type=F32)
            u = jnp.dot(xn_scr[...], wu_ref[:, cols], preferred_element_type=F32)
            a_scr[:, cols] = _silu_mul(g, u).astype(BF16)

    def ffn_down():
        y = jnp.dot(a_scr[...], wd_ref[...], preferred_element_type=F32)
        if batch_major_in:
            for b in range(MIX_BATCH):
                h1_new[:, b, :] = x_ref[b] + 0.5 * y[b * steps:(b + 1) * steps, :]
        else:
            h1_new[...] = x_ref[...] + 0.5 * y.reshape(steps, MIX_BATCH, D_MODEL)

    ffn_norm()
    _norm_tile(h1_prev, gm_ref, hn_scr)
    ffn_up(0, 2)

    for j in range((Q_WIDTH + 2 * KV_WIDTH) // MXU_DIM):
        res = jnp.dot(hn_scr[...], win_ref[:, j * MXU_DIM:(j + 1) * MXU_DIM],
                      preferred_element_type=F32)
        qkv_scr[2 * j] = res[:, :LANES]
        qkv_scr[2 * j + 1] = res[:, LANES:]
    o_g = Q_WIDTH + 2 * KV_WIDTH
    glu_a = jnp.dot(hn_scr[...], win_ref[:, o_g:o_g + CONV_CHANNELS], preferred_element_type=F32)
    glu_b = jnp.dot(hn_scr[...], win_ref[:, o_g + CONV_CHANNELS:], preferred_element_type=F32)
    xpad_scr[halo_rows:, :] = glu_a * _sigmoid(glu_b)

    k_slab = Q_WIDTH // LANES
    v_slab = k_slab + 1
    lane = lax.broadcasted_iota(jnp.int32, (n_keys, LANES), 1)
    qlane = lax.broadcasted_iota(jnp.int32, (steps, LANES), 1)
    scale = 1.0 / math.sqrt(HEAD_DIM)
    variant = jnp.minimum(seq_tile, WINDOW // steps)
    for b in range(MIX_BATCH):
        for j in range(qkv_scr.shape[0]):
            qseq_scr[b, j] = qkv_scr[j, pl.ds(b, steps, stride=MIX_BATCH), :]
    score_blocks = []
    for b in range(MIX_BATCH):
        k_cat = jnp.concatenate([kvprev_scr[b, 0], qseq_scr[b, k_slab]], axis=0)
        v_cat = jnp.concatenate([kvprev_scr[b, 1], qseq_scr[b, v_slab]], axis=0)
        kvprev_scr[b, 0] = k_cat[steps:, :]
        kvprev_scr[b, 1] = v_cat[steps:, :]
        k_sw = pltpu.roll(k_cat, HEAD_DIM, axis=1)
        v_sw = pltpu.roll(v_cat, HEAD_DIM, axis=1)
        k_dup = (jnp.where(lane < HEAD_DIM, k_cat, k_sw).astype(BF16),
                 jnp.where(lane < HEAD_DIM, k_sw, k_cat).astype(BF16))
        v_dup = (jnp.where(lane < HEAD_DIM, v_cat, v_sw).astype(BF16),
                 jnp.where(lane < HEAD_DIM, v_sw, v_cat).astype(BF16))
        for c in range(Q_WIDTH // LANES):
            kvh = (2 * c) // GROUP
            qcol = qseq_scr[b, c] * scale
            for half in range(2):
                keep = (qlane < HEAD_DIM) if half == 0 else (qlane >= HEAD_DIM)
                q = jnp.where(keep, qcol, 0.0).astype(BF16)
                sc = lax.dot_general(q, k_dup[kvh], (((1,), (1,)), ((), ())),
                                     preferred_element_type=F32)
                score_blocks.append((2 * c + half, sc, v_dup[kvh]))
    ffn_up(2, 9)
    prob_blocks = []
    for head, sc, v in score_blocks:
        sc = sc + bias_ref[variant, head]
        sink = sinks_ref[head]
        m = jnp.maximum(jnp.max(sc, axis=-1, keepdims=True), sink)
        e = jnp.exp(sc - m)
        den = jnp.sum(e, axis=-1, keepdims=True) + jnp.exp(sink - m)
        prob_blocks.append(((e / den).astype(BF16), v))
    outs = [jnp.dot(p, v, preferred_element_type=F32) for p, v in prob_blocks]
    ffn_up(9, N_FF_CHUNKS)
    ffn_down()
    for b in range(MIX_BATCH):
        for c in range(Q_WIDTH // LANES):
            n = (b * (Q_WIDTH // LANES) + c) * 2
            oseq_scr[b, c] = jnp.where(qlane < HEAD_DIM, outs[n], outs[n + 1])
    for b in range(MIX_BATCH):
        for c in range(Q_WIDTH // LANES):
            attn_scr[c, pl.ds(b, steps, stride=MIX_BATCH), :] = oseq_scr[b, c]
    for c in range(Q_WIDTH // LANES):
        hn_scr[:, c * LANES:(c + 1) * LANES] = attn_scr[c].astype(BF16)

    first_step = CONV_HALO - (CONV_WIDTH - 1)
    for k in range(rows // ROW_CHUNK):
        acc = jnp.zeros((ROW_CHUNK, CONV_CHANNELS), F32)
        for tap in range(CONV_WIDTH):
            r0 = (k * STEPS_PER_CHUNK + first_step + tap) * MIX_BATCH
            acc = acc + xpad_scr[r0:r0 + ROW_CHUNK, :] * cw_ref[tap:tap + 1, :]
        y = acc + cb_ref[...]
        mu = jnp.mean(y, axis=-1, keepdims=True)
        yc = y - mu
        var = jnp.mean(yc * yc, axis=-1, keepdims=True)
        z = yc * lax.rsqrt(var + LN_EPS) * lng_ref[...] + lnb_ref[...]
        hn_scr[k * ROW_CHUNK:(k + 1) * ROW_CHUNK, Q_WIDTH:] = (z * _sigmoid(z)).astype(BF16)

    mixed = jnp.dot(hn_scr[...], wout_ref[...], preferred_element_type=F32)
    o_ref[...] = h1_prev[...] + mixed.reshape(steps, MIX_BATCH, D_MODEL)
    xpad_scr[0:halo_rows, :] = xpad_scr[rows:rows + halo_rows, :]

    h1_prev[...] = h1_new[...]


def _ffn_even_call(x, gain1, wg, wu, wd, layer, gain_mix, w_in, sinks, bias_tbl, conv_w, conv_b,
                   ln_g, ln_b, w_out, batch_major_in=False):
    if batch_major_in:
        bsz, seq, _ = x.shape
    else:
        seq, bsz, _ = x.shape
    rows = EVEN_STEPS * MIX_BATCH
    tiles_per_seq = seq // EVEN_STEPS
    n_tiles = (bsz // MIX_BATCH) * tiles_per_seq
    block = (EVEN_STEPS, MIX_BATCH, D_MODEL)

    def in_map(i):
        j = jnp.minimum(i, n_tiles - 1)
        if batch_major_in:
            return (j // tiles_per_seq, j % tiles_per_seq, 0)
        return (j % tiles_per_seq, j // tiles_per_seq, 0)

    def out_map(i):
        j = jnp.maximum(i - 1, 0)
        return (j % tiles_per_seq, j // tiles_per_seq, 0)

    in_block = (MIX_BATCH, EVEN_STEPS, D_MODEL) if batch_major_in else block
    return pl.pallas_call(
        functools.partial(_ffn_even_kernel, tiles_per_seq=tiles_per_seq,
                          batch_major_in=batch_major_in),
        out_shape=jax.ShapeDtypeStruct((seq, bsz, D_MODEL), F32),
        grid=(n_tiles + 1,),
        in_specs=[
            pl.BlockSpec(memory_space=pltpu.SMEM),
            pl.BlockSpec(in_block, in_map),
            _resident((1, D_MODEL)),
            _layer_weight(wg, layer),
            _layer_weight(wu, layer),
            _layer_weight(wd, layer),
            _resident((1, D_MODEL)),
            _resident(w_in.shape),
            _resident(bias_tbl.shape),
            _resident(conv_w.shape),
            _resident(conv_b.shape),
            _resident(ln_g.shape),
            _resident(ln_b.shape),
            _resident(w_out.shape),
        ],
        out_specs=pl.BlockSpec(block, out_map),
        scratch_shapes=[
            pltpu.VMEM(block, F32),
            pltpu.VMEM(block, F32),
            pltpu.VMEM((rows, D_MODEL), BF16),
            pltpu.VMEM((rows, D_FF), BF16),
            pltpu.VMEM((rows, D_MODEL), BF16),
            pltpu.VMEM(((Q_WIDTH + 2 * KV_WIDTH) // LANES, rows, LANES), F32),
            pltpu.VMEM((MIX_BATCH, (Q_WIDTH + 2 * KV_WIDTH) // LANES, EVEN_STEPS, LANES), F32),
            pltpu.VMEM((MIX_BATCH, 2, WINDOW, LANES), F32),
            pltpu.VMEM((MIX_BATCH, Q_WIDTH // LANES, EVEN_STEPS, LANES), F32),
            pltpu.VMEM((Q_WIDTH // LANES, rows, LANES), F32),
            pltpu.VMEM(((CONV_HALO + EVEN_STEPS) * MIX_BATCH, CONV_CHANNELS), F32),
        ],
        compiler_params=pltpu.CompilerParams(
            dimension_semantics=("arbitrary",), vmem_limit_bytes=FUSED_VMEM_LIMIT_BYTES),
        name="ffn_even",
    )(sinks, x, gain1, wg, wu, wd, gain_mix, w_in, bias_tbl, conv_w, conv_b, ln_g, ln_b, w_out)


def _t5_bucket_table(steps):
    qi = np.arange(steps)[:, None]
    sj = np.arange(WINDOW + steps)[None, :]
    dist = qi + WINDOW - sj
    n = np.maximum(dist, 0)
    max_exact = NUM_BUCKETS // 2
    nf = np.maximum(n, max_exact).astype(np.float32)
    large = max_exact + (np.log(nf / max_exact) / math.log(MAX_DISTANCE / max_exact)
                         * (NUM_BUCKETS - max_exact)).astype(np.int32)
    large = np.minimum(large, NUM_BUCKETS - 1)
    bucket = np.where(n < max_exact, n, large)
    in_window = (dist >= 0) & (dist < WINDOW)
    return bucket, in_window, sj


def _attention_bias_tables(rel_bias, steps):
    bucket, in_window, sj = _t5_bucket_table(steps)
    onehot = (jnp.asarray(bucket, jnp.int32)[..., None] == jnp.arange(NUM_BUCKETS)).astype(F32)
    bias = jnp.einsum("qsb,bh->hqs", onehot, rel_bias.astype(F32),
                      precision=lax.Precision.HIGHEST)
    tables = []
    for v in range(WINDOW // steps + 1):
        valid = in_window & (sj >= WINDOW - v * steps)
        tables.append(jnp.where(valid[None], bias, NEG_INF))
    return jnp.stack(tables)


def kernel(x, norm_ffn1, ffn1_wg, ffn1_wu, ffn1_wd, norm_mix, norm_ffn2, ffn2_wg, ffn2_wu, ffn2_wd,
           rel_bias, even_w_in, attn_sinks, conv_b_w, conv_b_b, conv_ln_g, conv_ln_b, even_w_out,
           odd_w_in, lru_conv_w, lru_conv_b, gate_a_w, gate_a_b, gate_x_w, gate_x_b, lru_lambda,
           odd_w_out, norm_final):
    bsz, seq, d = x.shape
    depth = norm_ffn1.shape[0]
    bias_tbl = _attention_bias_tables(rel_bias, EVEN_STEPS)
    row = lambda v: v.reshape(1, -1)

    bf16 = lambda w: w.astype(BF16)
    ffn1 = (bf16(ffn1_wg), bf16(ffn1_wu), bf16(ffn1_wd))
    ffn2 = (bf16(ffn2_wg), bf16(ffn2_wu), bf16(ffn2_wd))

    h = x
    for layer in range(depth):
        if layer % 2 == 0:
            e = layer // 2
            h = _ffn_even_call(h, row(norm_ffn1[layer]), *ffn1, layer, row(norm_mix[layer]),
                               even_w_in[e].astype(BF16), attn_sinks[e], bias_tbl, conv_b_w[e],
                               row(conv_b_b[e]), row(conv_ln_g[e]), row(conv_ln_b[e]),
                               even_w_out[e].astype(BF16),
                               batch_major_in=(layer == 0))
        else:
            o = layer // 2
            w_gate = jnp.concatenate([gate_a_w[o], gate_x_w[o]], axis=-1).astype(BF16)
            h = _ffn_odd_call(h, row(norm_ffn1[layer]), *ffn1, layer, row(norm_mix[layer]),
                              odd_w_in[o].astype(BF16), lru_conv_w[o], row(lru_conv_b[o]), w_gate,
                              row(gate_a_b[o]), row(gate_x_b[o]), row(lru_lambda[o]),
                              odd_w_out[o].astype(BF16))
        last = layer == depth - 1
        h = _ffn_call(h, row(norm_ffn2[layer]), *ffn2, layer,
                      final_gain=row(norm_final) if last else None)
    return h
```

```python
import functools
import math

import jax
import jax.numpy as jnp
import numpy as np
from jax import lax
from jax.experimental import pallas as pl
from jax.experimental.pallas import tpu as pltpu

D_MODEL = 1024
D_FF = 2816
HEAD_DIM = 64
N_Q_HEADS = 8
N_KV_HEADS = 2
GROUP = N_Q_HEADS // N_KV_HEADS
WINDOW = 128
Q_WIDTH = N_Q_HEADS * HEAD_DIM
KV_WIDTH = N_KV_HEADS * HEAD_DIM
NUM_BUCKETS = 32
MAX_DISTANCE = 128
CONV_CHANNELS = 512
CONV_WIDTH = 31
LRU_WIDTH = 1024
LRU_HEADS = 8
LRU_BLOCK = 128
LRU_CONV_WIDTH = 4
RG_LRU_C = 8.0
RMS_EPS = 1e-6
LN_EPS = 1e-5
NEG_INF = -1e30

LANES = 128
SUBLANES = 8
MXU_DIM = 256
VMEM_LIMIT_BYTES = 56 * 1024 * 1024
FUSED_VMEM_LIMIT_BYTES = 60 * 1024 * 1024

FF_CHUNK = MXU_DIM
N_FF_CHUNKS = D_FF // FF_CHUNK
FFN_ROWS = 1024
ROW_CHUNK = 64
MIX_BATCH = SUBLANES
STEPS_PER_CHUNK = ROW_CHUNK // MIX_BATCH
EVEN_STEPS = 64
ODD_STEPS = 64
FINAL_STEPS = 128
CONV_HALO = 32
LRU_HALO = LRU_CONV_WIDTH - 1
SCAN_ROWS = 2 * MIX_BATCH
CONV_LANES = 2 * LANES

BF16 = jnp.bfloat16
F32 = jnp.float32


def _resident(shape):
    nd = len(shape)
    return pl.BlockSpec(shape, lambda *_: (0,) * nd, pipeline_mode=pl.Buffered(1))


def _sigmoid(x):
    return 0.5 * jnp.tanh(0.5 * x) + 0.5


def _silu_mul(g, u):
    half = 0.5 * g
    return (half + half * jnp.tanh(half)) * u


def _rmsnorm_rows(x, g):
    ms = jnp.mean(x * x, axis=-1, keepdims=True)
    return x * lax.rsqrt(ms + RMS_EPS) * g


def _norm_tile(h_ref, g_ref, hn_scr):
    steps = h_ref.shape[0]
    for i in range(steps // STEPS_PER_CHUNK):
        x = h_ref[i * STEPS_PER_CHUNK:(i + 1) * STEPS_PER_CHUNK].reshape(ROW_CHUNK, D_MODEL)
        hn_scr[i * ROW_CHUNK:(i + 1) * ROW_CHUNK, :] = _rmsnorm_rows(x, g_ref[...]).astype(BF16)


def _ffn_kernel(*refs, final_norm):
    if final_norm:
        (x_ref, xnext_ref, g_ref, wg_ref, wu_ref, wd_ref, gfin_ref, o_ref,
         xn_scr, a_scr, res_scr) = refs
    else:
        x_ref, xnext_ref, g_ref, wg_ref, wu_ref, wd_ref, o_ref, xn_scr, a_scr = refs
    rows = xn_scr.shape[1]
    i = pl.program_id(0)
    slot = lax.rem(i, 2)

    def norm_chunk(src_ref, dst_slot, k):
        if final_norm:
            x = src_ref[k * STEPS_PER_CHUNK:(k + 1) * STEPS_PER_CHUNK].reshape(ROW_CHUNK, D_MODEL)
        else:
            x = src_ref[k * ROW_CHUNK:(k + 1) * ROW_CHUNK, :]
        xn_scr[dst_slot, k * ROW_CHUNK:(k + 1) * ROW_CHUNK, :] = (
            _rmsnorm_rows(x, g_ref[...]).astype(BF16))

    n_norm = rows // ROW_CHUNK
    norm_per_dot = pl.cdiv(n_norm, N_FF_CHUNKS)

    @pl.when(i == 0)
    def _():
        for k in range(n_norm):
            norm_chunk(x_ref, 0, k)

    for c in range(N_FF_CHUNKS):
        cols = slice(c * FF_CHUNK, (c + 1) * FF_CHUNK)
        g = jnp.dot(xn_scr[slot], wg_ref[:, cols], preferred_element_type=F32)
        u = jnp.dot(xn_scr[slot], wu_ref[:, cols], preferred_element_type=F32)
        a_scr[:, cols] = _silu_mul(g, u).astype(BF16)
        for k in range(c * norm_per_dot, min((c + 1) * norm_per_dot, n_norm)):
            norm_chunk(xnext_ref, 1 - slot, k)
    y = jnp.dot(a_scr[...], wd_ref[...], preferred_element_type=F32)
    if final_norm:
        res_scr[...] = _rmsnorm_rows(x_ref[...] + 0.5 * y.reshape(res_scr.shape), gfin_ref[...])
        for b in range(o_ref.shape[0]):
            o_ref[b] = res_scr[:, b, :]
    else:
        o_ref[...] = x_ref[...] + 0.5 * y


def _layer_weight(w, layer):
    return pl.BlockSpec((None,) + w.shape[1:], lambda *_: (layer, 0, 0),
                        pipeline_mode=pl.Buffered(1))


def _ffn_call(x, gain, wg, wu, wd, layer, final_gain=None):
    seq, bsz, _ = x.shape
    final_norm = final_gain is not None
    if final_norm:
        tile_rows = FINAL_STEPS * MIX_BATCH
        n_st = seq // FINAL_STEPS
        n_tiles = (bsz // MIX_BATCH) * n_st
        block = (FINAL_STEPS, MIX_BATCH, D_MODEL)
        x_spec = pl.BlockSpec(block, lambda i: (i % n_st, i // n_st, 0))

        def next_map(i):
            j = jnp.minimum(i + 1, n_tiles - 1)
            return (j % n_st, j // n_st, 0)

        xnext_spec = pl.BlockSpec(block, next_map)
        out_shape = jax.ShapeDtypeStruct((bsz, seq, D_MODEL), F32)
        out_spec = pl.BlockSpec((MIX_BATCH, FINAL_STEPS, D_MODEL),
                                lambda i: (i // n_st, i % n_st, 0))
        operands = (x, x, gain, wg, wu, wd, final_gain)
        extra_specs = [_resident((1, D_MODEL))]
        extra_scratch = [pltpu.VMEM(block, F32)]
    else:
        tile_rows = FFN_ROWS
        x = x.reshape(seq * bsz, D_MODEL)
        n_tiles = x.shape[0] // FFN_ROWS
        x_spec = pl.BlockSpec((FFN_ROWS, D_MODEL), lambda i: (i, 0))
        xnext_spec = pl.BlockSpec((FFN_ROWS, D_MODEL),
                                  lambda i: (jnp.minimum(i + 1, n_tiles - 1), 0))
        out_shape = jax.ShapeDtypeStruct(x.shape, F32)
        out_spec = pl.BlockSpec((FFN_ROWS, D_MODEL), lambda i: (i, 0))
        operands = (x, x, gain, wg, wu, wd)
        extra_specs = []
        extra_scratch = []
    out = pl.pallas_call(
        functools.partial(_ffn_kernel, final_norm=final_norm),
        out_shape=out_shape,
        grid=(n_tiles,),
        in_specs=[
            x_spec,
            xnext_spec,
            _resident((1, D_MODEL)),
            _layer_weight(wg, layer),
            _layer_weight(wu, layer),
            _layer_weight(wd, layer),
        ] + extra_specs,
        out_specs=out_spec,
        scratch_shapes=[
            pltpu.VMEM((2, tile_rows, D_MODEL), BF16),
            pltpu.VMEM((tile_rows, D_FF), BF16),
        ] + extra_scratch,
        compiler_params=pltpu.CompilerParams(
            dimension_semantics=("arbitrary",), vmem_limit_bytes=VMEM_LIMIT_BYTES),
        name="ffn_final" if final_norm else "ffn",
    )(*operands)
    return out if final_norm else out.reshape(seq, bsz, D_MODEL)


def _ffn_odd_kernel(x_ref, g1_ref, wg_ref, wu_ref, wd_ref, gm_ref, win_ref, cw_ref, cb_ref,
                    wgate_ref, gab_ref, gxb_ref, lam_ref, wout_ref, o_ref,
                    h1_new, h1_prev, xn_scr, a_scr, hn_scr, ug_scr, xpad_scr, rec_scr, gpre_scr,
                    state_scr, *, tiles_per_seq):
    steps = x_ref.shape[0]
    rows = steps * MIX_BATCH
    halo_rows = LRU_HALO * MIX_BATCH
    i = pl.program_id(0)

    @pl.when(i == 0)
    def _():
        h1_prev[...] = jnp.zeros(h1_prev.shape, F32)

    @pl.when(jnp.logical_or(i == 0, lax.rem(i + tiles_per_seq - 1, tiles_per_seq) == 0))
    def _():
        xpad_scr[0:halo_rows, :] = jnp.zeros((halo_rows, LRU_WIDTH), F32)
        state_scr[...] = jnp.zeros((MIX_BATCH, LRU_WIDTH), F32)

    def ffn_up(c0, c1):
        for c in range(c0, c1):
            cols = slice(c * FF_CHUNK, (c + 1) * FF_CHUNK)
            g = jnp.dot(xn_scr[...], wg_ref[:, cols], preferred_element_type=F32)
            u = jnp.dot(xn_scr[...], wu_ref[:, cols], preferred_element_type=F32)
            a_scr[:, cols] = _silu_mul(g, u).astype(BF16)

    for k in range(rows // ROW_CHUNK):
        x = x_ref[k * STEPS_PER_CHUNK:(k + 1) * STEPS_PER_CHUNK].reshape(ROW_CHUNK, D_MODEL)
        xn_scr[k * ROW_CHUNK:(k + 1) * ROW_CHUNK, :] = _rmsnorm_rows(x, g1_ref[...]).astype(BF16)
    for k in range(rows // ROW_CHUNK):
        r = slice(k * ROW_CHUNK, (k + 1) * ROW_CHUNK)
        hn_scr[r, :] = _rmsnorm_rows(h1_prev[r, :], gm_ref[...]).astype(BF16)
    ffn_up(0, 2)
    ug_scr[...] = jnp.dot(hn_scr[...], win_ref[:, 0:LRU_WIDTH], preferred_element_type=F32)
    xpad_scr[halo_rows:, :] = jnp.dot(hn_scr[...], win_ref[:, LRU_WIDTH:],
                                      preferred_element_type=F32)
    ffn_up(2, 7)

    for k in range(rows // ROW_CHUNK):
        acc = jnp.zeros((ROW_CHUNK, LRU_WIDTH), F32)
        for tap in range(LRU_CONV_WIDTH):
            r0 = (k * STEPS_PER_CHUNK + tap) * MIX_BATCH
            acc = acc + xpad_scr[r0:r0 + ROW_CHUNK, :] * cw_ref[tap:tap + 1, :]
        rec_scr[k * ROW_CHUNK:(k + 1) * ROW_CHUNK, :] = acc + cb_ref[...]

    for hd in range(LRU_HEADS):
        cs = slice(hd * LRU_BLOCK, (hd + 1) * LRU_BLOCK)
        pre = jnp.dot(rec_scr[:, cs].astype(BF16), wgate_ref[hd], preferred_element_type=F32)
        gpre_scr[:, cs] = pre[:, :LRU_BLOCK]
        gpre_scr[:, LRU_WIDTH + hd * LRU_BLOCK:LRU_WIDTH + (hd + 1) * LRU_BLOCK] = pre[:, LRU_BLOCK:]

    ffn_up(7, N_FF_CHUNKS)
    y = jnp.dot(a_scr[...], wd_ref[...], preferred_element_type=F32)
    h1_new[...] = x_ref[...].reshape(rows, D_MODEL) + 0.5 * y

    lam = lam_ref[...]
    log_sig_lam = -(jnp.maximum(-lam, 0.0) + jnp.log1p(jnp.exp(-jnp.abs(lam))))
    c_log_sig_lam = RG_LRU_C * log_sig_lam

    hprev = state_scr[...]
    for k in range(rows // SCAN_ROWS):
        r = slice(k * SCAN_ROWS, (k + 1) * SCAN_ROWS)
        x = rec_scr[r, :]
        rg = _sigmoid(gpre_scr[r, 0:LRU_WIDTH] + gab_ref[...])
        ig = _sigmoid(gpre_scr[r, LRU_WIDTH:] + gxb_ref[...])
        log_a = rg * c_log_sig_lam
        a = jnp.exp(log_a)
        bx = jnp.sqrt(-jnp.tanh(log_a) * (a * a + 1.0)) * (ig * x)
        hs = []
        for t in range(SCAN_ROWS // MIX_BATCH):
            rt = slice(t * MIX_BATCH, (t + 1) * MIX_BATCH)
            hprev = a[rt, :] * hprev + bx[rt, :]
            hs.append(hprev)
        hs = jnp.concatenate(hs, axis=0)
        hn_scr[r, :] = (jax.nn.gelu(ug_scr[r, :]) * hs).astype(BF16)
    state_scr[...] = hprev

    mixed = jnp.dot(hn_scr[...], wout_ref[...], preferred_element_type=F32)
    o_ref[...] = (h1_prev[...] + mixed).reshape(steps, MIX_BATCH, D_MODEL)
    xpad_scr[0:halo_rows, :] = xpad_scr[rows:rows + halo_rows, :]

    h1_prev[...] = h1_new[...]


def _ffn_odd_call(h, gain1, wg, wu, wd, layer, gain_mix, w_in, conv_w, conv_b, w_gate, ga_b, gx_b,
                  lam, w_out):
    seq, bsz, _ = h.shape
    rows = ODD_STEPS * MIX_BATCH
    tiles_per_seq = seq // ODD_STEPS
    n_tiles = (bsz // MIX_BATCH) * tiles_per_seq
    block = (ODD_STEPS, MIX_BATCH, D_MODEL)

    def in_map(i):
        j = jnp.minimum(i, n_tiles - 1)
        return (j % tiles_per_seq, j // tiles_per_seq, 0)

    def out_map(i):
        j = jnp.maximum(i - 1, 0)
        return (j % tiles_per_seq, j // tiles_per_seq, 0)

    return pl.pallas_call(
        functools.partial(_ffn_odd_kernel, tiles_per_seq=tiles_per_seq),
        out_shape=jax.ShapeDtypeStruct(h.shape, F32),
        grid=(n_tiles + 1,),
        in_specs=[
            pl.BlockSpec(block, in_map),
            _resident((1, D_MODEL)),
            _layer_weight(wg, layer),
            _layer_weight(wu, layer),
            _layer_weight(wd, layer),
            _resident((1, D_MODEL)),
            _resident(w_in.shape),
            _resident(conv_w.shape),
            _resident(conv_b.shape),
            _resident(w_gate.shape),
            _resident(ga_b.shape),
            _resident(gx_b.shape),
            _resident(lam.shape),
            _resident(w_out.shape),
        ],
        out_specs=pl.BlockSpec(block, out_map),
        scratch_shapes=[
            pltpu.VMEM((rows, D_MODEL), F32),
            pltpu.VMEM((rows, D_MODEL), F32),
            pltpu.VMEM((rows, D_MODEL), BF16),
            pltpu.VMEM((rows, D_FF), BF16),
            pltpu.VMEM((rows, D_MODEL), BF16),
            pltpu.VMEM((rows, LRU_WIDTH), F32),
            pltpu.VMEM(((LRU_HALO + ODD_STEPS) * MIX_BATCH, LRU_WIDTH), F32),
            pltpu.VMEM((rows, LRU_WIDTH), F32),
            pltpu.VMEM((rows, 2 * LRU_WIDTH), F32),
            pltpu.VMEM((MIX_BATCH, LRU_WIDTH), F32),
        ],
        compiler_params=pltpu.CompilerParams(
            dimension_semantics=("arbitrary",), vmem_limit_bytes=FUSED_VMEM_LIMIT_BYTES),
        name="ffn_odd",
    )(h, gain1, wg, wu, wd, gain_mix, w_in, conv_w, conv_b, w_gate, ga_b, gx_b, lam, w_out)


def _ffn_even_kernel(sinks_ref, x_ref, g1_ref, wg_ref, wu_ref, wd_ref, gm_ref, win_ref, bias_ref,
                     cw_ref, cb_ref, lng_ref, lnb_ref, wout_ref, o_ref,
                     h1_new, h1_prev, xn_scr, a_scr, hn_scr, qkv_scr, qseq_scr, kvprev_scr, oseq_scr,
                     attn_scr, xpad_scr, *, tiles_per_seq, batch_major_in):
    steps = o_ref.shape[0]
    rows = steps * MIX_BATCH
    halo_rows = CONV_HALO * MIX_BATCH
    n_keys = WINDOW + steps
    i = pl.program_id(0)
    seq_tile = lax.rem(i + tiles_per_seq - 1, tiles_per_seq)

    @pl.when(i == 0)
    def _():
        h1_prev[...] = jnp.zeros(h1_prev.shape, F32)

    @pl.when(jnp.logical_or(i == 0, seq_tile == 0))
    def _():
        kvprev_scr[...] = jnp.zeros(kvprev_scr.shape, F32)
        xpad_scr[0:halo_rows, :] = jnp.zeros((halo_rows, CONV_CHANNELS), F32)

    def ffn_norm():
        for k in range(rows // ROW_CHUNK):
            if batch_major_in:
                r0 = k * ROW_CHUNK
                x = x_ref[r0 // steps, r0 % steps:r0 % steps + ROW_CHUNK, :]
            else:
                x = x_ref[k * STEPS_PER_CHUNK:(k + 1) * STEPS_PER_CHUNK].reshape(ROW_CHUNK,
                                                                                 D_MODEL)
            xn_scr[k * ROW_CHUNK:(k + 1) * ROW_CHUNK, :] = (
                _rmsnorm_rows(x, g1_ref[...]).astype(BF16))

    def ffn_up(c0, c1):
        for c in range(c0, c1):
            cols = slice(c * FF_CHUNK, (c + 1) * FF_CHUNK)
            g = jnp.dot(xn_scr[...], wg_ref[:, cols], preferred_element_type=F32)
            u = jnp.dot(xn_scr[...], wu_ref[:, cols], preferred_element_type=F32)
            a_scr[:, cols] = _silu_mul(g, u).astype(BF16)

    def ffn_down():
        y = jnp.dot(a_scr[...], wd_ref[...], preferred_element_type=F32)
        if batch_major_in:
            for b in range(MIX_BATCH):
                h1_new[:, b, :] = x_ref[b] + 0.5 * y[b * steps:(b + 1) * steps, :]
        else:
            h1_new[...] = x_ref[...] + 0.5 * y.reshape(steps, MIX_BATCH, D_MODEL)

    ffn_norm()
    _norm_tile(h1_prev, gm_ref, hn_scr)
    ffn_up(0, 2)

    for j in range((Q_WIDTH + 2 * KV_WIDTH) // MXU_DIM):
        res = jnp.dot(hn_scr[...], win_ref[:, j * MXU_DIM:(j + 1) * MXU_DIM],
                      preferred_element_type=F32)
        qkv_scr[2 * j] = res[:, :LANES]
        qkv_scr[2 * j + 1] = res[:, LANES:]
    o_g = Q_WIDTH + 2 * KV_WIDTH
    glu_a = jnp.dot(hn_scr[...], win_ref[:, o_g:o_g + CONV_CHANNELS], preferred_element_type=F32)
    glu_b = jnp.dot(hn_scr[...], win_ref[:, o_g + CONV_CHANNELS:], preferred_element_type=F32)
    xpad_scr[halo_rows:, :] = glu_a * _sigmoid(glu_b)

    k_slab = Q_WIDTH // LANES
    v_slab = k_slab + 1
    lane = lax.broadcasted_iota(jnp.int32, (n_keys, LANES), 1)
    qlane = lax.broadcasted_iota(jnp.int32, (steps, LANES), 1)
    scale = 1.0 / math.sqrt(HEAD_DIM)
    variant = jnp.minimum(seq_tile, WINDOW // steps)
    for b in range(MIX_BATCH):
        for j in range(qkv_scr.shape[0]):
            qseq_scr[b, j] = qkv_scr[j, pl.ds(b, steps, stride=MIX_BATCH), :]
    score_blocks = []
    for b in range(MIX_BATCH):
        k_cat = jnp.concatenate([kvprev_scr[b, 0], qseq_scr[b, k_slab]], axis=0)
        v_cat = jnp.concatenate([kvprev_scr[b, 1], qseq_scr[b, v_slab]], axis=0)
        kvprev_scr[b, 0] = k_cat[steps:, :]
        kvprev_scr[b, 1] = v_cat[steps:, :]
        k_sw = pltpu.roll(k_cat, HEAD_DIM, axis=1)
        v_sw = pltpu.roll(v_cat, HEAD_DIM, axis=1)
        k_dup = (jnp.where(lane < HEAD_DIM, k_cat, k_sw).astype(BF16),
                 jnp.where(lane < HEAD_DIM, k_sw, k_cat).astype(BF16))
        v_dup = (jnp.where(lane < HEAD_DIM, v_cat, v_sw).astype(BF16),
                 jnp.where(lane < HEAD_DIM, v_sw, v_cat).astype(BF16))
        for c in range(Q_WIDTH // LANES):
            kvh = (2 * c) // GROUP
            qcol = qseq_scr[b, c] * scale
            for half in range(2):
                keep = (qlane < HEAD_DIM) if half == 0 else (qlane >= HEAD_DIM)
                q = jnp.where(keep, qcol, 0.0).astype(BF16)
                sc = lax.dot_general(q, k_dup[kvh], (((1,), (1,)), ((), ())),
                                     preferred_element_type=F32)
                score_blocks.append((2 * c + half, sc, v_dup[kvh]))
    ffn_up(2, 9)
    prob_blocks = []
    for head, sc, v in score_blocks:
        sc = sc + bias_ref[variant, head]
        sink = sinks_ref[head]
        m = jnp.maximum(jnp.max(sc, axis=-1, keepdims=True), sink)
        e = jnp.exp(sc - m)
        den = jnp.sum(e, axis=-1, keepdims=True) + jnp.exp(sink - m)
        prob_blocks.append(((e / den).astype(BF16), v))
    outs = [jnp.dot(p, v, preferred_element_type=F32) for p, v in prob_blocks]
    ffn_up(9, N_FF_CHUNKS)
    ffn_down()
    for b in range(MIX_BATCH):
        for c in range(Q_WIDTH // LANES):
            n = (b * (Q_WIDTH // LANES) + c) * 2
            oseq_scr[b, c] = jnp.where(qlane < HEAD_DIM, outs[n], outs[n + 1])
    for b in range(MIX_BATCH):
        for c in range(Q_WIDTH // LANES):
            attn_scr[c, pl.ds(b, steps, stride=MIX_BATCH), :] = oseq_scr[b, c]
    for c in range(Q_WIDTH // LANES):
        hn_scr[:, c * LANES:(c + 1) * LANES] = attn_scr[c].astype(BF16)

    first_step = CONV_HALO - (CONV_WIDTH - 1)
    for k in range(rows // ROW_CHUNK):
        halves = []
        for c0 in range(0, CONV_CHANNELS, CONV_LANES):
            cs = slice(c0, c0 + CONV_LANES)
            acc = jnp.zeros((ROW_CHUNK, CONV_LANES), F32)
            for tap in range(CONV_WIDTH):
                r0 = (k * STEPS_PER_CHUNK + first_step + tap) * MIX_BATCH
                acc = acc + xpad_scr[r0:r0 + ROW_CHUNK, cs] * cw_ref[tap:tap + 1, cs]
            halves.append(acc + cb_ref[:, cs])
        y = jnp.concatenate(halves, axis=-1)
        mu = jnp.mean(y, axis=-1, keepdims=True)
        yc = y - mu
        var = jnp.mean(yc * yc, axis=-1, keepdims=True)
        z = yc * lax.rsqrt(var + LN_EPS) * lng_ref[...] + lnb_ref[...]
        hn_scr[k * ROW_CHUNK:(k + 1) * ROW_CHUNK, Q_WIDTH:] = (z * _sigmoid(z)).astype(BF16)

    mixed = jnp.dot(hn_scr[...], wout_ref[...], preferred_element_type=F32)
    o_ref[...] = h1_prev[...] + mixed.reshape(steps, MIX_BATCH, D_MODEL)
    xpad_scr[0:halo_rows, :] = xpad_scr[rows:rows + halo_rows, :]

    h1_prev[...] = h1_new[...]


def _ffn_even_call(x, gain1, wg, wu, wd, layer, gain_mix, w_in, sinks, bias_tbl, conv_w, conv_b,
                   ln_g, ln_b, w_out, batch_major_in=False):
    if batch_major_in:
        bsz, seq, _ = x.shape
    else:
        seq, bsz, _ = x.shape
    rows = EVEN_STEPS * MIX_BATCH
    tiles_per_seq = seq // EVEN_STEPS
    n_tiles = (bsz // MIX_BATCH) * tiles_per_seq
    block = (EVEN_STEPS, MIX_BATCH, D_MODEL)

    def in_map(i):
        j = jnp.minimum(i, n_tiles - 1)
        if batch_major_in:
            return (j // tiles_per_seq, j % tiles_per_seq, 0)
        return (j % tiles_per_seq, j // tiles_per_seq, 0)

    def out_map(i):
        j = jnp.maximum(i - 1, 0)
        return (j % tiles_per_seq, j // tiles_per_seq, 0)

    in_block = (MIX_BATCH, EVEN_STEPS, D_MODEL) if batch_major_in else block
    return pl.pallas_call(
        functools.partial(_ffn_even_kernel, tiles_per_seq=tiles_per_seq,
                          batch_major_in=batch_major_in),
        out_shape=jax.ShapeDtypeStruct((seq, bsz, D_MODEL), F32),
        grid=(n_tiles + 1,),
        in_specs=[
            pl.BlockSpec(memory_space=pltpu.SMEM),
            pl.BlockSpec(in_block, in_map),
            _resident((1, D_MODEL)),
            _layer_weight(wg, layer),
            _layer_weight(wu, layer),
            _layer_weight(wd, layer),
            _resident((1, D_MODEL)),
            _resident(w_in.shape),
            _resident(bias_tbl.shape),
            _resident(conv_w.shape),
            _resident(conv_b.shape),
            _resident(ln_g.shape),
            _resident(ln_b.shape),
            _resident(w_out.shape),
        ],
        out_specs=pl.BlockSpec(block, out_map),
        scratch_shapes=[
            pltpu.VMEM(block, F32),
            pltpu.VMEM(block, F32),
            pltpu.VMEM((rows, D_MODEL), BF16),
            pltpu.VMEM((rows, D_FF), BF16),
            pltpu.VMEM((rows, D_MODEL), BF16),
            pltpu.VMEM(((Q_WIDTH + 2 * KV_WIDTH) // LANES, rows, LANES), F32),
            pltpu.VMEM((MIX_BATCH, (Q_WIDTH + 2 * KV_WIDTH) // LANES, EVEN_STEPS, LANES), F32),
            pltpu.VMEM((MIX_BATCH, 2, WINDOW, LANES), F32),
            pltpu.VMEM((MIX_BATCH, Q_WIDTH // LANES, EVEN_STEPS, LANES), F32),
            pltpu.VMEM((Q_WIDTH // LANES, rows, LANES), F32),
            pltpu.VMEM(((CONV_HALO + EVEN_STEPS) * MIX_BATCH, CONV_CHANNELS), F32),
        ],
        compiler_params=pltpu.CompilerParams(
            dimension_semantics=("arbitrary",), vmem_limit_bytes=FUSED_VMEM_LIMIT_BYTES),
        name="ffn_even",
    )(sinks, x, gain1, wg, wu, wd, gain_mix, w_in, bias_tbl, conv_w, conv_b, ln_g, ln_b, w_out)


def _t5_bucket_table(steps):
    qi = np.arange(steps)[:, None]
    sj = np.arange(WINDOW + steps)[None, :]
    dist = qi + WINDOW - sj
    n = np.maximum(dist, 0)
    max_exact = NUM_BUCKETS // 2
    nf = np.maximum(n, max_exact).astype(np.float32)
    large = max_exact + (np.log(nf / max_exact) / math.log(MAX_DISTANCE / max_exact)
                         * (NUM_BUCKETS - max_exact)).astype(np.int32)
    large = np.minimum(large, NUM_BUCKETS - 1)
    bucket = np.where(n < max_exact, n, large)
    in_window = (dist >= 0) & (dist < WINDOW)
    return bucket, in_window, sj


def _attention_bias_tables(rel_bias, steps):
    bucket, in_window, sj = _t5_bucket_table(steps)
    onehot = (jnp.asarray(bucket, jnp.int32)[..., None] == jnp.arange(NUM_BUCKETS)).astype(F32)
    bias = jnp.einsum("qsb,bh->hqs", onehot, rel_bias.astype(F32),
                      precision=lax.Precision.HIGHEST)
    tables = []
    for v in range(WINDOW // steps + 1):
        valid = in_window & (sj >= WINDOW - v * steps)
        tables.append(jnp.where(valid[None], bias, NEG_INF))
    return jnp.stack(tables)


def kernel(x, norm_ffn1, ffn1_wg, ffn1_wu, ffn1_wd, norm_mix, norm_ffn2, ffn2_wg, ffn2_wu, ffn2_wd,
           rel_bias, even_w_in, attn_sinks, conv_b_w, conv_b_b, conv_ln_g, conv_ln_b, even_w_out,
           odd_w_in, lru_conv_w, lru_conv_b, gate_a_w, gate_a_b, gate_x_w, gate_x_b, lru_lambda,
           odd_w_out, norm_final):
    bsz, seq, d = x.shape
    depth = norm_ffn1.shape[0]
    bias_tbl = _attention_bias_tables(rel_bias, EVEN_STEPS)
    row = lambda v: v.reshape(1, -1)

    bf16 = lambda w: w.astype(BF16)
    ffn1 = (bf16(ffn1_wg), bf16(ffn1_wu), bf16(ffn1_wd))
    ffn2 = (bf16(ffn2_wg), bf16(ffn2_wu), bf16(ffn2_wd))

    h = x
    for layer in range(depth):
        if layer % 2 == 0:
            e = layer // 2
            h = _ffn_even_call(h, row(norm_ffn1[layer]), *ffn1, layer, row(norm_mix[layer]),
                               even_w_in[e].astype(BF16), attn_sinks[e], bias_tbl, conv_b_w[e],
                               row(conv_b_b[e]), row(conv_ln_g[e]), row(conv_ln_b[e]),
                               even_w_out[e].astype(BF16),
                               batch_major_in=(layer == 0))
        else:
            o = layer // 2
            w_gate = jnp.concatenate([gate_a_w[o], gate_x_w[o]], axis=-1).astype(BF16)
            h = _ffn_odd_call(h, row(norm_ffn1[layer]), *ffn1, layer, row(norm_mix[layer]),
                              odd_w_in[o].astype(BF16), lru_conv_w[o], row(lru_conv_b[o]), w_gate,
                              row(gate_a_b[o]), row(gate_x_b[o]), row(lru_lambda[o]),
                              odd_w_out[o].astype(BF16))
        last = layer == depth - 1
        h = _ffn_call(h, row(norm_ffn2[layer]), *ffn2, layer,
                      final_gain=row(norm_final) if last else None)
    return h
```
